```python
import math
import jax, jax.numpy as jnp
from jax import lax
import numpy as np


D_MODEL = 2048
BATCH = 16
SEQ = 2048
DEPTH = 2

GRID_W = 64
CTX_LEN = 256
EPS = 1e-6
F32 = jnp.float32

NA_HEADS = 16
NA_HEAD_DIM = 64
NA_WIDTH = NA_HEADS * NA_HEAD_DIM
WIN_R = 8
WIN_C = 16
COL_BLOCK = 16
COL_BAND = 32

SSM_HEADS = 16
SSM_HEAD_DIM = 64
SSM_INNER = SSM_HEADS * SSM_HEAD_DIM
SSM_GROUPS = 2
SSM_STATE = 128
SSM_XBC = SSM_INNER + 2 * SSM_GROUPS * SSM_STATE
SSM_CONV = 5
SSM_CHUNK = 128

CONV_CH = 1024
CONV_K = 31

N_BRANCH = 3
IN_COLS = 3 * NA_WIDTH + SSM_INNER + SSM_XBC + 2 * SSM_HEADS + 2 * CONV_CH + N_BRANCH * D_MODEL

PEER_HEADS = 8
PEER_NKEYS = 128
PEER_KEY_DIM = 128
PEER_TOPK = 16
PEER_EXPERTS = PEER_NKEYS * PEER_NKEYS
PEER_CHUNK = 128

kernel_name = "hybrid_na_ssd_conformer_peer_dit"


def rmsnorm(x, g):
    xf = x.astype(F32)
    y = xf * lax.rsqrt(jnp.mean(xf * xf, axis=-1, keepdims=True) + EPS)
    return (y * g.astype(F32)).astype(x.dtype)


def layernorm(x, g, b):
    xf = x.astype(F32)
    mu = jnp.mean(xf, axis=-1, keepdims=True)
    xc = xf - mu
    y = xc * lax.rsqrt(jnp.mean(xc * xc, axis=-1, keepdims=True) + EPS)
    return (y * g.astype(F32) + b.astype(F32)).astype(x.dtype)


def modulate(x, g, shift, scale):
    return rmsnorm(x, g) * (1 + scale) + shift


def dwconv(x, w, b):
    k = w.shape[0]
    y = lax.conv_general_dilated(x, w[:, None, :].astype(x.dtype), window_strides=(1,),
                                 padding=[((k - 1) // 2, (k - 1) // 2)],
                                 dimension_numbers=('NWC', 'WIO', 'NWC'),
                                 feature_group_count=x.shape[-1])
    return y + b


def split_cols(p):
    sizes = (NA_WIDTH, NA_WIDTH, NA_WIDTH, SSM_INNER, SSM_XBC, 2 * SSM_HEADS, 2 * CONV_CH, N_BRANCH * D_MODEL)
    idx = np.cumsum(sizes)[:-1].tolist()
    return jnp.split(p, idx, axis=-1)


def to_heads(t):
    return t.reshape(t.shape[0], t.shape[1], NA_HEADS, NA_HEAD_DIM)


def na_static(rows):
    wr = min(WIN_R, rows)
    nb = GRID_W // COL_BLOCK
    j = np.arange(GRID_W)
    cstart = np.clip(j - WIN_C // 2, 0, GRID_W - WIN_C)
    bstart = np.clip(np.arange(nb) * COL_BLOCK - WIN_C // 2, 0, GRID_W - COL_BAND)
    band_cols = bstart[:, None] + np.arange(COL_BAND)[None, :]
    qcols = j.reshape(nb, COL_BLOCK)
    qcs = cstart.reshape(nb, COL_BLOCK)
    kcol = band_cols[:, None, :]
    mask = (kcol >= qcs[..., None]) & (kcol < qcs[..., None] + WIN_C)
    col_off = np.clip(kcol - qcols[..., None], -(WIN_C - 1), WIN_C - 1) + WIN_C - 1
    return wr, nb, band_cols, mask, col_off


def neighbourhood_attention(q, k, v, kc, vc, rpb):
    b, s, h, dh = q.shape
    rows = s // GRID_W
    wr, nb, band_cols, mask, col_off = na_static(rows)
    qg = (q * (dh ** -0.5)).reshape(b, rows, nb, COL_BLOCK, h, dh)
    kg = k.reshape(b, rows, GRID_W, h, dh)
    vg = v.reshape(b, rows, GRID_W, h, dh)
    col_bias = rpb.astype(F32)[:, :, col_off]
    win_mask = mask[:, :, None, :]

    def row_block(r):
        rs = jnp.clip(r - wr // 2, 0, rows - wr)
        kb = lax.dynamic_slice_in_dim(kg, rs, wr, axis=1)[:, :, band_cols]
        vb = lax.dynamic_slice_in_dim(vg, rs, wr, axis=1)[:, :, band_cols]
        qr = lax.dynamic_index_in_dim(qg, r, axis=1, keepdims=False)
        row_idx = rs + jnp.arange(wr) - r + WIN_R - 1
        bias = jnp.take(col_bias, row_idx, axis=1).transpose(0, 2, 3, 1, 4)
        s_win = jnp.einsum('bnihd,bwnkhd->bhniwk', qr, kb).astype(F32) + bias[None]
        s_win = jnp.where(win_mask, s_win, -jnp.inf).reshape(b, h, nb, COL_BLOCK, wr * COL_BAND)
        s_ctx = jnp.einsum('bnihd,bchd->bhnic', qr, kc).astype(F32)
        p = jax.nn.softmax(jnp.concatenate([s_win, s_ctx], axis=-1), axis=-1).astype(v.dtype)
        p_win = p[..., :wr * COL_BAND].reshape(b, h, nb, COL_BLOCK, wr, COL_BAND)
        p_ctx = p[..., wr * COL_BAND:]
        o = jnp.einsum('bhniwk,bwnkhd->bnihd', p_win, vb) + jnp.einsum('bhnic,bchd->bnihd', p_ctx, vc)
        return o.reshape(b, GRID_W, h, dh)

    out = lax.map(row_block, jnp.arange(rows))
    return out.transpose(1, 0, 2, 3, 4).reshape(b, s, h * dh)


def context_attention(qc, kc, vc):
    b, l, h, dh = qc.shape
    s = jnp.einsum('bqhd,bkhd->bhqk', qc * (dh ** -0.5), kc).astype(F32)
    p = jax.nn.softmax(s, axis=-1).astype(vc.dtype)
    return jnp.einsum('bhqk,bkhd->bqhd', p, vc).reshape(b, l, h * dh)


def segsum_exp(a):
    t = a.shape[-1]
    cs = jnp.cumsum(a, axis=-1)
    diff = cs[..., :, None] - cs[..., None, :]
    return jnp.exp(jnp.where(np.tril(np.ones((t, t), bool)), diff, -jnp.inf))


def ssd_scan(x, dt, A, bm, cm, h0, want_y):
    bsz, l, h, p = x.shape
    g, n = bm.shape[2], bm.shape[3]
    r = h // g
    q = SSM_CHUNK
    nc = l // q
    xc = (x.astype(F32) * dt[..., None]).reshape(bsz, nc, q, g, r, p)
    ac = (dt * A).reshape(bsz, nc, q, g, r).transpose(0, 1, 3, 4, 2)
    bc = bm.astype(F32).reshape(bsz, nc, q, g, n)
    cc = cm.astype(F32).reshape(bsz, nc, q, g, n)
    a_cs = jnp.cumsum(ac, axis=-1)
    decay_s = jnp.exp(a_cs[..., -1:] - a_cs).transpose(0, 1, 4, 2, 3)
    states = jnp.einsum('bclgn,bclgrp->bcgrpn', bc, xc * decay_s[..., None])
    states_all = jnp.concatenate([h0.reshape(bsz, 1, g, r, p, n), states], axis=1)
    chunk_a = jnp.pad(a_cs[..., -1], ((0, 0), (1, 0), (0, 0), (0, 0))).transpose(0, 2, 3, 1)
    new = jnp.einsum('bgrzc,bcgrpn->bzgrpn', segsum_exp(chunk_a), states_all)
    final = new[:, -1].reshape(bsz, h, p, n)
    if not want_y:
        return None, final
    prev = new[:, :-1]
    cb = jnp.einsum('bclgn,bcsgn->bcgls', cc, bc)
    m = cb[:, :, :, None] * segsum_exp(ac)
    y_diag = jnp.einsum('bcgrls,bcsgrp->bclgrp', m, xc)
    y_off = jnp.einsum('bclgn,bcgrpn->bclgrp', cc, prev) * jnp.exp(a_cs).transpose(0, 1, 4, 2, 3)[..., None]
    return (y_diag + y_off).reshape(bsz, l, h, p).astype(x.dtype), final


def bidirectional_ssd_mixer(zl, xbcl, dtl, zc, xbcc, dtc, conv_w, conv_b, dt_bias, A_log, D_skip,
                            norm_g, w_o, need_ctx):
    h, p, g, n = SSM_HEADS, SSM_HEAD_DIM, SSM_GROUPS, SSM_STATE

    def prep(xbc, dtr):
        xbc = jax.nn.silu(dwconv(xbc, conv_w, conv_b))
        bsz, l, _ = xbc.shape
        xs = xbc[..., :SSM_INNER].reshape(bsz, l, h, p)
        bm = xbc[..., SSM_INNER:SSM_INNER + g * n].reshape(bsz, l, g, n)
        cm = xbc[..., SSM_INNER + g * n:].reshape(bsz, l, g, n)
        dt = jax.nn.softplus(dtr.astype(F32).reshape(bsz, l, 2, h) + dt_bias.astype(F32))
        return xs, bm, cm, dt

    def gated_out(y, z):
        bsz, l = y.shape[:2]
        y = y.reshape(bsz, l, SSM_INNER) * jax.nn.silu(z)
        yg = y.reshape(bsz, l, g, SSM_INNER // g).astype(F32)
        yg = yg * lax.rsqrt(jnp.mean(yg * yg, axis=-1, keepdims=True) + EPS)
        return (yg.reshape(bsz, l, SSM_INNER) * norm_g.astype(F32)).astype(z.dtype) @ w_o

    xl, bl, cl, dtl_ = prep(xbcl, dtl)
    xc, bc, cc, dtc_ = prep(xbcc, dtc)
    h0 = jnp.zeros((xl.shape[0], h, p, n), F32)
    yl = D_skip[:, None] * xl
    yc = D_skip[:, None] * xc if need_ctx else None
    for d in range(2):
        A = -jnp.exp(A_log[d].astype(F32))
        f = (lambda t: t) if d == 0 else (lambda t: t[:, ::-1])
        yc_d, hc = ssd_scan(f(xc), f(dtc_[:, :, d]), A, f(bc), f(cc), h0, need_ctx)
        yl_d, _ = ssd_scan(f(xl), f(dtl_[:, :, d]), A, f(bl), f(cl), hc, True)
        yl = yl + f(yl_d)
        if need_ctx:
            yc = yc + f(yc_d)
    out_l = gated_out(yl, zl)
    out_c = gated_out(yc, zc) if need_ctx else None
    return out_l, out_c


def conformer_conv(u, dw_w, dw_b, ln_g, ln_b, w_o, b_o):
    a, gt = jnp.split(u, 2, axis=-1)
    hh = dwconv(a * jax.nn.sigmoid(gt), dw_w, dw_b)
    hh = layernorm(hh, ln_g, ln_b)
    return jax.nn.silu(hh) @ w_o + b_o


def merge_branches(gate_logits, y_att, y_ssm, y_cv, w_out):
    ga, gm, gc = jnp.split(jax.nn.sigmoid(gate_logits), 3, axis=-1)
    return (ga * y_att + gm * y_ssm + gc * y_cv) @ w_out


def peer(hh, wq, keys, u, v):
    t, d = hh.shape
    k = PEER_TOPK
    q = (hh @ wq).reshape(t, PEER_HEADS, 2, PEER_KEY_DIM)
    s = jnp.einsum('thjd,hjkd->thjk', q, keys).astype(F32)
    s1, i1 = lax.top_k(s[:, :, 0], k)
    s2, i2 = lax.top_k(s[:, :, 1], k)
    cand = (s1[..., :, None] + s2[..., None, :]).reshape(t, PEER_HEADS, k * k)
    best, pos = lax.top_k(cand, k)
    idx = jnp.take_along_axis(i1, pos // k, axis=-1) * PEER_NKEYS + jnp.take_along_axis(i2, pos % k, axis=-1)
    gate = jax.nn.softmax(best, axis=-1).astype(hh.dtype)
    nch = t // PEER_CHUNK
    hs = hh.reshape(nch, PEER_CHUNK, d)
    idxs = idx.reshape(nch, PEER_CHUNK, PEER_HEADS * k)
    gs = gate.reshape(nch, PEER_CHUNK, PEER_HEADS * k)

    def block(args):
        hb, ib, gb = args
        act = jax.nn.gelu(jnp.einsum('cd,ced->ce', hb, jnp.take(u, ib, axis=0)), approximate=False)
        return jnp.einsum('ce,ced->cd', gb * act, jnp.take(v, ib, axis=0))

    return lax.map(block, (hs, idxs, gs)).reshape(t, d)


def trunk_layer(xl, xc, c, c_ctx, need_ctx, w_mod, b_mod, norm1_g, norm2_g, w_in, na_rpb, na_wo,
                ssm_conv_w, ssm_conv_b, ssm_dt_bias, ssm_A_log, ssm_D, ssm_norm_g, ssm_wo,
                cv_dw_w, cv_dw_b, cv_ln_g, cv_ln_b, cv_wo, cv_bo, w_out, peer_wq, peer_keys, peer_u, peer_v):
    mod_l = jax.nn.silu(c) @ w_mod + b_mod
    mod_c = jax.nn.silu(c_ctx) @ w_mod + b_mod
    sh1, sc1, g1, sh2, sc2, g2 = jnp.split(mod_l[:, None, :], 6, axis=-1)
    csh1, csc1, cg1, csh2, csc2, cg2 = jnp.split(mod_c, 6, axis=-1)

    hl = modulate(xl, norm1_g, sh1, sc1)
    hc = modulate(xc, norm1_g, csh1, csc1)
    ql, kl, vl, zl, xbcl, dtl, glul, gatel = split_cols(hl @ w_in)
    qc, kc, vc, zc, xbcc, dtc, gluc, gatec = split_cols(hc @ w_in)
    kc_h, vc_h = to_heads(kc), to_heads(vc)
    y_att = neighbourhood_attention(to_heads(ql), to_heads(kl), to_heads(vl), kc_h, vc_h, na_rpb) @ na_wo
    y_ssm, y_ssm_c = bidirectional_ssd_mixer(zl, xbcl, dtl, zc, xbcc, dtc, ssm_conv_w, ssm_conv_b,
                                             ssm_dt_bias, ssm_A_log, ssm_D, ssm_norm_g, ssm_wo, need_ctx)
    y_cv = conformer_conv(glul, cv_dw_w, cv_dw_b, cv_ln_g, cv_ln_b, cv_wo, cv_bo)
    xl = xl + g1 * merge_branches(gatel, y_att, y_ssm, y_cv, w_out)
    if need_ctx:
        y_att_c = context_attention(to_heads(qc), kc_h, vc_h) @ na_wo
        y_cv_c = conformer_conv(gluc, cv_dw_w, cv_dw_b, cv_ln_g, cv_ln_b, cv_wo, cv_bo)
        xc = xc + cg1 * merge_branches(gatec, y_att_c, y_ssm_c, y_cv_c, w_out)

    d = xl.shape[-1]
    fl = modulate(xl, norm2_g, sh2, sc2).reshape(-1, d)
    if need_ctx:
        fc = modulate(xc, norm2_g, csh2, csc2).reshape(-1, d)
        out = peer(jnp.concatenate([fl, fc], axis=0), peer_wq, peer_keys, peer_u, peer_v)
        ol = out[:fl.shape[0]]
        xc = xc + cg2 * out[fl.shape[0]:].reshape(xc.shape)
    else:
        ol = peer(fl, peer_wq, peer_keys, peer_u, peer_v)
    xl = xl + g2 * ol.reshape(xl.shape)
    return xl, xc


def setup_inputs(seed: int = 0) -> dict:
    key = jax.random.key(seed)
    ks = jax.random.split(key, 32)
    L, D = DEPTH, D_MODEL

    def nrm(k, shape, s):
        return jax.random.normal(k, shape, F32) * s

    u01 = jax.random.uniform(ks[14], (L, 2, SSM_HEADS), F32)
    dt0 = jnp.exp(u01 * (math.log(0.1) - math.log(0.001)) + math.log(0.001))
    return {
        "x": nrm(ks[0], (BATCH, SEQ, D), 1.0),
        "c": nrm(ks[1], (BATCH, D), 1.0),
        "ctx": nrm(ks[2], (BATCH, CTX_LEN, D), 1.0),
        "c_ctx": nrm(ks[3], (D,), 1.0),
        "w_mod": nrm(ks[4], (L, D, 6 * D), 0.5 * D ** -0.5),
        "b_mod": nrm(ks[5], (L, 6 * D), 0.01),
        "norm1_g": 1.0 + nrm(ks[6], (L, D), 0.02),
        "norm2_g": 1.0 + nrm(ks[7], (L, D), 0.02),
        "w_in": nrm(ks[8], (L, D, IN_COLS), D ** -0.5),
        "na_rpb": nrm(ks[9], (L, NA_HEADS, 2 * WIN_R - 1, 2 * WIN_C - 1), 0.1),
        "na_wo": nrm(ks[10], (L, NA_WIDTH, D), NA_WIDTH ** -0.5),
        "ssm_conv_w": nrm(ks[11], (L, SSM_CONV, SSM_XBC), SSM_CONV ** -0.5),
        "ssm_conv_b": nrm(ks[12], (L, SSM_XBC), 0.01),
        "ssm_dt_bias": dt0 + jnp.log(-jnp.expm1(-dt0)),
        "ssm_A_log": jnp.log(jax.random.uniform(ks[13], (L, 2, SSM_HEADS), F32, minval=1.0, maxval=16.0)),
        "ssm_D": 1.0 + nrm(ks[15], (L, SSM_HEADS), 0.02),
        "ssm_norm_g": 1.0 + nrm(ks[16], (L, SSM_INNER), 0.02),
        "ssm_wo": nrm(ks[17], (L, SSM_INNER, D), SSM_INNER ** -0.5),
        "cv_dw_w": nrm(ks[18], (L, CONV_K, CONV_CH), CONV_K ** -0.5),
        "cv_dw_b": nrm(ks[19], (L, CONV_CH), 0.01),
        "cv_ln_g": 1.0 + nrm(ks[20], (L, CONV_CH), 0.02),
        "cv_ln_b": nrm(ks[21], (L, CONV_CH), 0.01),
        "cv_wo": nrm(ks[22], (L, CONV_CH, D), CONV_CH ** -0.5),
        "cv_bo": nrm(ks[23], (L, D), 0.01),
        "w_out": nrm(ks[24], (L, D, D), D ** -0.5),
        "peer_wq": nrm(ks[25], (L, D, PEER_HEADS * 2 * PEER_KEY_DIM), D ** -0.5),
        "peer_keys": nrm(ks[26], (L, PEER_HEADS, 2, PEER_NKEYS, PEER_KEY_DIM), PEER_KEY_DIM ** -0.5),
        "peer_u": nrm(ks[27], (L, PEER_EXPERTS, D), D ** -0.5),
        "peer_v": nrm(ks[28], (L, PEER_EXPERTS, D), 0.5),
        "final_norm_g": 1.0 + nrm(ks[29], (D,), 0.02),
    }


def reference(x, c, ctx, c_ctx, w_mod, b_mod, norm1_g, norm2_g, w_in, na_rpb, na_wo,
              ssm_conv_w, ssm_conv_b, ssm_dt_bias, ssm_A_log, ssm_D, ssm_norm_g, ssm_wo,
              cv_dw_w, cv_dw_b, cv_ln_g, cv_ln_b, cv_wo, cv_bo, w_out,
              peer_wq, peer_keys, peer_u, peer_v, final_norm_g):
    xl, xc = x, ctx
    for l in range(DEPTH):
        xl, xc = trunk_layer(xl, xc, c, c_ctx, l < DEPTH - 1, w_mod[l], b_mod[l], norm1_g[l], norm2_g[l],
                             w_in[l], na_rpb[l], na_wo[l], ssm_conv_w[l], ssm_conv_b[l], ssm_dt_bias[l],
                             ssm_A_log[l], ssm_D[l], ssm_norm_g[l], ssm_wo[l], cv_dw_w[l], cv_dw_b[l],
                             cv_ln_g[l], cv_ln_b[l], cv_wo[l], cv_bo[l], w_out[l],
                             peer_wq[l], peer_keys[l], peer_u[l], peer_v[l])
    return rmsnorm(xl, final_norm_g)
```

```python
import functools
import math

import numpy as np
import jax
import jax.numpy as jnp
from jax import lax
from jax.experimental import pallas as pl
from jax.experimental.pallas import tpu as pltpu

F32 = jnp.float32
BF16 = jnp.bfloat16
EPS = 1e-6

D_MODEL = 2048
GRID_W = 64
NA_HEADS = 16
NA_HEAD_DIM = 64
NA_WIDTH = NA_HEADS * NA_HEAD_DIM
WIN_R = 8
WIN_C = 16
SSM_HEADS = 16
SSM_HEAD_DIM = 64
SSM_INNER = SSM_HEADS * SSM_HEAD_DIM
SSM_GROUPS = 2
SSM_STATE = 128
SSM_XBC = SSM_INNER + 2 * SSM_GROUPS * SSM_STATE
SSM_CONV = 5
SSM_CHUNK = 128
CONV_CH = 1024
CONV_K = 31
PEER_HEADS = 8
PEER_NKEYS = 128
PEER_KEY_DIM = 128
PEER_TOPK = 16
PEER_EXPERTS = PEER_NKEYS * PEER_NKEYS

OFF_GATE = 0
OFF_GLU_A = 6144
OFF_GLU_G = 7168
OFF_Q = 8192
OFF_K = 9216
OFF_V = 10240
OFF_Z = 11264
OFF_XBC = 12288
PACK_COLS = 13824

LANES = 128
VMEM_LIMIT = 56 * 1024 * 1024


def _cparams(sem):
    return pltpu.CompilerParams(dimension_semantics=sem, vmem_limit_bytes=VMEM_LIMIT)


def _dot(a, b):
    return jnp.dot(a, b, preferred_element_type=F32)


def _dot_nt(a, b):
    return lax.dot_general(a, b, (((1,), (1,)), ((), ())), preferred_element_type=F32)


def _dot_tn(a, b):
    return lax.dot_general(a, b, (((0,), (0,)), ((), ())), preferred_element_type=F32)


def _split3(a):
    hi = a.astype(BF16)
    r1 = a - hi.astype(F32)
    mid = r1.astype(BF16)
    lo = (r1 - mid.astype(F32)).astype(BF16)
    return hi, mid, lo


def _dot_exact_rhs01(a, m01):
    hi, mid, lo = _split3(a)
    return _dot(hi, m01) + _dot(mid, m01) + _dot(lo, m01)


def _dot_exact_lhs01(m01, a):
    hi, mid, lo = _split3(a)
    return _dot(m01, hi) + _dot(m01, mid) + _dot(m01, lo)


def _sigmoid(x):
    return 1.0 / (1.0 + jnp.exp(-x))


def _silu(x):
    return x * _sigmoid(x)


def _mod_kernel(c_ref, w_ref, b_ref, o_ref):
    a = _silu(c_ref[...]).astype(BF16)
    o_ref[...] = _dot(a, w_ref[...].astype(BF16)) + b_ref[...]


def _modulation(cs, w_mod, b_mod):
    r, d = cs.shape
    n = w_mod.shape[1]
    tn = 1024
    return pl.pallas_call(
        _mod_kernel,
        grid=(n // tn,),
        in_specs=[pl.BlockSpec((r, d), lambda j: (0, 0)),
                  pl.BlockSpec((d, tn), lambda j: (0, j)),
                  pl.BlockSpec((1, tn), lambda j: (0, j))],
        out_specs=pl.BlockSpec((r, tn), lambda j: (0, j)),
        out_shape=jax.ShapeDtypeStruct((r, n), F32),
        compiler_params=_cparams(("arbitrary",)),
        name="modulation",
    )(cs, w_mod, b_mod.reshape(1, n))


def _nmm_kernel(x_ref, g_ref, sh_ref, sc_ref, w_ref, *rest, softplus_bias, emit_h):
    if softplus_bias:
        bias_ref, rest = rest[0], rest[1:]
    if emit_h:
        o_ref, ho_ref, h_ref = rest
    else:
        o_ref, h_ref = rest

    @pl.when(pl.program_id(2) == 0)
    def _():
        x = x_ref[0]
        ms = jnp.mean(x * x, axis=-1, keepdims=True)
        y = x * lax.rsqrt(ms + EPS) * g_ref[...]
        h = (y * (1.0 + sc_ref[0]) + sh_ref[0]).astype(BF16)
        h_ref[...] = h
        if emit_h:
            ho_ref[0] = h

    acc = _dot(h_ref[...], w_ref[...])
    if softplus_bias:
        t = acc + bias_ref[...]
        acc = jnp.maximum(t, 0.0) + jnp.log1p(jnp.exp(-jnp.abs(t)))
    o_ref[0] = acc.astype(o_ref.dtype)


def _norm_mod_matmul(x, g, shift, scale, w, out_dtype, tn, softplus_bias=None, emit_h=False):
    b, l, d = x.shape
    n = w.shape[1]
    tm = min(l, 1024)
    bm = shift.shape[0]
    mod_map = (lambda i, m, j: (i, 0, 0)) if bm > 1 else (lambda i, m, j: (0, 0, 0))
    in_specs = [pl.BlockSpec((1, tm, d), lambda i, m, j: (i, m, 0)),
                pl.BlockSpec((1, d), lambda i, m, j: (0, 0)),
                pl.BlockSpec((1, 1, d), mod_map),
                pl.BlockSpec((1, 1, d), mod_map),
                pl.BlockSpec((d, tn), lambda i, m, j: (0, j))]
    args = [x, g.reshape(1, d), shift.reshape(bm, 1, d), scale.reshape(bm, 1, d), w]
    if softplus_bias is not None:
        in_specs.append(pl.BlockSpec((1, tn), lambda i, m, j: (0, j)))
        args.append(softplus_bias.reshape(1, n))
    out_shape = [jax.ShapeDtypeStruct((b, l, n), out_dtype)]
    out_specs = [pl.BlockSpec((1, tm, tn), lambda i, m, j: (i, m, j))]
    if emit_h:
        out_shape.append(jax.ShapeDtypeStruct((b, l, d), BF16))
        out_specs.append(pl.BlockSpec((1, tm, d), lambda i, m, j: (i, m, 0)))
    res = pl.pallas_call(
        functools.partial(_nmm_kernel, softplus_bias=softplus_bias is not None, emit_h=emit_h),
        grid=(b, l // tm, n // tn),
        in_specs=in_specs,
        out_specs=out_specs,
        out_shape=out_shape,
        scratch_shapes=[pltpu.VMEM((tm, d), BF16)],
        compiler_params=_cparams(("parallel", "parallel", "arbitrary")),
        name="norm_mod_matmul",
    )(*args)
    return res if emit_h else res[0]


def _na_bias_table(rpb, rows):
    wr = min(WIN_R, rows)
    j = np.arange(GRID_W)
    cstart = np.clip(j - WIN_C // 2, 0, GRID_W - WIN_C)
    kc = np.arange(GRID_W)
    mask = (kc[None, :] >= cstart[:, None]) & (kc[None, :] < cstart[:, None] + WIN_C)
    col_off = np.clip(kc[None, :] - j[:, None], -(WIN_C - 1), WIN_C - 1) + WIN_C - 1
    dd = np.arange(wr)[:, None]
    ww = np.arange(wr)[None, :]
    row_idx = ww - dd + WIN_R - 1
    t = rpb.astype(F32)[:, row_idx][:, :, :, col_off]
    t = jnp.where(mask[None, None, None], t, -1e30)
    t = t.transpose(0, 1, 3, 2, 4).reshape(NA_HEADS, wr, GRID_W, wr * GRID_W)
    return t.reshape(NA_HEADS // 2, 2, wr, GRID_W, wr * GRID_W)


def _na_kernel(q_ref, k_ref, v_ref, kc_ref, vc_ref, bias_ref, *rest, rows, wr, with_ctx):
    if with_ctx:
        qc_ref, o_ref, oc_ref = rest
    else:
        (o_ref,) = rest
    lane = lax.broadcasted_iota(jnp.int32, (1, LANES), 1)
    lo = lane < NA_HEAD_DIM
    kc = kc_ref[0]
    vc = vc_ref[0]
    scale = NA_HEAD_DIM ** -0.5
    nwin = wr * GRID_W

    def heads_attend(q, kw, vw, bias_of):
        outs = []
        for h in range(2):
            sel = lo if h == 0 else jnp.logical_not(lo)
            qh = jnp.where(sel, q, jnp.zeros_like(q))
            s_c = _dot_nt(qh, kc) * scale
            m = jnp.max(s_c, axis=-1, keepdims=True)
            if kw is not None:
                s_w = _dot_nt(qh, kw) * scale + bias_of(h)
                m = jnp.maximum(m, jnp.max(s_w, axis=-1, keepdims=True))
                p_w = jnp.exp(s_w - m)
            p_c = jnp.exp(s_c - m)
            den = jnp.sum(p_c, axis=-1, keepdims=True)
            o = _dot(p_c.astype(BF16), vc)
            if kw is not None:
                den = den + jnp.sum(p_w, axis=-1, keepdims=True)
                o = o + _dot(p_w.astype(BF16), vw)
            outs.append(o / den)
        return jnp.where(lo, outs[0], outs[1])

    def body(r, carry):
        rs = jnp.clip(r - wr // 2, 0, rows - wr)
        d = r - rs
        q0 = pl.multiple_of(r * GRID_W, GRID_W)
        k0 = pl.multiple_of(rs * GRID_W, GRID_W)
        q = q_ref[0, pl.ds(q0, GRID_W), :]
        kw = k_ref[0, pl.ds(k0, nwin), :]
        vw = v_ref[0, pl.ds(k0, nwin), :]
        o = heads_attend(q, kw, vw, lambda h: bias_ref[0, h, d])
        o_ref[0, pl.ds(q0, GRID_W), :] = o.astype(o_ref.dtype)
        return carry

    lax.fori_loop(0, rows, body, 0)
    if with_ctx:
        oc_ref[0] = heads_attend(qc_ref[0], None, None, None).astype(oc_ref.dtype)


def _neighbourhood_attention(pl_lat, pl_ctx, bias_tab, with_ctx):
    b, s, _ = pl_lat.shape
    cl = pl_ctx.shape[1]
    rows = s // GRID_W
    wr = min(WIN_R, rows)
    nhp = NA_HEADS // 2
    qb, kb, vb = OFF_Q // LANES, OFF_K // LANES, OFF_V // LANES
    in_specs = [pl.BlockSpec((1, s, LANES), lambda hp, i: (i, 0, qb + hp)),
                pl.BlockSpec((1, s, LANES), lambda hp, i: (i, 0, kb + hp)),
                pl.BlockSpec((1, s, LANES), lambda hp, i: (i, 0, vb + hp)),
                pl.BlockSpec((1, cl, LANES), lambda hp, i: (i, 0, kb + hp)),
                pl.BlockSpec((1, cl, LANES), lambda hp, i: (i, 0, vb + hp)),
                pl.BlockSpec((1, 2, wr, GRID_W, wr * GRID_W), lambda hp, i: (hp, 0, 0, 0, 0))]
    args = [pl_lat, pl_lat, pl_lat, pl_ctx, pl_ctx, bias_tab]
    out_shape = [jax.ShapeDtypeStruct((b, s, NA_WIDTH), BF16)]
    out_specs = [pl.BlockSpec((1, s, LANES), lambda hp, i: (i, 0, hp))]
    if with_ctx:
        in_specs.append(pl.BlockSpec((1, cl, LANES), lambda hp, i: (i, 0, qb + hp)))
        args.append(pl_ctx)
        out_shape.append(jax.ShapeDtypeStruct((b, cl, NA_WIDTH), BF16))
        out_specs.append(pl.BlockSpec((1, cl, LANES), lambda hp, i: (i, 0, hp)))
    res = pl.pallas_call(
        functools.partial(_na_kernel, rows=rows, wr=wr, with_ctx=with_ctx),
        grid=(nhp, b),
        in_specs=in_specs,
        out_specs=out_specs,
        out_shape=out_shape,
        compiler_params=_cparams(("parallel", "parallel")),
        name="neighbourhood_attention",
    )(*args)
    return (res[0], res[1]) if with_ctx else (res[0], None)


CONV_TL = 256
CONV_HALO = 16


def _dwconv_kernel(*refs, taps, ch, glu, tl):
    if glu:
        (a_p, a_c, a_n, g_p, g_c, g_n, w_ref, b_ref, lng_ref, lnb_ref, o_ref, u_ref, y_ref) = refs
    else:
        (a_p, a_c, a_n, w_ref, b_ref, o_ref, u_ref) = refs
    i = pl.program_id(1)
    n = pl.num_programs(1)
    pad = (taps - 1) // 2
    hl = CONV_HALO

    def pre(a, g):
        a = a.astype(F32)
        if glu:
            return a * _sigmoid(g.astype(F32))
        return a

    u_ref[hl:hl + tl, :] = pre(a_c[0], g_c[0] if glu else None)
    top = pre(a_p[0, tl - hl:tl, :], g_p[0, tl - hl:tl, :] if glu else None)
    u_ref[0:hl, :] = jnp.where(i > 0, top, 0.0)
    bot = pre(a_n[0, 0:hl, :], g_n[0, 0:hl, :] if glu else None)
    u_ref[hl + tl:hl + tl + hl, :] = jnp.where(i < n - 1, bot, 0.0)

    for cc in range(ch // LANES):
        cs = slice(cc * LANES, (cc + 1) * LANES)
        acc = jnp.zeros((tl, LANES), F32) + b_ref[:, cs]
        for k in range(taps):
            off = hl - pad + k
            acc = acc + w_ref[k:k + 1, cs] * u_ref[off:off + tl, cs]
        if glu:
            y_ref[:, cs] = acc
        else:
            o_ref[0, :, cs] = _silu(acc).astype(o_ref.dtype)

    if glu:
        y = y_ref[...]
        mu = jnp.mean(y, axis=-1, keepdims=True)
        yc = y - mu
        var = jnp.mean(yc * yc, axis=-1, keepdims=True)
        z = yc * lax.rsqrt(var + EPS) * lng_ref[...] + lnb_ref[...]
        o_ref[0] = _silu(z).astype(o_ref.dtype)


def _dwconv(src, off_a, off_g, ch, w, b, ln_g, ln_b, out_dtype):
    bsz, l, _ = src.shape
    taps = w.shape[0]
    glu = off_g is not None
    tl = CONV_TL
    nt = l // tl
    ca = off_a // ch

    def spec(cb, delta):
        def imap(i, t):
            return (i, jnp.clip(t + delta, 0, nt - 1), cb)
        return pl.BlockSpec((1, tl, ch), imap)

    in_specs = [spec(ca, -1), spec(ca, 0), spec(ca, 1)]
    args = [src, src, src]
    if glu:
        cg = off_g // ch
        in_specs += [spec(cg, -1), spec(cg, 0), spec(cg, 1)]
        args += [src, src, src]
    in_specs += [pl.BlockSpec((taps, ch), lambda i, t: (0, 0)), pl.BlockSpec((1, ch), lambda i, t: (0, 0))]
    args += [w, b.reshape(1, ch)]
    scratch = [pltpu.VMEM((tl + 2 * CONV_HALO, ch), F32)]
    if glu:
        in_specs += [pl.BlockSpec((1, ch), lambda i, t: (0, 0)), pl.BlockSpec((1, ch), lambda i, t: (0, 0))]
        args += [ln_g.reshape(1, ch), ln_b.reshape(1, ch)]
        scratch.append(pltpu.VMEM((tl, ch), F32))
    return pl.pallas_call(
        functools.partial(_dwconv_kernel, taps=taps, ch=ch, glu=glu, tl=tl),
        grid=(bsz, nt),
        in_specs=in_specs,
        out_specs=pl.BlockSpec((1, tl, ch), lambda i, t: (i, t, 0)),
        out_shape=jax.ShapeDtypeStruct((bsz, l, ch), out_dtype),
        scratch_shapes=scratch,
        compiler_params=_cparams(("parallel", "parallel")),
        name="dwconv_glu" if glu else "dwconv_ssm",
    )(*args)


def _ssd_kernel(xl_ref, xc_ref, dtl_ref, dtc_ref, dttl_ref, dttc_ref, alog_ref, alogt_ref, dskip_ref,
                exp_ref, yl_ref, yc_ref, h_ref, *, ncc):
    d = pl.program_id(1)
    c = pl.program_id(2)
    q = SSM_CHUNK
    hh = SSM_HEADS
    hpg = SSM_HEADS // SSM_GROUPS
    gw = hpg * SSM_HEAD_DIM
    n = SSM_STATE
    is_ctx = c < ncc
    fwd = d == 0

    @pl.when(c == 0)
    def _():
        h_ref[...] = jnp.zeros_like(h_ref)

    xbc = jnp.where(is_ctx, xc_ref[0], xl_ref[0]).astype(F32)
    x = xbc[:, :SSM_INNER]
    dt2 = jnp.where(is_ctx, dtc_ref[0], dtl_ref[0])
    dt_col = jnp.where(fwd, dt2[:, 0:hh], dt2[:, hh:2 * hh])
    dtt2 = jnp.where(is_ctx, dttc_ref[0], dttl_ref[0])
    dt_row = jnp.where(fwd, dtt2[0:hh, :], dtt2[hh:2 * hh, :])
    alog = alog_ref[...]
    nega_row = -jnp.exp(jnp.where(fwd, alog[0:1, :], alog[1:2, :]))
    alogt = alogt_ref[...]
    nega_col = -jnp.exp(jnp.where(fwd, alogt[:, 0:1], alogt[:, 1:2]))
    a_col = dt_col * nega_row
    a_row = dt_row * nega_col

    li = lax.broadcasted_iota(jnp.int32, (q, q), 0)
    si = lax.broadcasted_iota(jnp.int32, (q, q), 1)
    sgn = jnp.where(fwd, 1, -1)
    tri = (si - li) * sgn <= 0
    trit = (li - si) * sgn <= 0
    tri_b = jnp.where(tri, 1.0, 0.0).astype(BF16)
    trit_b = jnp.where(trit, 1.0, 0.0).astype(BF16)
    e_col = _dot_exact_lhs01(tri_b, a_col)
    e_row = _dot_exact_rhs01(a_row, trit_b)
    e_tot = jnp.where(fwd, e_col[q - 1:q, :], e_col[0:1, :])

    expand = exp_ref[...]
    w_dt = _dot_exact_rhs01(dt_col, expand)
    w_dec = _dot_exact_rhs01(jnp.exp(e_tot - e_col), expand)
    w_off = _dot_exact_rhs01(jnp.exp(e_col), expand)
    w_tot = jnp.where(fwd, w_off[q - 1:q, :], w_off[0:1, :])

    xdt = x * w_dt
    xdt_b = xdt.astype(BF16)
    xdec_b = (xdt * w_dec).astype(BF16)
    lane = lax.broadcasted_iota(jnp.int32, (1, LANES), 1)
    lo = lane < SSM_HEAD_DIM

    y_parts = []
    for g in range(SSM_GROUPS):
        bm = xbc[:, SSM_INNER + g * n:SSM_INNER + (g + 1) * n].astype(BF16)
        cm = xbc[:, SSM_INNER + (SSM_GROUPS + g) * n:SSM_INNER + (SSM_GROUPS + g + 1) * n].astype(BF16)
        cb = _dot_nt(cm, bm)
        hg = h_ref[g]
        y_off = _dot(cm, hg.astype(BF16))
        gs = slice(g * gw, (g + 1) * gw)
        h_ref[g] = hg * w_tot[:, gs] + _dot_tn(bm, xdec_b[:, gs])
        for pr in range(hpg // 2):
            h0 = g * hpg + 2 * pr
            cs = slice(h0 * SSM_HEAD_DIM, (h0 + 2) * SSM_HEAD_DIM)
            xp = xdt_b[:, cs]
            acc = None
            for k in range(2):
                h = h0 + k
                lm = jnp.where(tri, jnp.exp(e_col[:, h:h + 1] - e_row[h:h + 1, :]), 0.0)
                mh = (cb * lm).astype(BF16)
                sel = lo if k == 0 else jnp.logical_not(lo)
                t = _dot(mh, jnp.where(sel, xp, jnp.zeros_like(xp)))
                acc = t if acc is None else acc + t
            ys = slice(2 * pr * SSM_HEAD_DIM, (2 * pr + 2) * SSM_HEAD_DIM)
            y_parts.append(acc + y_off[:, ys] * w_off[:, cs])
    y = jnp.concatenate(y_parts, axis=-1)
    y = y + jnp.where(fwd, dskip_ref[...], 0.0) * x

    @pl.when(is_ctx)
    def _():
        yc_ref[0, 0] = y

    @pl.when(jnp.logical_not(is_ctx))
    def _():
        yl_ref[0, 0] = y


def _ssd_scan(xbc_l, xbc_c, dt_l, dt_c, a_log, d_skip):
    b, l, _ = xbc_l.shape
    cl = xbc_c.shape[1]
    q = SSM_CHUNK
    ncl, ncc = l // q, cl // q
    nc = ncl + ncc
    hh = SSM_HEADS
    dtt_l = jnp.swapaxes(dt_l[..., :2 * hh], 1, 2)
    dtt_c = jnp.swapaxes(dt_c[..., :2 * hh], 1, 2)
    expand = jnp.asarray(np.kron(np.eye(hh), np.ones((1, SSM_HEAD_DIM))), BF16)
    dskip = jnp.repeat(d_skip.astype(F32), SSM_HEAD_DIM).reshape(1, SSM_INNER)

    def lat_chunk(d, c):
        cc = jnp.maximum(c - ncc, 0)
        return jnp.where(d == 0, cc, ncl - 1 - cc)

    def ctx_chunk(d, c):
        cc = jnp.minimum(c, ncc - 1)
        return jnp.where(d == 0, cc, ncc - 1 - cc)

    in_specs = [pl.BlockSpec((1, q, SSM_XBC), lambda i, d, c: (i, lat_chunk(d, c), 0)),
                pl.BlockSpec((1, q, SSM_XBC), lambda i, d, c: (i, ctx_chunk(d, c), 0)),
                pl.BlockSpec((1, q, LANES), lambda i, d, c: (i, lat_chunk(d, c), 0)),
                pl.BlockSpec((1, q, LANES), lambda i, d, c: (i, ctx_chunk(d, c), 0)),
                pl.BlockSpec((1, 2 * hh, q), lambda i, d, c: (i, 0, lat_chunk(d, c))),
                pl.BlockSpec((1, 2 * hh, q), lambda i, d, c: (i, 0, ctx_chunk(d, c))),
                pl.BlockSpec((2, hh), lambda i, d, c: (0, 0)),
                pl.BlockSpec((hh, 2), lambda i, d, c: (0, 0)),
                pl.BlockSpec((1, SSM_INNER), lambda i, d, c: (0, 0)),
                pl.BlockSpec((hh, SSM_INNER), lambda i, d, c: (0, 0))]
    out_specs = [pl.BlockSpec((1, 1, q, SSM_INNER), lambda i, d, c: (i, d, lat_chunk(d, c), 0)),
                 pl.BlockSpec((1, 1, q, SSM_INNER), lambda i, d, c: (i, d, ctx_chunk(d, c), 0))]
    out_shape = [jax.ShapeDtypeStruct((b, 2, l, SSM_INNER), F32),
                 jax.ShapeDtypeStruct((b, 2, cl, SSM_INNER), F32)]
    gw = (SSM_HEADS // SSM_GROUPS) * SSM_HEAD_DIM
    return pl.pallas_call(
        functools.partial(_ssd_kernel, ncc=ncc),
        grid=(b, 2, nc),
        in_specs=in_specs,
        out_specs=out_specs,
        out_shape=out_shape,
        scratch_shapes=[pltpu.VMEM((SSM_GROUPS, SSM_STATE, gw), F32)],
        compiler_params=_cparams(("parallel", "arbitrary", "arbitrary")),
        name="ssd_scan",
    )(xbc_l, xbc_c, dt_l, dt_c, dtt_l, dtt_c, a_log.astype(F32), a_log.astype(F32).T, dskip, expand)


def _merge_kernel(x_ref, g1_ref, gate_ref, att_ref, ys_ref, z_ref, cv_ref, ng_ref,
                  wa_ref, ws_ref, wc_ref, bc_ref, wo_ref, o_ref):
    d = D_MODEL
    ys = (ys_ref[0, 0] + ys_ref[0, 1]) * _silu(z_ref[0].astype(F32))
    gsz = SSM_INNER // SSM_GROUPS
    parts = []
    for g in range(SSM_GROUPS):
        yg = ys[:, g * gsz:(g + 1) * gsz]
        parts.append(yg * lax.rsqrt(jnp.mean(yg * yg, axis=-1, keepdims=True) + EPS))
    yn = (jnp.concatenate(parts, axis=-1) * ng_ref[...]).astype(BF16)
    p_s = _dot(yn, ws_ref[...])
    p_a = _dot(att_ref[0], wa_ref[...])
    p_c = _dot(cv_ref[0], wc_ref[...]) + bc_ref[...]
    gl = gate_ref[0].astype(F32)
    m = (_sigmoid(gl[:, 0:d]) * p_a + _sigmoid(gl[:, d:2 * d]) * p_s
         + _sigmoid(gl[:, 2 * d:3 * d]) * p_c).astype(BF16)
    o_ref[0] = x_ref[0] + g1_ref[0] * _dot(m, wo_ref[...])


def _merge(x, g1, packed, y_att, y_ssm2, y_cv, norm_g, wa, ws, wc, bc, wo):
    b, l, d = x.shape
    tm = 256
    bm = g1.shape[0]
    g_map = (lambda i, t: (i, 0, 0)) if bm > 1 else (lambda i, t: (0, 0, 0))
    const = lambda i, t: (0, 0)
    one = pl.Buffered(1)
    in_specs = [pl.BlockSpec((1, tm, d), lambda i, t: (i, t, 0)),
                pl.BlockSpec((1, 1, d), g_map),
                pl.BlockSpec((1, tm, 3 * d), lambda i, t: (i, t, OFF_GATE // (3 * d))),
                pl.BlockSpec((1, tm, NA_WIDTH), lambda i, t: (i, t, 0)),
                pl.BlockSpec((1, 2, tm, SSM_INNER), lambda i, t: (i, 0, t, 0)),
                pl.BlockSpec((1, tm, SSM_INNER), lambda i, t: (i, t, OFF_Z // SSM_INNER)),
                pl.BlockSpec((1, tm, CONV_CH), lambda i, t: (i, t, 0)),
                pl.BlockSpec((1, SSM_INNER), const),
                pl.BlockSpec((NA_WIDTH, d), const, pipeline_mode=one),
                pl.BlockSpec((SSM_INNER, d), const, pipeline_mode=one),
                pl.BlockSpec((CONV_CH, d), const, pipeline_mode=one),
                pl.BlockSpec((1, d), const),
                pl.BlockSpec((d, d), const, pipeline_mode=one)]
    return pl.pallas_call(
        _merge_kernel,
        grid=(b, l // tm),
        in_specs=in_specs,
        out_specs=pl.BlockSpec((1, tm, d), lambda i, t: (i, t, 0)),
        out_shape=jax.ShapeDtypeStruct((b, l, d), F32),
        compiler_params=_cparams(("parallel", "parallel")),
        name="merge",
    )(x, g1.reshape(bm, 1, d), packed, y_att, y_ssm2, packed, y_cv, norm_g.reshape(1, -1),
      wa, ws, wc, bc.reshape(1, d), wo)


PEER_TQ = 128
NEG_INF = float("-inf")


def _cand_pairs():
    return [(i, j) for i in range(PEER_TOPK) for j in range(PEER_TOPK) if (i + 1) * (j + 1) <= PEER_TOPK]


def _extract_max(x, iota):
    m = jnp.max(x, axis=0, keepdims=True)
    first = jnp.min(jnp.where(x == m, iota, float(x.shape[0])), axis=0, keepdims=True)
    return m, jnp.where(iota == first, NEG_INF, x)


def _peer_select_kernel(q_ref, keys_ref, s_ref, st_ref, top_ref, cand_ref):
    tq = PEER_TQ
    k = PEER_TOPK
    nk = PEER_NKEYS
    iota = lax.broadcasted_iota(jnp.int32, (nk, tq), 0).astype(F32)

    def per_set(i, carry):
        c0 = pl.multiple_of(i * PEER_KEY_DIM, PEER_KEY_DIM)
        s = _dot_nt(keys_ref[i], q_ref[:, pl.ds(c0, PEER_KEY_DIM)])
        s_ref[i] = s
        for r in range(k):
            m, s = _extract_max(s, iota)
            top_ref[i, r:r + 1, :] = m
        return carry

    lax.fori_loop(0, 2 * PEER_HEADS, per_set, 0)

    pairs = _cand_pairs()
    ncand = cand_ref.shape[0]
    ciota = lax.broadcasted_iota(jnp.int32, (ncand, tq), 0).astype(F32)
    cand_ref[...] = jnp.full((ncand, tq), NEG_INF, F32)
    for h in range(PEER_HEADS):
        for n, (i, j) in enumerate(pairs):
            cand_ref[n:n + 1, :] = top_ref[2 * h, i:i + 1, :] + top_ref[2 * h + 1, j:j + 1, :]
        cnd = cand_ref[...]
        best0 = None
        z = None
        m = None
        for r in range(k):
            m, cnd = _extract_max(cnd, ciota)
            if r == 0:
                best0 = m
                z = jnp.ones_like(m)
            else:
                z = z + jnp.exp(m - best0)
        st_ref[4 * h + 0:4 * h + 1, :] = m
        st_ref[4 * h + 1:4 * h + 2, :] = top_ref[2 * h, 0:1, :]
        st_ref[4 * h + 2:4 * h + 3, :] = top_ref[2 * h + 1, 0:1, :]
        st_ref[4 * h + 3:4 * h + 4, :] = 1.0 / z


def _peer_select(q, keys):
    t = q.shape[0]
    tq = PEER_TQ
    nset = 2 * PEER_HEADS
    ncand = -(-len(_cand_pairs()) // 8) * 8
    return pl.pallas_call(
        _peer_select_kernel,
        grid=(t // tq,),
        in_specs=[pl.BlockSpec((tq, q.shape[1]), lambda i: (i, 0)),
                  pl.BlockSpec((nset, PEER_NKEYS, PEER_KEY_DIM), lambda i: (0, 0, 0))],
        out_specs=[pl.BlockSpec((nset, PEER_NKEYS, tq), lambda i: (0, 0, i)),
                   pl.BlockSpec((4 * PEER_HEADS, tq), lambda i: (0, i))],
        out_shape=[jax.ShapeDtypeStruct((nset, PEER_NKEYS, t), F32),
                   jax.ShapeDtypeStruct((4 * PEER_HEADS, t), F32)],
        scratch_shapes=[pltpu.VMEM((nset, PEER_TOPK, tq), F32), pltpu.VMEM((ncand, tq), F32)],
        compiler_params=_cparams(("parallel",)),
        name="peer_select",
    )(q, keys)


PEER_TM = 512
PEER_TE = 1024


def _gelu(x):
    return 0.5 * x * (1.0 + lax.erf(x * np.float32(math.sqrt(0.5))))


def _peer_dense_kernel(h_ref, u_ref, vt_ref, s_ref, st_ref, x_ref, g2_ref, o_ref,
                       acc_ref, e_ref, at_ref, wa_ref):
    j = pl.program_id(1)
    nj = pl.num_programs(1)
    tm = PEER_TM
    te = PEER_TE
    nk = PEER_NKEYS

    @pl.when(j == 0)
    def _():
        acc_ref[...] = jnp.zeros_like(acc_ref)
        for h in range(PEER_HEADS):
            m1 = st_ref[4 * h + 1:4 * h + 2, :]
            m2 = st_ref[4 * h + 2:4 * h + 3, :]
            rz = st_ref[4 * h + 3:4 * h + 4, :]
            e_ref[2 * h] = jnp.exp(s_ref[2 * h] - m1) * rz
            e_ref[2 * h + 1] = jnp.exp(s_ref[2 * h + 1] - m2)

    at_ref[...] = _dot_nt(u_ref[...], h_ref[...])

    na = te // nk
    a0 = pl.multiple_of(j * na, na)
    for lc in range(tm // LANES):
        ls = slice(lc * LANES, (lc + 1) * LANES)
        for al in range(na):
            rs = slice(al * nk, (al + 1) * nk)
            w = jnp.zeros((nk, LANES), F32)
            for h in range(PEER_HEADS):
                s1 = s_ref[2 * h, pl.ds(a0, na), ls][al:al + 1]
                e1 = e_ref[2 * h, pl.ds(a0, na), ls][al:al + 1]
                thr = st_ref[4 * h:4 * h + 1, ls]
                ssum = s_ref[2 * h + 1, :, ls] + s1
                w = w + jnp.where(ssum >= thr, e_ref[2 * h + 1, :, ls], 0.0) * e1
            wa_ref[rs, ls] = (w * _gelu(at_ref[rs, ls])).astype(BF16)

    acc_ref[...] += _dot(vt_ref[...], wa_ref[...])

    @pl.when(j == nj - 1)
    def _():
        o_ref[...] = x_ref[...] + g2_ref[0] * acc_ref[...].T


def _peer_dense(hh, u, vt, scores, stats, x, g2, tiles_per_row):
    t, d = hh.shape
    ne = u.shape[0]
    tm, te = PEER_TM, PEER_TE
    one = pl.Buffered(1)
    return pl.pallas_call(
        _peer_dense_kernel,
        grid=(t // tm, ne // te),
        in_specs=[pl.BlockSpec((tm, d), lambda i, j: (i, 0), pipeline_mode=one),
                  pl.BlockSpec((te, d), lambda i, j: (j, 0)),
                  pl.BlockSpec((d, te), lambda i, j: (0, j)),
                  pl.BlockSpec((2 * PEER_HEADS, PEER_NKEYS, tm), lambda i, j: (0, 0, i), pipeline_mode=one),
                  pl.BlockSpec((4 * PEER_HEADS, tm), lambda i, j: (0, i), pipeline_mode=one),
                  pl.BlockSpec((tm, d), lambda i, j: (i, 0), pipeline_mode=one),
                  pl.BlockSpec((1, 1, d), lambda i, j: (i // tiles_per_row, 0, 0))],
        out_specs=pl.BlockSpec((tm, d), lambda i, j: (i, 0)),
        out_shape=jax.ShapeDtypeStruct((t, d), F32),
        scratch_shapes=[pltpu.VMEM((d, tm), F32),
                        pltpu.VMEM((2 * PEER_HEADS, PEER_NKEYS, tm), F32),
                        pltpu.VMEM((te, tm), F32),
                        pltpu.VMEM((te, tm), BF16)],
        compiler_params=_cparams(("parallel", "arbitrary")),
        name="peer_dense",
    )(hh, u, vt, scores, stats, x, g2)


def _peer_block(x, g, shift, scale, gate2, wq, keys, u, vt):
    b, l, d = x.shape
    q, hh = _norm_mod_matmul(x, g, shift, scale, wq, BF16, 512, emit_h=True)
    t = b * l
    scores, stats = _peer_select(q.reshape(t, -1), keys)
    bm = gate2.shape[0]
    tiles_per_row = (l // PEER_TM) if bm > 1 else (t // PEER_TM)
    out = _peer_dense(hh.reshape(t, d), u, vt, scores, stats, x.reshape(t, d),
                      gate2.reshape(bm, 1, d), tiles_per_row)
    return out.reshape(b, l, d)


def _rmsnorm_kernel(x_ref, g_ref, o_ref):
    x = x_ref[...]
    o_ref[...] = x * lax.rsqrt(jnp.mean(x * x, axis=-1, keepdims=True) + EPS) * g_ref[...]


def _rmsnorm(x, g):
    t, d = x.shape
    tm = 512
    return pl.pallas_call(
        _rmsnorm_kernel,
        grid=(t // tm,),
        in_specs=[pl.BlockSpec((tm, d), lambda i: (i, 0)), pl.BlockSpec((1, d), lambda i: (0, 0))],
        out_specs=pl.BlockSpec((tm, d), lambda i: (i, 0)),
        out_shape=jax.ShapeDtypeStruct((t, d), F32),
        compiler_params=_cparams(("parallel",)),
        name="final_rmsnorm",
    )(x, g.reshape(1, d))


def _pack_w_in(w_in):
    q, k, v, z, xbc, dt, glu, gate = jnp.split(
        w_in, np.cumsum([NA_WIDTH, NA_WIDTH, NA_WIDTH, SSM_INNER, SSM_XBC, 2 * SSM_HEADS, 2 * CONV_CH]).tolist(),
        axis=-1)
    packed = jnp.concatenate([gate, glu, q, k, v, z, xbc], axis=-1).astype(BF16)
    dt_w = jnp.pad(dt, ((0, 0), (0, LANES - 2 * SSM_HEADS))).astype(BF16)
    return packed, dt_w


def _trunk_layer(xl, xc, c_rows, need_ctx, w_mod, b_mod, norm1_g, norm2_g, w_in, na_rpb, na_wo,
                 ssm_conv_w, ssm_conv_b, ssm_dt_bias, ssm_a_log, ssm_d, ssm_norm_g, ssm_wo,
                 cv_dw_w, cv_dw_b, cv_ln_g, cv_ln_b, cv_wo, cv_bo, w_out, peer_wq, peer_keys, peer_u, peer_v):
    b, s, d = xl.shape
    mod = _modulation(c_rows, w_mod, b_mod)
    sh1, sc1, g1, sh2, sc2, g2 = [mod[:b, i * d:(i + 1) * d] for i in range(6)]
    csh1, csc1, cg1, csh2, csc2, cg2 = [mod[b:b + 1, i * d:(i + 1) * d] for i in range(6)]

    w_pack, w_dt = _pack_w_in(w_in)
    dt_bias = jnp.pad(ssm_dt_bias.astype(F32).reshape(-1), (0, LANES - 2 * SSM_HEADS))

    p_l = _norm_mod_matmul(xl, norm1_g, sh1, sc1, w_pack, BF16, 512)
    p_c = _norm_mod_matmul(xc, norm1_g, csh1, csc1, w_pack, BF16, 512)
    dt_l = _norm_mod_matmul(xl, norm1_g, sh1, sc1, w_dt, F32, LANES, softplus_bias=dt_bias)
    dt_c = _norm_mod_matmul(xc, norm1_g, csh1, csc1, w_dt, F32, LANES, softplus_bias=dt_bias)

    bias_tab = _na_bias_table(na_rpb, s // GRID_W)
    y_att, y_att_c = _neighbourhood_attention(p_l, p_c, bias_tab, need_ctx)

    xbc_l = _dwconv(p_l, OFF_XBC, None, SSM_XBC, ssm_conv_w, ssm_conv_b, None, None, F32)
    xbc_c = _dwconv(p_c, OFF_XBC, None, SSM_XBC, ssm_conv_w, ssm_conv_b, None, None, F32)
    y_ssm, y_ssm_c = _ssd_scan(xbc_l, xbc_c, dt_l, dt_c, ssm_a_log, ssm_d)

    wa, ws, wc, wo = (na_wo.astype(BF16), ssm_wo.astype(BF16), cv_wo.astype(BF16), w_out.astype(BF16))
    y_cv = _dwconv(p_l, OFF_GLU_A, OFF_GLU_G, CONV_CH, cv_dw_w, cv_dw_b, cv_ln_g, cv_ln_b, BF16)
    xl = _merge(xl, g1, p_l, y_att, y_ssm, y_cv, ssm_norm_g, wa, ws, wc, cv_bo, wo)
    if need_ctx:
        y_cv_c = _dwconv(p_c, OFF_GLU_A, OFF_GLU_G, CONV_CH, cv_dw_w, cv_dw_b, cv_ln_g, cv_ln_b, BF16)
        xc = _merge(xc, cg1, p_c, y_att_c, y_ssm_c, y_cv_c, ssm_norm_g, wa, ws, wc, cv_bo, wo)

    wq = peer_wq.astype(BF16)
    keys = peer_keys.astype(BF16).reshape(2 * PEER_HEADS, PEER_NKEYS, PEER_KEY_DIM)
    u = peer_u.astype(BF16)
    vt = peer_v.astype(BF16).T
    xl = _peer_block(xl, norm2_g, sh2, sc2, g2, wq, keys, u, vt)
    if need_ctx:
        xc = _peer_block(xc, norm2_g, csh2, csc2, cg2, wq, keys, u, vt)
    return xl, xc


def kernel(x, c, ctx, c_ctx, w_mod, b_mod, norm1_g, norm2_g, w_in, na_rpb, na_wo, ssm_conv_w, ssm_conv_b,
           ssm_dt_bias, ssm_A_log, ssm_D, ssm_norm_g, ssm_wo, cv_dw_w, cv_dw_b, cv_ln_g, cv_ln_b, cv_wo,
           cv_bo, w_out, peer_wq, peer_keys, peer_u, peer_v, final_norm_g):
    b, s, d = x.shape
    depth = w_mod.shape[0]
    rows = -(-(b + 1) // 8) * 8
    c_rows = jnp.zeros((rows, d), F32).at[:b].set(c).at[b].set(c_ctx)
    xl, xc = x, ctx
    for l in range(depth):
        xl, xc = _trunk_layer(xl, xc, c_rows, l < depth - 1, w_mod[l], b_mod[l], norm1_g[l], norm2_g[l],
                              w_in[l], na_rpb[l], na_wo[l], ssm_conv_w[l], ssm_conv_b[l], ssm_dt_bias[l],
                              ssm_A_log[l], ssm_D[l], ssm_norm_g[l], ssm_wo[l], cv_dw_w[l], cv_dw_b[l],
                              cv_ln_g[l], cv_ln_b[l], cv_wo[l], cv_bo[l], w_out[l],
                              peer_wq[l], peer_keys[l], peer_u[l], peer_v[l])
    return _rmsnorm(xl.reshape(b * s, d), final_norm_g).reshape(b, s, d)
```

```python
import functools
import itertools
import math

import numpy as np
import jax
import jax.numpy as jnp
from jax import lax
from jax.experimental import pallas as pl
from jax.experimental.pallas import tpu as pltpu

F32 = jnp.float32
BF16 = jnp.bfloat16
EPS = 1e-6

D_MODEL = 2048
GRID_W = 64
NA_HEADS = 16
NA_HEAD_DIM = 64
NA_WIDTH = NA_HEADS * NA_HEAD_DIM
WIN_R = 8
WIN_C = 16
SSM_HEADS = 16
SSM_HEAD_DIM = 64
SSM_INNER = SSM_HEADS * SSM_HEAD_DIM
SSM_GROUPS = 2
SSM_STATE = 128
SSM_XBC = SSM_INNER + 2 * SSM_GROUPS * SSM_STATE
SSM_CONV = 5
SSM_CHUNK = 128
CONV_CH = 1024
CONV_K = 31
PEER_HEADS = 8
PEER_NKEYS = 128
PEER_KEY_DIM = 128
PEER_TOPK = 16
PEER_EXPERTS = PEER_NKEYS * PEER_NKEYS

OFF_GATE = 0
OFF_GLU_A = 6144
OFF_GLU_G = 7168
OFF_Q = 8192
OFF_K = 9216
OFF_V = 10240
OFF_Z = 11264
OFF_XBC = 12288
PACK_COLS = 13824

LANES = 128
VMEM_LIMIT = 56 * 1024 * 1024


def _cparams(sem):
    return pltpu.CompilerParams(dimension_semantics=sem, vmem_limit_bytes=VMEM_LIMIT)


def _dot(a, b):
    return jnp.dot(a, b, preferred_element_type=F32)


def _dot_nt(a, b):
    return lax.dot_general(a, b, (((1,), (1,)), ((), ())), preferred_element_type=F32)


def _dot_tn(a, b):
    return lax.dot_general(a, b, (((0,), (0,)), ((), ())), preferred_element_type=F32)


def _split3(a):
    hi = a.astype(BF16)
    r1 = a - hi.astype(F32)
    mid = r1.astype(BF16)
    lo = (r1 - mid.astype(F32)).astype(BF16)
    return hi, mid, lo


def _dot_exact_rhs01(a, m01):
    hi, mid, lo = _split3(a)
    return _dot(hi, m01) + _dot(mid, m01) + _dot(lo, m01)


def _dot_exact_lhs01(m01, a):
    hi, mid, lo = _split3(a)
    return _dot(m01, hi) + _dot(m01, mid) + _dot(m01, lo)


def _sigmoid(x):
    return 1.0 / (1.0 + jnp.exp(-x))


def _silu(x):
    return x * _sigmoid(x)


def _mod_kernel(c_ref, w_ref, b_ref, o_ref):
    a = _silu(c_ref[...]).astype(BF16)
    o_ref[...] = _dot(a, w_ref[...].astype(BF16)) + b_ref[...]


def _modulation(cs, w_mod, b_mod):
    r, d = cs.shape
    n = w_mod.shape[1]
    tn = 1024
    return pl.pallas_call(
        _mod_kernel,
        grid=(n // tn,),
        in_specs=[pl.BlockSpec((r, d), lambda j: (0, 0)),
                  pl.BlockSpec((d, tn), lambda j: (0, j)),
                  pl.BlockSpec((1, tn), lambda j: (0, j))],
        out_specs=pl.BlockSpec((r, tn), lambda j: (0, j)),
        out_shape=jax.ShapeDtypeStruct((r, n), F32),
        compiler_params=_cparams(("arbitrary",)),
        name="modulation",
    )(cs, w_mod, b_mod.reshape(1, n))


def _nmm_kernel(x_ref, g_ref, sh_ref, sc_ref, w_ref, *rest, softplus_bias, emit_h):
    if softplus_bias:
        bias_ref, rest = rest[0], rest[1:]
    if emit_h:
        o_ref, ho_ref, h_ref = rest
    else:
        o_ref, h_ref = rest

    @pl.when(pl.program_id(2) == 0)
    def _():
        x = x_ref[0]
        ms = jnp.mean(x * x, axis=-1, keepdims=True)
        y = x * lax.rsqrt(ms + EPS) * g_ref[...]
        h = (y * (1.0 + sc_ref[0]) + sh_ref[0]).astype(BF16)
        h_ref[...] = h
        if emit_h:
            ho_ref[0] = h

    acc = _dot(h_ref[...], w_ref[...])
    if softplus_bias:
        t = acc + bias_ref[...]
        acc = jnp.maximum(t, 0.0) + jnp.log1p(jnp.exp(-jnp.abs(t)))
    o_ref[0] = acc.astype(o_ref.dtype)


def _norm_mod_matmul(x, g, shift, scale, w, out_dtype, tn, softplus_bias=None, emit_h=False):
    b, l, d = x.shape
    n = w.shape[1]
    tm = min(l, 1024)
    bm = shift.shape[0]
    mod_map = (lambda i, m, j: (i, 0, 0)) if bm > 1 else (lambda i, m, j: (0, 0, 0))
    in_specs = [pl.BlockSpec((1, tm, d), lambda i, m, j: (i, m, 0)),
                pl.BlockSpec((1, d), lambda i, m, j: (0, 0)),
                pl.BlockSpec((1, 1, d), mod_map),
                pl.BlockSpec((1, 1, d), mod_map),
                pl.BlockSpec((d, tn), lambda i, m, j: (0, j))]
    args = [x, g.reshape(1, d), shift.reshape(bm, 1, d), scale.reshape(bm, 1, d), w]
    if softplus_bias is not None:
        in_specs.append(pl.BlockSpec((1, tn), lambda i, m, j: (0, j)))
        args.append(softplus_bias.reshape(1, n))
    out_shape = [jax.ShapeDtypeStruct((b, l, n), out_dtype)]
    out_specs = [pl.BlockSpec((1, tm, tn), lambda i, m, j: (i, m, j))]
    if emit_h:
        out_shape.append(jax.ShapeDtypeStruct((b, l, d), BF16))
        out_specs.append(pl.BlockSpec((1, tm, d), lambda i, m, j: (i, m, 0)))
    res = pl.pallas_call(
        functools.partial(_nmm_kernel, softplus_bias=softplus_bias is not None, emit_h=emit_h),
        grid=(b, l // tm, n // tn),
        in_specs=in_specs,
        out_specs=out_specs,
        out_shape=out_shape,
        scratch_shapes=[pltpu.VMEM((tm, d), BF16)],
        compiler_params=_cparams(("parallel", "parallel", "arbitrary")),
        name="norm_mod_matmul",
    )(*args)
    return res if emit_h else res[0]


def _na_bias_table(rpb, rows):
    wr = min(WIN_R, rows)
    j = np.arange(GRID_W)
    cstart = np.clip(j - WIN_C // 2, 0, GRID_W - WIN_C)
    kc = np.arange(GRID_W)
    mask = (kc[None, :] >= cstart[:, None]) & (kc[None, :] < cstart[:, None] + WIN_C)
    col_off = np.clip(kc[None, :] - j[:, None], -(WIN_C - 1), WIN_C - 1) + WIN_C - 1
    dd = np.arange(wr)[:, None]
    ww = np.arange(wr)[None, :]
    row_idx = ww - dd + WIN_R - 1
    t = rpb.astype(F32)[:, row_idx][:, :, :, col_off]
    t = jnp.where(mask[None, None, None], t, -1e30)
    t = t.transpose(0, 1, 3, 2, 4).reshape(NA_HEADS, wr, GRID_W, wr * GRID_W)
    return t.reshape(NA_HEADS // 2, 2, wr, GRID_W, wr * GRID_W)


NA_ROW_GROUP = 8


def _na_kernel(q_ref, k_ref, v_ref, kc_ref, vc_ref, bias_ref, *rest, rows, wr, with_ctx):
    if with_ctx:
        qc_ref, o_ref, oc_ref = rest
    else:
        (o_ref,) = rest
    lane = lax.broadcasted_iota(jnp.int32, (1, LANES), 1)
    lo = lane < NA_HEAD_DIM
    kc = kc_ref[0]
    vc = vc_ref[0]
    scale = NA_HEAD_DIM ** -0.5
    nwin = wr * GRID_W

    def attend(problems):
        chains = []
        for q, kw, vw, bias_of in problems:
            for h in range(2):
                sel = lo if h == 0 else jnp.logical_not(lo)
                qh = jnp.where(sel, q, jnp.zeros_like(q))
                s_c = _dot_nt(qh, kc) * scale
                s_w = None if kw is None else _dot_nt(qh, kw) * scale + bias_of(h)
                chains.append((s_c, s_w, vw))
        maxes = []
        for s_c, s_w, _ in chains:
            m = jnp.max(s_c, axis=-1, keepdims=True)
            if s_w is not None:
                m = jnp.maximum(m, jnp.max(s_w, axis=-1, keepdims=True))
            maxes.append(m)
        probs = []
        for (s_c, s_w, _), m in zip(chains, maxes):
            probs.append((jnp.exp(s_c - m), None if s_w is None else jnp.exp(s_w - m)))
        outs = []
        for (_, _, vw), (p_c, p_w) in zip(chains, probs):
            den = jnp.sum(p_c, axis=-1, keepdims=True)
            o = _dot(p_c.astype(BF16), vc)
            if p_w is not None:
                den = den + jnp.sum(p_w, axis=-1, keepdims=True)
                o = o + _dot(p_w.astype(BF16), vw)
            outs.append(o / den)
        return [jnp.where(lo, outs[2 * i], outs[2 * i + 1]) for i in range(len(problems))]

    def body(g, carry):
        problems, q0s = [], []
        for rr in range(NA_ROW_GROUP):
            r = g * NA_ROW_GROUP + rr
            rs = jnp.clip(r - wr // 2, 0, rows - wr)
            d = r - rs
            q0 = pl.multiple_of(r * GRID_W, GRID_W)
            k0 = pl.multiple_of(rs * GRID_W, GRID_W)
            problems.append((q_ref[0, pl.ds(q0, GRID_W), :], k_ref[0, pl.ds(k0, nwin), :],
                             v_ref[0, pl.ds(k0, nwin), :], lambda h, d=d: bias_ref[0, h, d]))
            q0s.append(q0)
        for q0, o in zip(q0s, attend(problems)):
            o_ref[0, pl.ds(q0, GRID_W), :] = o.astype(o_ref.dtype)
        return carry

    lax.fori_loop(0, rows // NA_ROW_GROUP, body, 0)
    if with_ctx:
        oc_ref[0] = attend([(qc_ref[0], None, None, None)])[0].astype(oc_ref.dtype)


def _neighbourhood_attention(pl_lat, pl_ctx, bias_tab, with_ctx):
    b, s, _ = pl_lat.shape
    cl = pl_ctx.shape[1]
    rows = s // GRID_W
    wr = min(WIN_R, rows)
    nhp = NA_HEADS // 2
    qb, kb, vb = OFF_Q // LANES, OFF_K // LANES, OFF_V // LANES
    in_specs = [pl.BlockSpec((1, s, LANES), lambda hp, i: (i, 0, qb + hp)),
                pl.BlockSpec((1, s, LANES), lambda hp, i: (i, 0, kb + hp)),
                pl.BlockSpec((1, s, LANES), lambda hp, i: (i, 0, vb + hp)),
                pl.BlockSpec((1, cl, LANES), lambda hp, i: (i, 0, kb + hp)),
                pl.BlockSpec((1, cl, LANES), lambda hp, i: (i, 0, vb + hp)),
                pl.BlockSpec((1, 2, wr, GRID_W, wr * GRID_W), lambda hp, i: (hp, 0, 0, 0, 0))]
    args = [pl_lat, pl_lat, pl_lat, pl_ctx, pl_ctx, bias_tab]
    out_shape = [jax.ShapeDtypeStruct((b, s, NA_WIDTH), BF16)]
    out_specs = [pl.BlockSpec((1, s, LANES), lambda hp, i: (i, 0, hp))]
    if with_ctx:
        in_specs.append(pl.BlockSpec((1, cl, LANES), lambda hp, i: (i, 0, qb + hp)))
        args.append(pl_ctx)
        out_shape.append(jax.ShapeDtypeStruct((b, cl, NA_WIDTH), BF16))
        out_specs.append(pl.BlockSpec((1, cl, LANES), lambda hp, i: (i, 0, hp)))
    res = pl.pallas_call(
        functools.partial(_na_kernel, rows=rows, wr=wr, with_ctx=with_ctx),
        grid=(nhp, b),
        in_specs=in_specs,
        out_specs=out_specs,
        out_shape=out_shape,
        compiler_params=_cparams(("parallel", "parallel")),
        name="neighbourhood_attention",
    )(*args)
    return (res[0], res[1]) if with_ctx else (res[0], None)


CONV_TL = 256
CONV_HALO = 16


def _dwconv_kernel(*refs, taps, ch, glu, tl):
    if glu:
        (a_p, a_c, a_n, g_p, g_c, g_n, w_ref, b_ref, lng_ref, lnb_ref, o_ref, u_ref, y_ref) = refs
    else:
        (a_p, a_c, a_n, w_ref, b_ref, o_ref, u_ref) = refs
    i = pl.program_id(1)
    n = pl.num_programs(1)
    pad = (taps - 1) // 2
    hl = CONV_HALO

    def pre(a, g):
        a = a.astype(F32)
        if glu:
            return a * _sigmoid(g.astype(F32))
        return a

    u_ref[hl:hl + tl, :] = pre(a_c[0], g_c[0] if glu else None)
    top = pre(a_p[0, tl - hl:tl, :], g_p[0, tl - hl:tl, :] if glu else None)
    u_ref[0:hl, :] = jnp.where(i > 0, top, 0.0)
    bot = pre(a_n[0, 0:hl, :], g_n[0, 0:hl, :] if glu else None)
    u_ref[hl + tl:hl + tl + hl, :] = jnp.where(i < n - 1, bot, 0.0)

    for cc in range(ch // LANES):
        cs = slice(cc * LANES, (cc + 1) * LANES)
        acc = jnp.zeros((tl, LANES), F32) + b_ref[:, cs]
        for k in range(taps):
            off = hl - pad + k
            acc = acc + w_ref[k:k + 1, cs] * u_ref[off:off + tl, cs]
        if glu:
            y_ref[:, cs] = acc
        else:
            o_ref[0, :, cs] = _silu(acc).astype(o_ref.dtype)

    if glu:
        y = y_ref[...]
        mu = jnp.mean(y, axis=-1, keepdims=True)
        yc = y - mu
        var = jnp.mean(yc * yc, axis=-1, keepdims=True)
        z = yc * lax.rsqrt(var + EPS) * lng_ref[...] + lnb_ref[...]
        o_ref[0] = _silu(z).astype(o_ref.dtype)


def _dwconv(src, off_a, off_g, ch, w, b, ln_g, ln_b, out_dtype):
    bsz, l, _ = src.shape
    taps = w.shape[0]
    glu = off_g is not None
    tl = CONV_TL
    nt = l // tl
    ca = off_a // ch

    def spec(cb, delta):
        def imap(i, t):
            return (i, jnp.clip(t + delta, 0, nt - 1), cb)
        return pl.BlockSpec((1, tl, ch), imap)

    in_specs = [spec(ca, -1), spec(ca, 0), spec(ca, 1)]
    args = [src, src, src]
    if glu:
        cg = off_g // ch
        in_specs += [spec(cg, -1), spec(cg, 0), spec(cg, 1)]
        args += [src, src, src]
    in_specs += [pl.BlockSpec((taps, ch), lambda i, t: (0, 0)), pl.BlockSpec((1, ch), lambda i, t: (0, 0))]
    args += [w, b.reshape(1, ch)]
    scratch = [pltpu.VMEM((tl + 2 * CONV_HALO, ch), F32)]
    if glu:
        in_specs += [pl.BlockSpec((1, ch), lambda i, t: (0, 0)), pl.BlockSpec((1, ch), lambda i, t: (0, 0))]
        args += [ln_g.reshape(1, ch), ln_b.reshape(1, ch)]
        scratch.append(pltpu.VMEM((tl, ch), F32))
    return pl.pallas_call(
        functools.partial(_dwconv_kernel, taps=taps, ch=ch, glu=glu, tl=tl),
        grid=(bsz, nt),
        in_specs=in_specs,
        out_specs=pl.BlockSpec((1, tl, ch), lambda i, t: (i, t, 0)),
        out_shape=jax.ShapeDtypeStruct((bsz, l, ch), out_dtype),
        scratch_shapes=scratch,
        compiler_params=_cparams(("parallel", "parallel")),
        name="dwconv_glu" if glu else "dwconv_ssm",
    )(*args)


def _ssd_kernel(xl_ref, xc_ref, dtl_ref, dtc_ref, dttl_ref, dttc_ref, alog_ref, alogt_ref, dskip_ref,
                exp_ref, yl_ref, yc_ref, h_ref, *, ncc):
    d = pl.program_id(1)
    c = pl.program_id(2)
    q = SSM_CHUNK
    hh = SSM_HEADS
    hpg = SSM_HEADS // SSM_GROUPS
    gw = hpg * SSM_HEAD_DIM
    n = SSM_STATE
    is_ctx = c < ncc
    fwd = d == 0

    @pl.when(c == 0)
    def _():
        h_ref[...] = jnp.zeros_like(h_ref)

    xbc = jnp.where(is_ctx, xc_ref[0], xl_ref[0]).astype(F32)
    x = xbc[:, :SSM_INNER]
    dt2 = jnp.where(is_ctx, dtc_ref[0], dtl_ref[0])
    dt_col = jnp.where(fwd, dt2[:, 0:hh], dt2[:, hh:2 * hh])
    dtt2 = jnp.where(is_ctx, dttc_ref[0], dttl_ref[0])
    dt_row = jnp.where(fwd, dtt2[0:hh, :], dtt2[hh:2 * hh, :])
    alog = alog_ref[...]
    nega_row = -jnp.exp(jnp.where(fwd, alog[0:1, :], alog[1:2, :]))
    alogt = alogt_ref[...]
    nega_col = -jnp.exp(jnp.where(fwd, alogt[:, 0:1], alogt[:, 1:2]))
    a_col = dt_col * nega_row
    a_row = dt_row * nega_col

    li = lax.broadcasted_iota(jnp.int32, (q, q), 0)
    si = lax.broadcasted_iota(jnp.int32, (q, q), 1)
    sgn = jnp.where(fwd, 1, -1)
    tri = (si - li) * sgn <= 0
    trit = (li - si) * sgn <= 0
    tri_b = jnp.where(tri, 1.0, 0.0).astype(BF16)
    trit_b = jnp.where(trit, 1.0, 0.0).astype(BF16)
    e_col = _dot_exact_lhs01(tri_b, a_col)
    e_row = _dot_exact_rhs01(a_row, trit_b)
    e_tot = jnp.where(fwd, e_col[q - 1:q, :], e_col[0:1, :])

    expand = exp_ref[...]
    w_dt = _dot_exact_rhs01(dt_col, expand)
    w_dec = _dot_exact_rhs01(jnp.exp(e_tot - e_col), expand)
    w_off = _dot_exact_rhs01(jnp.exp(e_col), expand)
    w_tot = jnp.where(fwd, w_off[q - 1:q, :], w_off[0:1, :])

    xdt = x * w_dt
    xdt_b = xdt.astype(BF16)
    xdec_b = (xdt * w_dec).astype(BF16)
    lane = lax.broadcasted_iota(jnp.int32, (1, LANES), 1)
    lo = lane < SSM_HEAD_DIM

    y_parts = []
    for g in range(SSM_GROUPS):
        bm = xbc[:, SSM_INNER + g * n:SSM_INNER + (g + 1) * n].astype(BF16)
        cm = xbc[:, SSM_INNER + (SSM_GROUPS + g) * n:SSM_INNER + (SSM_GROUPS + g + 1) * n].astype(BF16)
        cb = _dot_nt(cm, bm)
        hg = h_ref[g]
        y_off = _dot(cm, hg.astype(BF16))
        gs = slice(g * gw, (g + 1) * gw)
        h_ref[g] = hg * w_tot[:, gs] + _dot_tn(bm, xdec_b[:, gs])
        for pr in range(hpg // 2):
            h0 = g * hpg + 2 * pr
            cs = slice(h0 * SSM_HEAD_DIM, (h0 + 2) * SSM_HEAD_DIM)
            xp = xdt_b[:, cs]
            acc = None
            for k in range(2):
                h = h0 + k
                lm = jnp.where(tri, jnp.exp(e_col[:, h:h + 1] - e_row[h:h + 1, :]), 0.0)
                mh = (cb * lm).astype(BF16)
                sel = lo if k == 0 else jnp.logical_not(lo)
                t = _dot(mh, jnp.where(sel, xp, jnp.zeros_like(xp)))
                acc = t if acc is None else acc + t
            ys = slice(2 * pr * SSM_HEAD_DIM, (2 * pr + 2) * SSM_HEAD_DIM)
            y_parts.append(acc + y_off[:, ys] * w_off[:, cs])
    y = jnp.concatenate(y_parts, axis=-1)
    y = y + jnp.where(fwd, dskip_ref[...], 0.0) * x

    @pl.when(is_ctx)
    def _():
        yc_ref[0, 0] = y

    @pl.when(jnp.logical_not(is_ctx))
    def _():
        yl_ref[0, 0] = y


def _ssd_scan(xbc_l, xbc_c, dt_l, dt_c, a_log, d_skip):
    b, l, _ = xbc_l.shape
    cl = xbc_c.shape[1]
    q = SSM_CHUNK
    ncl, ncc = l // q, cl // q
    nc = ncl + ncc
    hh = SSM_HEADS
    dtt_l = jnp.swapaxes(dt_l[..., :2 * hh], 1, 2)
    dtt_c = jnp.swapaxes(dt_c[..., :2 * hh], 1, 2)
    expand = jnp.asarray(np.kron(np.eye(hh), np.ones((1, SSM_HEAD_DIM))), BF16)
    dskip = jnp.repeat(d_skip.astype(F32), SSM_HEAD_DIM).reshape(1, SSM_INNER)

    def lat_chunk(d, c):
        cc = jnp.maximum(c - ncc, 0)
        return jnp.where(d == 0, cc, ncl - 1 - cc)

    def ctx_chunk(d, c):
        cc = jnp.minimum(c, ncc - 1)
        return jnp.where(d == 0, cc, ncc - 1 - cc)

    in_specs = [pl.BlockSpec((1, q, SSM_XBC), lambda i, d, c: (i, lat_chunk(d, c), 0)),
                pl.BlockSpec((1, q, SSM_XBC), lambda i, d, c: (i, ctx_chunk(d, c), 0)),
                pl.BlockSpec((1, q, LANES), lambda i, d, c: (i, lat_chunk(d, c), 0)),
                pl.BlockSpec((1, q, LANES), lambda i, d, c: (i, ctx_chunk(d, c), 0)),
                pl.BlockSpec((1, 2 * hh, q), lambda i, d, c: (i, 0, lat_chunk(d, c))),
                pl.BlockSpec((1, 2 * hh, q), lambda i, d, c: (i, 0, ctx_chunk(d, c))),
                pl.BlockSpec((2, hh), lambda i, d, c: (0, 0)),
                pl.BlockSpec((hh, 2), lambda i, d, c: (0, 0)),
                pl.BlockSpec((1, SSM_INNER), lambda i, d, c: (0, 0)),
                pl.BlockSpec((hh, SSM_INNER), lambda i, d, c: (0, 0))]
    out_specs = [pl.BlockSpec((1, 1, q, SSM_INNER), lambda i, d, c: (i, d, lat_chunk(d, c), 0)),
                 pl.BlockSpec((1, 1, q, SSM_INNER), lambda i, d, c: (i, d, ctx_chunk(d, c), 0))]
    out_shape = [jax.ShapeDtypeStruct((b, 2, l, SSM_INNER), F32),
                 jax.ShapeDtypeStruct((b, 2, cl, SSM_INNER), F32)]
    gw = (SSM_HEADS // SSM_GROUPS) * SSM_HEAD_DIM
    return pl.pallas_call(
        functools.partial(_ssd_kernel, ncc=ncc),
        grid=(b, 2, nc),
        in_specs=in_specs,
        out_specs=out_specs,
        out_shape=out_shape,
        scratch_shapes=[pltpu.VMEM((SSM_GROUPS, SSM_STATE, gw), F32)],
        compiler_params=_cparams(("parallel", "arbitrary", "arbitrary")),
        name="ssd_scan",
    )(xbc_l, xbc_c, dt_l, dt_c, dtt_l, dtt_c, a_log.astype(F32), a_log.astype(F32).T, dskip, expand)


def _merge_kernel(x_ref, g1_ref, gate_ref, att_ref, ys_ref, z_ref, cv_ref, ng_ref,
                  wa_ref, ws_ref, wc_ref, bc_ref, wo_ref, o_ref):
    d = D_MODEL
    ys = (ys_ref[0, 0] + ys_ref[0, 1]) * _silu(z_ref[0].astype(F32))
    gsz = SSM_INNER // SSM_GROUPS
    parts = []
    for g in range(SSM_GROUPS):
        yg = ys[:, g * gsz:(g + 1) * gsz]
        parts.append(yg * lax.rsqrt(jnp.mean(yg * yg, axis=-1, keepdims=True) + EPS))
    yn = (jnp.concatenate(parts, axis=-1) * ng_ref[...]).astype(BF16)
    p_s = _dot(yn, ws_ref[...])
    p_a = _dot(att_ref[0], wa_ref[...])
    p_c = _dot(cv_ref[0], wc_ref[...]) + bc_ref[...]
    gl = gate_ref[0].astype(F32)
    m = (_sigmoid(gl[:, 0:d]) * p_a + _sigmoid(gl[:, d:2 * d]) * p_s
         + _sigmoid(gl[:, 2 * d:3 * d]) * p_c).astype(BF16)
    o_ref[0] = x_ref[0] + g1_ref[0] * _dot(m, wo_ref[...])


def _merge(x, g1, packed, y_att, y_ssm2, y_cv, norm_g, wa, ws, wc, bc, wo):
    b, l, d = x.shape
    tm = 256
    bm = g1.shape[0]
    g_map = (lambda i, t: (i, 0, 0)) if bm > 1 else (lambda i, t: (0, 0, 0))
    const = lambda i, t: (0, 0)
    one = pl.Buffered(1)
    in_specs = [pl.BlockSpec((1, tm, d), lambda i, t: (i, t, 0)),
                pl.BlockSpec((1, 1, d), g_map),
                pl.BlockSpec((1, tm, 3 * d), lambda i, t: (i, t, OFF_GATE // (3 * d))),
                pl.BlockSpec((1, tm, NA_WIDTH), lambda i, t: (i, t, 0)),
                pl.BlockSpec((1, 2, tm, SSM_INNER), lambda i, t: (i, 0, t, 0)),
                pl.BlockSpec((1, tm, SSM_INNER), lambda i, t: (i, t, OFF_Z // SSM_INNER)),
                pl.BlockSpec((1, tm, CONV_CH), lambda i, t: (i, t, 0)),
                pl.BlockSpec((1, SSM_INNER), const),
                pl.BlockSpec((NA_WIDTH, d), const, pipeline_mode=one),
                pl.BlockSpec((SSM_INNER, d), const, pipeline_mode=one),
                pl.BlockSpec((CONV_CH, d), const, pipeline_mode=one),
                pl.BlockSpec((1, d), const),
                pl.BlockSpec((d, d), const, pipeline_mode=one)]
    return pl.pallas_call(
        _merge_kernel,
        grid=(b, l // tm),
        in_specs=in_specs,
        out_specs=pl.BlockSpec((1, tm, d), lambda i, t: (i, t, 0)),
        out_shape=jax.ShapeDtypeStruct((b, l, d), F32),
        compiler_params=_cparams(("parallel", "parallel")),
        name="merge",
    )(x, g1.reshape(bm, 1, d), packed, y_att, y_ssm2, packed, y_cv, norm_g.reshape(1, -1),
      wa, ws, wc, bc.reshape(1, d), wo)


PEER_TQ = 256
NEG_INF = float("-inf")


def _sort_network(n):
    pairs = []

    def merge(lo, hi, r):
        step = r * 2
        if step < hi - lo:
            merge(lo, hi, step)
            merge(lo + r, hi, step)
            pairs.extend((i, i + r) for i in range(lo + r, hi - r, step))
        else:
            pairs.append((lo, lo + r))

    def sort(lo, hi):
        if hi - lo >= 1:
            mid = lo + (hi - lo) // 2
            sort(lo, mid)
            sort(mid + 1, hi)
            merge(lo, hi, 1)

    sort(0, n - 1)
    return pairs


def _exchange(v, i, j):
    a, b = v[i], v[j]
    v[i], v[j] = jnp.maximum(a, b), jnp.minimum(a, b)


def _top_k_sorted(v):
    k = len(v)
    v = list(v)
    for i, j in _sort_network(k):
        _exchange(v, i, j)
    for shift in (4, 2, 1):
        other = [pltpu.roll(x, shift, axis=0) for x in v]
        v = [jnp.maximum(v[i], other[k - 1 - i]) for i in range(k)]
        dist = k // 2
        while dist >= 1:
            for i in range(k):
                if i & dist == 0:
                    _exchange(v, i, i + dist)
            dist //= 2
    return v


def _peer_select_kernel(q_ref, keys_ref, s_ref, st_ref, top_ref):
    tq = PEER_TQ
    k = PEER_TOPK
    sub = lax.broadcasted_iota(jnp.int32, (8, tq), 0)

    def per_set(i, carry):
        c0 = pl.multiple_of(i * PEER_KEY_DIM, PEER_KEY_DIM)
        s = _dot_nt(keys_ref[i], q_ref[:, pl.ds(c0, PEER_KEY_DIM)])
        s_ref[i] = s
        top = _top_k_sorted([s[8 * r:8 * r + 8, :] for r in range(PEER_NKEYS // 8)])
        for r in range(k):
            top_ref[i, r] = top[r]
        return carry

    lax.fori_loop(0, 2 * PEER_HEADS, per_set, 0)

    def spread(rows):
        out = rows[7]
        for s in range(6, -1, -1):
            out = jnp.where(sub == s, rows[s], out)
        return out

    def per_head(h, carry):
        t1 = [top_ref[2 * h, r] for r in range(k)]
        t2 = [top_ref[2 * h + 1, r] for r in range(k)]
        p2a, p2b, p1b = spread(t2[:8]), spread(t2[8:]), spread(t1[8:])
        cand = [t1[0] + p2a, t1[0] + p2b] + [t1[i] + p2a for i in range(1, 8)] + [p1b + t2[0]]
        cand += [jnp.full((8, tq), NEG_INF, F32)] * (k - len(cand))
        best = _top_k_sorted(cand)
        z = jnp.ones((8, tq), F32)
        for r in range(1, k):
            z = z + jnp.exp(best[r] - best[0])
        st_ref[h] = jnp.where(sub == 0, best[k - 1],
                              jnp.where(sub == 1, t1[0], jnp.where(sub == 2, t2[0], 1.0 / z)))
        return carry

    lax.fori_loop(0, PEER_HEADS, per_head, 0)


def _peer_select(q, keys):
    t = q.shape[0]
    tq = PEER_TQ
    nset = 2 * PEER_HEADS
    return pl.pallas_call(
        _peer_select_kernel,
        grid=(t // tq,),
        in_specs=[pl.BlockSpec((tq, q.shape[1]), lambda i: (i, 0)),
                  pl.BlockSpec((nset, PEER_NKEYS, PEER_KEY_DIM), lambda i: (0, 0, 0))],
        out_specs=[pl.BlockSpec((nset, PEER_NKEYS, tq), lambda i: (0, 0, i)),
                   pl.BlockSpec((PEER_HEADS, 8, tq), lambda i: (0, 0, i))],
        out_shape=[jax.ShapeDtypeStruct((nset, PEER_NKEYS, t), F32),
                   jax.ShapeDtypeStruct((PEER_HEADS, 8, t), F32)],
        scratch_shapes=[pltpu.VMEM((nset, PEER_TOPK, 8, tq), F32)],
        compiler_params=_cparams(("parallel",)),
        name="peer_select",
    )(q, keys)


PEER_TM = 512
PEER_TE = 1024
PEER_KEY_BLOCK = 32
PEER_ROW_BLOCK = 4


def _gelu(x):
    return 0.5 * x * (1.0 + lax.erf(x * np.float32(math.sqrt(0.5))))


def _peer_dense_kernel(h_ref, u_ref, vt_ref, s_ref, st_ref, x_ref, g2_ref, o_ref,
                       acc_ref, e_ref, at0_ref, at1_ref, wa0_ref, wa1_ref):
    j = pl.program_id(1)
    nsteps = pl.num_programs(1)
    nblk = nsteps - 2
    tm = PEER_TM
    te = PEER_TE
    d = h_ref.shape[1]
    nk = PEER_NKEYS

    @pl.when(j == 0)
    def _():
        acc_ref[...] = jnp.zeros_like(acc_ref)
        at1_ref[...] = jnp.zeros((te, tm), F32)
        wa0_ref[...] = jnp.zeros((te, tm), BF16)
        for h in range(PEER_HEADS):
            m1 = st_ref[h, 1:2, :]
            m2 = st_ref[h, 2:3, :]
            rz = st_ref[h, 3:4, :]
            e_ref[2 * h] = jnp.exp(s_ref[2 * h] - m1) * rz
            e_ref[2 * h + 1] = jnp.exp(s_ref[2 * h + 1] - m2)

    na = te // nk
    ja = jnp.clip(j - 1, 0, nblk - 1)
    kb = PEER_KEY_BLOCK
    kg = kb // 8
    nlc = tm // LANES
    nq = nk // kb
    dk = d // nlc
    ek = te // nlc
    m1 = te // nq
    m3 = d // nq

    def step(at_w, at_r, wa_w, wa_r):
        for lc, kq in itertools.product(range(nlc), range(nq)):
            ls = slice(lc * LANES, (lc + 1) * LANES)
            r1 = slice(kq * m1, (kq + 1) * m1)
            kc = slice(lc * dk, (lc + 1) * dk)
            part = _dot_nt(u_ref[r1, kc], h_ref[:, kc])
            if lc == 0:
                at_w[r1, :] = part
            else:
                at_w[r1, :] += part
            r3 = slice(kq * m3, (kq + 1) * m3)
            ec = slice(lc * ek, (lc + 1) * ek)
            acc_ref[r3, :] += _dot(vt_ref[r3, ec], wa_r[ec, :])
            for ab in range(na // PEER_ROW_BLOCK):
                gs = slice(kq * kg, (kq + 1) * kg)
                als = range(ab * PEER_ROW_BLOCK, (ab + 1) * PEER_ROW_BLOCK)
                w = {al: jnp.zeros((kg, 8, LANES), F32) for al in als}
                for h in range(PEER_HEADS):
                    thr = st_ref[h, 0:1, ls]
                    s2 = s_ref[2 * h + 1, gs, :, ls]
                    e2 = e_ref[2 * h + 1, gs, :, ls]
                    for al in als:
                        s1 = s_ref[2 * h, ja, al:al + 1, ls]
                        e1 = e_ref[2 * h, ja, al:al + 1, ls]
                        w[al] = w[al] + jnp.where(s2 + s1 >= thr, e2, 0.0) * e1
                for al in als:
                    rs = slice(al * nk + kq * kb, al * nk + (kq + 1) * kb)
                    wa_w[rs, ls] = (w[al].reshape(kb, LANES) * _gelu(at_r[rs, ls])).astype(BF16)

    @pl.when(j % 2 == 0)
    def _():
        step(at0_ref, at1_ref, wa1_ref, wa0_ref)

    @pl.when(j % 2 == 1)
    def _():
        step(at1_ref, at0_ref, wa0_ref, wa1_ref)

    @pl.when(j == nsteps - 1)
    def _():
        o_ref[...] = x_ref[...] + g2_ref[0] * acc_ref[...].T


def _peer_dense(hh, u, vt, scores, stats, x, g2, tiles_per_row):
    t, d = hh.shape
    ne = u.shape[0]
    tm, te = PEER_TM, PEER_TE
    one = pl.Buffered(1)
    nblk = ne // te
    return pl.pallas_call(
        _peer_dense_kernel,
        grid=(t // tm, nblk + 2),
        in_specs=[pl.BlockSpec((tm, d), lambda i, j: (i, 0), pipeline_mode=one),
                  pl.BlockSpec((te, d), lambda i, j: (jnp.minimum(j, nblk - 1), 0)),
                  pl.BlockSpec((d, te), lambda i, j: (0, jnp.maximum(j - 2, 0))),
                  pl.BlockSpec((2 * PEER_HEADS, PEER_NKEYS // 8, 8, tm), lambda i, j: (0, 0, 0, i),
                               pipeline_mode=one),
                  pl.BlockSpec((PEER_HEADS, 8, tm), lambda i, j: (0, 0, i), pipeline_mode=one),
                  pl.BlockSpec((tm, d), lambda i, j: (i, 0), pipeline_mode=one),
                  pl.BlockSpec((1, 1, d), lambda i, j: (i // tiles_per_row, 0, 0))],
        out_specs=pl.BlockSpec((tm, d), lambda i, j: (i, 0)),
        out_shape=jax.ShapeDtypeStruct((t, d), F32),
        scratch_shapes=[pltpu.VMEM((d, tm), F32),
                        pltpu.VMEM((2 * PEER_HEADS, PEER_NKEYS // 8, 8, tm), F32),
                        pltpu.VMEM((te, tm), F32), pltpu.VMEM((te, tm), F32),
                        pltpu.VMEM((te, tm), BF16), pltpu.VMEM((te, tm), BF16)],
        compiler_params=_cparams(("parallel", "arbitrary")),
        name="peer_dense",
    )(hh, u, vt, scores.reshape(2 * PEER_HEADS, PEER_NKEYS // 8, 8, t), stats, x, g2)


def _peer_block(x, g, shift, scale, gate2, wq, keys, u, vt):
    b, l, d = x.shape
    t = b * l
    x_rows = x if shift.shape[0] > 1 else x.reshape(1, t, d)
    q, hh = _norm_mod_matmul(x_rows, g, shift, scale, wq, BF16, 512, emit_h=True)
    scores, stats = _peer_select(q.reshape(t, -1), keys)
    bm = gate2.shape[0]
    tiles_per_row = (l // PEER_TM) if bm > 1 else (t // PEER_TM)
    out = _peer_dense(hh.reshape(t, d), u, vt, scores, stats, x.reshape(t, d),
                      gate2.reshape(bm, 1, d), tiles_per_row)
    return out.reshape(b, l, d)


def _rmsnorm_kernel(x_ref, g_ref, o_ref):
    x = x_ref[...]
    o_ref[...] = x * lax.rsqrt(jnp.mean(x * x, axis=-1, keepdims=True) + EPS) * g_ref[...]


def _rmsnorm(x, g):
    t, d = x.shape
    tm = 512
    return pl.pallas_call(
        _rmsnorm_kernel,
        grid=(t // tm,),
        in_specs=[pl.BlockSpec((tm, d), lambda i: (i, 0)), pl.BlockSpec((1, d), lambda i: (0, 0))],
        out_specs=pl.BlockSpec((tm, d), lambda i: (i, 0)),
        out_shape=jax.ShapeDtypeStruct((t, d), F32),
        compiler_params=_cparams(("parallel",)),
        name="final_rmsnorm",
    )(x, g.reshape(1, d))


def _pack_w_in(w_in):
    q, k, v, z, xbc, dt, glu, gate = jnp.split(
        w_in, np.cumsum([NA_WIDTH, NA_WIDTH, NA_WIDTH, SSM_INNER, SSM_XBC, 2 * SSM_HEADS, 2 * CONV_CH]).tolist(),
        axis=-1)
    packed = jnp.concatenate([gate, glu, q, k, v, z, xbc], axis=-1).astype(BF16)
    dt_w = jnp.pad(dt, ((0, 0), (0, LANES - 2 * SSM_HEADS))).astype(BF16)
    return packed, dt_w


def _trunk_layer(xl, xc, c_rows, need_ctx, w_mod, b_mod, norm1_g, norm2_g, w_in, na_rpb, na_wo,
                 ssm_conv_w, ssm_conv_b, ssm_dt_bias, ssm_a_log, ssm_d, ssm_norm_g, ssm_wo,
                 cv_dw_w, cv_dw_b, cv_ln_g, cv_ln_b, cv_wo, cv_bo, w_out, peer_wq, peer_keys, peer_u, peer_v):
    b, s, d = xl.shape
    mod = _modulation(c_rows, w_mod, b_mod)
    sh1, sc1, g1, sh2, sc2, g2 = [mod[:b, i * d:(i + 1) * d] for i in range(6)]
    csh1, csc1, cg1, csh2, csc2, cg2 = [mod[b:b + 1, i * d:(i + 1) * d] for i in range(6)]

    w_pack, w_dt = _pack_w_in(w_in)
    dt_bias = jnp.pad(ssm_dt_bias.astype(F32).reshape(-1), (0, LANES - 2 * SSM_HEADS))

    cb, cl, _ = xc.shape
    xc_flat = xc.reshape(1, cb * cl, d)
    p_l = _norm_mod_matmul(xl, norm1_g, sh1, sc1, w_pack, BF16, 512)
    p_c = _norm_mod_matmul(xc_flat, norm1_g, csh1, csc1, w_pack, BF16, 512).reshape(cb, cl, -1)
    dt_l = _norm_mod_matmul(xl, norm1_g, sh1, sc1, w_dt, F32, LANES, softplus_bias=dt_bias)
    dt_c = _norm_mod_matmul(xc_flat, norm1_g, csh1, csc1, w_dt, F32, LANES,
                            softplus_bias=dt_bias).reshape(cb, cl, -1)

    bias_tab = _na_bias_table(na_rpb, s // GRID_W)
    y_att, y_att_c = _neighbourhood_attention(p_l, p_c, bias_tab, need_ctx)

    xbc_l = _dwconv(p_l, OFF_XBC, None, SSM_XBC, ssm_conv_w, ssm_conv_b, None, None, F32)
    xbc_c = _dwconv(p_c, OFF_XBC, None, SSM_XBC, ssm_conv_w, ssm_conv_b, None, None, F32)
    y_ssm, y_ssm_c = _ssd_scan(xbc_l, xbc_c, dt_l, dt_c, ssm_a_log, ssm_d)

    wa, ws, wc, wo = (na_wo.astype(BF16), ssm_wo.astype(BF16), cv_wo.astype(BF16), w_out.astype(BF16))
    y_cv = _dwconv(p_l, OFF_GLU_A, OFF_GLU_G, CONV_CH, cv_dw_w, cv_dw_b, cv_ln_g, cv_ln_b, BF16)
    xl = _merge(xl, g1, p_l, y_att, y_ssm, y_cv, ssm_norm_g, wa, ws, wc, cv_bo, wo)
    if need_ctx:
        y_cv_c = _dwconv(p_c, OFF_GLU_A, OFF_GLU_G, CONV_CH, cv_dw_w, cv_dw_b, cv_ln_g, cv_ln_b, BF16)
        xc = _merge(xc, cg1, p_c, y_att_c, y_ssm_c, y_cv_c, ssm_norm_g, wa, ws, wc, cv_bo, wo)

    wq = peer_wq.astype(BF16)
    keys = peer_keys.astype(BF16).reshape(2 * PEER_HEADS, PEER_NKEYS, PEER_KEY_DIM)
    u = peer_u.astype(BF16)
    vt = peer_v.astype(BF16).T
    xl = _peer_block(xl, norm2_g, sh2, sc2, g2, wq, keys, u, vt)
    if need_ctx:
        xc = _peer_block(xc, norm2_g, csh2, csc2, cg2, wq, keys, u, vt)
    return xl, xc


def kernel(x, c, ctx, c_ctx, w_mod, b_mod, norm1_g, norm2_g, w_in, na_rpb, na_wo, ssm_conv_w, ssm_conv_b,
           ssm_dt_bias, ssm_A_log, ssm_D, ssm_norm_g, ssm_wo, cv_dw_w, cv_dw_b, cv_ln_g, cv_ln_b, cv_wo,
           cv_bo, w_out, peer_wq, peer_keys, peer_u, peer_v, final_norm_g):
    b, s, d = x.shape
    depth = w_mod.shape[0]
    rows = -(-(b + 1) // 8) * 8
    c_rows = jnp.zeros((rows, d), F32).at[:b].set(c).at[b].set(c_ctx)
    xl, xc = x, ctx
    for l in range(depth):
        xl, xc = _trunk_layer(xl, xc, c_rows, l < depth - 1, w_mod[l], b_mod[l], norm1_g[l], norm2_g[l],
                              w_in[l], na_rpb[l], na_wo[l], ssm_conv_w[l], ssm_conv_b[l], ssm_dt_bias[l],
                              ssm_A_log[l], ssm_D[l], ssm_norm_g[l], ssm_wo[l], cv_dw_w[l], cv_dw_b[l],
                              cv_ln_g[l], cv_ln_b[l], cv_wo[l], cv_bo[l], w_out[l],
                              peer_wq[l], peer_keys[l], peer_u[l], peer_v[l])
    return _rmsnorm(xl.reshape(b * s, d), final_norm_g).reshape(b, s, d)
```

```python
import functools
import itertools
import math

import numpy as np
import jax
import jax.numpy as jnp
from jax import lax
from jax.experimental import pallas as pl
from jax.experimental.pallas import tpu as pltpu

F32 = jnp.float32
BF16 = jnp.bfloat16
EPS = 1e-6

D_MODEL = 2048
GRID_W = 64
NA_HEADS = 16
NA_HEAD_DIM = 64
NA_WIDTH = NA_HEADS * NA_HEAD_DIM
WIN_R = 8
WIN_C = 16
SSM_HEADS = 16
SSM_HEAD_DIM = 64
SSM_INNER = SSM_HEADS * SSM_HEAD_DIM
SSM_GROUPS = 2
SSM_STATE = 128
SSM_XBC = SSM_INNER + 2 * SSM_GROUPS * SSM_STATE
SSM_CONV = 5
SSM_CHUNK = 128
CONV_CH = 1024
CONV_K = 31
PEER_HEADS = 8
PEER_NKEYS = 128
PEER_KEY_DIM = 128
PEER_TOPK = 16
PEER_EXPERTS = PEER_NKEYS * PEER_NKEYS

OFF_GATE = 0
OFF_GLU_A = 6144
OFF_GLU_G = 7168
OFF_Q = 8192
OFF_K = 9216
OFF_V = 10240
OFF_Z = 11264
OFF_XBC = 12288
PACK_COLS = 13824

LANES = 128
VMEM_LIMIT = 56 * 1024 * 1024


def _cparams(sem):
    return pltpu.CompilerParams(dimension_semantics=sem, vmem_limit_bytes=VMEM_LIMIT)


def _dot(a, b):
    return jnp.dot(a, b, preferred_element_type=F32)


def _dot_nt(a, b):
    return lax.dot_general(a, b, (((1,), (1,)), ((), ())), preferred_element_type=F32)


def _dot_tn(a, b):
    return lax.dot_general(a, b, (((0,), (0,)), ((), ())), preferred_element_type=F32)


def _split3(a):
    hi = a.astype(BF16)
    r1 = a - hi.astype(F32)
    mid = r1.astype(BF16)
    lo = (r1 - mid.astype(F32)).astype(BF16)
    return hi, mid, lo


def _dot_exact_rhs01(a, m01):
    hi, mid, lo = _split3(a)
    return _dot(hi, m01) + _dot(mid, m01) + _dot(lo, m01)


def _dot_exact_lhs01(m01, a):
    hi, mid, lo = _split3(a)
    return _dot(m01, hi) + _dot(m01, mid) + _dot(m01, lo)


def _sigmoid(x):
    return 1.0 / (1.0 + jnp.exp(-x))


def _silu(x):
    return x * _sigmoid(x)


def _mod_kernel(c_ref, w_ref, b_ref, o_ref):
    a = _silu(c_ref[...]).astype(BF16)
    o_ref[...] = _dot(a, w_ref[...].astype(BF16)) + b_ref[...]


def _modulation(cs, w_mod, b_mod):
    r, d = cs.shape
    n = w_mod.shape[1]
    tn = 1024
    return pl.pallas_call(
        _mod_kernel,
        grid=(n // tn,),
        in_specs=[pl.BlockSpec((r, d), lambda j: (0, 0)),
                  pl.BlockSpec((d, tn), lambda j: (0, j)),
                  pl.BlockSpec((1, tn), lambda j: (0, j))],
        out_specs=pl.BlockSpec((r, tn), lambda j: (0, j)),
        out_shape=jax.ShapeDtypeStruct((r, n), F32),
        compiler_params=_cparams(("arbitrary",)),
        name="modulation",
    )(cs, w_mod, b_mod.reshape(1, n))


def _nmm_kernel(x_ref, g_ref, sh_ref, sc_ref, w_ref, *rest, softplus_bias, emit_h):
    if softplus_bias:
        bias_ref, rest = rest[0], rest[1:]
    if emit_h:
        o_ref, ho_ref, h_ref = rest
    else:
        o_ref, h_ref = rest

    @pl.when(pl.program_id(2) == 0)
    def _():
        x = x_ref[0]
        ms = jnp.mean(x * x, axis=-1, keepdims=True)
        y = x * lax.rsqrt(ms + EPS) * g_ref[...]
        h = (y * (1.0 + sc_ref[0]) + sh_ref[0]).astype(BF16)
        h_ref[...] = h
        if emit_h:
            ho_ref[0] = h

    acc = _dot(h_ref[...], w_ref[...])
    if softplus_bias:
        t = acc + bias_ref[...]
        acc = jnp.maximum(t, 0.0) + jnp.log1p(jnp.exp(-jnp.abs(t)))
    o_ref[0] = acc.astype(o_ref.dtype)


def _norm_mod_matmul(x, g, shift, scale, w, out_dtype, tn, softplus_bias=None, emit_h=False):
    b, l, d = x.shape
    n = w.shape[1]
    tm = min(l, 1024)
    bm = shift.shape[0]
    mod_map = (lambda i, m, j: (i, 0, 0)) if bm > 1 else (lambda i, m, j: (0, 0, 0))
    in_specs = [pl.BlockSpec((1, tm, d), lambda i, m, j: (i, m, 0)),
                pl.BlockSpec((1, d), lambda i, m, j: (0, 0)),
                pl.BlockSpec((1, 1, d), mod_map),
                pl.BlockSpec((1, 1, d), mod_map),
                pl.BlockSpec((d, tn), lambda i, m, j: (0, j))]
    args = [x, g.reshape(1, d), shift.reshape(bm, 1, d), scale.reshape(bm, 1, d), w]
    if softplus_bias is not None:
        in_specs.append(pl.BlockSpec((1, tn), lambda i, m, j: (0, j)))
        args.append(softplus_bias.reshape(1, n))
    out_shape = [jax.ShapeDtypeStruct((b, l, n), out_dtype)]
    out_specs = [pl.BlockSpec((1, tm, tn), lambda i, m, j: (i, m, j))]
    if emit_h:
        out_shape.append(jax.ShapeDtypeStruct((b, l, d), BF16))
        out_specs.append(pl.BlockSpec((1, tm, d), lambda i, m, j: (i, m, 0)))
    res = pl.pallas_call(
        functools.partial(_nmm_kernel, softplus_bias=softplus_bias is not None, emit_h=emit_h),
        grid=(b, l // tm, n // tn),
        in_specs=in_specs,
        out_specs=out_specs,
        out_shape=out_shape,
        scratch_shapes=[pltpu.VMEM((tm, d), BF16)],
        compiler_params=_cparams(("parallel", "parallel", "arbitrary")),
        name="norm_mod_matmul",
    )(*args)
    return res if emit_h else res[0]


def _na_bias_table(rpb, rows):
    wr = min(WIN_R, rows)
    j = np.arange(GRID_W)
    cstart = np.clip(j - WIN_C // 2, 0, GRID_W - WIN_C)
    kc = np.arange(GRID_W)
    mask = (kc[None, :] >= cstart[:, None]) & (kc[None, :] < cstart[:, None] + WIN_C)
    col_off = np.clip(kc[None, :] - j[:, None], -(WIN_C - 1), WIN_C - 1) + WIN_C - 1
    dd = np.arange(wr)[:, None]
    ww = np.arange(wr)[None, :]
    row_idx = ww - dd + WIN_R - 1
    t = rpb.astype(F32)[:, row_idx][:, :, :, col_off]
    t = jnp.where(mask[None, None, None], t, -1e30)
    t = t.transpose(0, 1, 3, 2, 4).reshape(NA_HEADS, wr, GRID_W, wr * GRID_W)
    return t.reshape(NA_HEADS // 2, 2, wr, GRID_W, wr * GRID_W)


NA_ROW_GROUP = 8


def _na_kernel(q_ref, k_ref, v_ref, kc_ref, vc_ref, bias_ref, *rest, rows, wr, with_ctx):
    if with_ctx:
        qc_ref, o_ref, oc_ref = rest
    else:
        (o_ref,) = rest
    lane = lax.broadcasted_iota(jnp.int32, (1, LANES), 1)
    lo = lane < NA_HEAD_DIM
    kc = kc_ref[0]
    vc = vc_ref[0]
    scale = NA_HEAD_DIM ** -0.5
    nwin = wr * GRID_W

    def attend(problems):
        chains = []
        for q, kw, vw, bias_of in problems:
            for h in range(2):
                sel = lo if h == 0 else jnp.logical_not(lo)
                qh = jnp.where(sel, q, jnp.zeros_like(q))
                s_c = _dot_nt(qh, kc) * scale
                s_w = None if kw is None else _dot_nt(qh, kw) * scale + bias_of(h)
                chains.append((s_c, s_w, vw))
        maxes = []
        for s_c, s_w, _ in chains:
            m = jnp.max(s_c, axis=-1, keepdims=True)
            if s_w is not None:
                m = jnp.maximum(m, jnp.max(s_w, axis=-1, keepdims=True))
            maxes.append(m)
        probs = []
        for (s_c, s_w, _), m in zip(chains, maxes):
            probs.append((jnp.exp(s_c - m), None if s_w is None else jnp.exp(s_w - m)))
        outs = []
        for (_, _, vw), (p_c, p_w) in zip(chains, probs):
            den = jnp.sum(p_c, axis=-1, keepdims=True)
            o = _dot(p_c.astype(BF16), vc)
            if p_w is not None:
                den = den + jnp.sum(p_w, axis=-1, keepdims=True)
                o = o + _dot(p_w.astype(BF16), vw)
            outs.append(o / den)
        return [jnp.where(lo, outs[2 * i], outs[2 * i + 1]) for i in range(len(problems))]

    def body(g, carry):
        problems, q0s = [], []
        for rr in range(NA_ROW_GROUP):
            r = g * NA_ROW_GROUP + rr
            rs = jnp.clip(r - wr // 2, 0, rows - wr)
            d = r - rs
            q0 = pl.multiple_of(r * GRID_W, GRID_W)
            k0 = pl.multiple_of(rs * GRID_W, GRID_W)
            problems.append((q_ref[0, pl.ds(q0, GRID_W), :], k_ref[0, pl.ds(k0, nwin), :],
                             v_ref[0, pl.ds(k0, nwin), :], lambda h, d=d: bias_ref[0, h, d]))
            q0s.append(q0)
        for q0, o in zip(q0s, attend(problems)):
            o_ref[0, pl.ds(q0, GRID_W), :] = o.astype(o_ref.dtype)
        return carry

    lax.fori_loop(0, rows // NA_ROW_GROUP, body, 0)
    if with_ctx:
        oc_ref[0] = attend([(qc_ref[0], None, None, None)])[0].astype(oc_ref.dtype)


def _neighbourhood_attention(pl_lat, pl_ctx, bias_tab, with_ctx):
    b, s, _ = pl_lat.shape
    cl = pl_ctx.shape[1]
    rows = s // GRID_W
    wr = min(WIN_R, rows)
    nhp = NA_HEADS // 2
    qb, kb, vb = OFF_Q // LANES, OFF_K // LANES, OFF_V // LANES
    in_specs = [pl.BlockSpec((1, s, LANES), lambda hp, i: (i, 0, qb + hp)),
                pl.BlockSpec((1, s, LANES), lambda hp, i: (i, 0, kb + hp)),
                pl.BlockSpec((1, s, LANES), lambda hp, i: (i, 0, vb + hp)),
                pl.BlockSpec((1, cl, LANES), lambda hp, i: (i, 0, kb + hp)),
                pl.BlockSpec((1, cl, LANES), lambda hp, i: (i, 0, vb + hp)),
                pl.BlockSpec((1, 2, wr, GRID_W, wr * GRID_W), lambda hp, i: (hp, 0, 0, 0, 0))]
    args = [pl_lat, pl_lat, pl_lat, pl_ctx, pl_ctx, bias_tab]
    out_shape = [jax.ShapeDtypeStruct((b, s, NA_WIDTH), BF16)]
    out_specs = [pl.BlockSpec((1, s, LANES), lambda hp, i: (i, 0, hp))]
    if with_ctx:
        in_specs.append(pl.BlockSpec((1, cl, LANES), lambda hp, i: (i, 0, qb + hp)))
        args.append(pl_ctx)
        out_shape.append(jax.ShapeDtypeStruct((b, cl, NA_WIDTH), BF16))
        out_specs.append(pl.BlockSpec((1, cl, LANES), lambda hp, i: (i, 0, hp)))
    res = pl.pallas_call(
        functools.partial(_na_kernel, rows=rows, wr=wr, with_ctx=with_ctx),
        grid=(nhp, b),
        in_specs=in_specs,
        out_specs=out_specs,
        out_shape=out_shape,
        compiler_params=_cparams(("parallel", "parallel")),
        name="neighbourhood_attention",
    )(*args)
    return (res[0], res[1]) if with_ctx else (res[0], None)


CONV_TL = 256
CONV_HALO = 16


def _dwconv_kernel(*refs, taps, ch, glu, tl):
    if glu:
        (a_p, a_c, a_n, g_p, g_c, g_n, w_ref, b_ref, lng_ref, lnb_ref, o_ref, u_ref, y_ref) = refs
    else:
        (a_p, a_c, a_n, w_ref, b_ref, o_ref, u_ref) = refs
    i = pl.program_id(1)
    n = pl.num_programs(1)
    pad = (taps - 1) // 2
    hl = CONV_HALO

    def pre(a, g):
        a = a.astype(F32)
        if glu:
            return a * _sigmoid(g.astype(F32))
        return a

    u_ref[hl:hl + tl, :] = pre(a_c[0], g_c[0] if glu else None)
    top = pre(a_p[0, tl - hl:tl, :], g_p[0, tl - hl:tl, :] if glu else None)
    u_ref[0:hl, :] = jnp.where(i > 0, top, 0.0)
    bot = pre(a_n[0, 0:hl, :], g_n[0, 0:hl, :] if glu else None)
    u_ref[hl + tl:hl + tl + hl, :] = jnp.where(i < n - 1, bot, 0.0)

    for cc in range(ch // LANES):
        cs = slice(cc * LANES, (cc + 1) * LANES)
        acc = jnp.zeros((tl, LANES), F32) + b_ref[:, cs]
        for k in range(taps):
            off = hl - pad + k
            acc = acc + w_ref[k:k + 1, cs] * u_ref[off:off + tl, cs]
        if glu:
            y_ref[:, cs] = acc
        else:
            o_ref[0, :, cs] = _silu(acc).astype(o_ref.dtype)

    if glu:
        y = y_ref[...]
        mu = jnp.mean(y, axis=-1, keepdims=True)
        yc = y - mu
        var = jnp.mean(yc * yc, axis=-1, keepdims=True)
        z = yc * lax.rsqrt(var + EPS) * lng_ref[...] + lnb_ref[...]
        o_ref[0] = _silu(z).astype(o_ref.dtype)


def _dwconv(src, off_a, off_g, ch, w, b, ln_g, ln_b, out_dtype):
    bsz, l, _ = src.shape
    taps = w.shape[0]
    glu = off_g is not None
    tl = CONV_TL
    nt = l // tl
    ca = off_a // ch

    def spec(cb, delta):
        def imap(i, t):
            return (i, jnp.clip(t + delta, 0, nt - 1), cb)
        return pl.BlockSpec((1, tl, ch), imap)

    in_specs = [spec(ca, -1), spec(ca, 0), spec(ca, 1)]
    args = [src, src, src]
    if glu:
        cg = off_g // ch
        in_specs += [spec(cg, -1), spec(cg, 0), spec(cg, 1)]
        args += [src, src, src]
    in_specs += [pl.BlockSpec((taps, ch), lambda i, t: (0, 0)), pl.BlockSpec((1, ch), lambda i, t: (0, 0))]
    args += [w, b.reshape(1, ch)]
    scratch = [pltpu.VMEM((tl + 2 * CONV_HALO, ch), F32)]
    if glu:
        in_specs += [pl.BlockSpec((1, ch), lambda i, t: (0, 0)), pl.BlockSpec((1, ch), lambda i, t: (0, 0))]
        args += [ln_g.reshape(1, ch), ln_b.reshape(1, ch)]
        scratch.append(pltpu.VMEM((tl, ch), F32))
    return pl.pallas_call(
        functools.partial(_dwconv_kernel, taps=taps, ch=ch, glu=glu, tl=tl),
        grid=(bsz, nt),
        in_specs=in_specs,
        out_specs=pl.BlockSpec((1, tl, ch), lambda i, t: (i, t, 0)),
        out_shape=jax.ShapeDtypeStruct((bsz, l, ch), out_dtype),
        scratch_shapes=scratch,
        compiler_params=_cparams(("parallel", "parallel")),
        name="dwconv_glu" if glu else "dwconv_ssm",
    )(*args)


def _ssd_kernel(xl_ref, xc_ref, dtl_ref, dtc_ref, dttl_ref, dttc_ref, alog_ref, alogt_ref, dskip_ref,
                exp_ref, yl_ref, yc_ref, h_ref, *, ncc):
    d = pl.program_id(1)
    c = pl.program_id(2)
    q = SSM_CHUNK
    hh = SSM_HEADS
    hpg = SSM_HEADS // SSM_GROUPS
    gw = hpg * SSM_HEAD_DIM
    n = SSM_STATE
    is_ctx = c < ncc
    fwd = d == 0

    @pl.when(c == 0)
    def _():
        h_ref[...] = jnp.zeros_like(h_ref)

    xbc = jnp.where(is_ctx, xc_ref[0], xl_ref[0]).astype(F32)
    x = xbc[:, :SSM_INNER]
    dt2 = jnp.where(is_ctx, dtc_ref[0], dtl_ref[0])
    dt_col = jnp.where(fwd, dt2[:, 0:hh], dt2[:, hh:2 * hh])
    dtt2 = jnp.where(is_ctx, dttc_ref[0], dttl_ref[0])
    dt_row = jnp.where(fwd, dtt2[0:hh, :], dtt2[hh:2 * hh, :])
    alog = alog_ref[...]
    nega_row = -jnp.exp(jnp.where(fwd, alog[0:1, :], alog[1:2, :]))
    alogt = alogt_ref[...]
    nega_col = -jnp.exp(jnp.where(fwd, alogt[:, 0:1], alogt[:, 1:2]))
    a_col = dt_col * nega_row
    a_row = dt_row * nega_col

    li = lax.broadcasted_iota(jnp.int32, (q, q), 0)
    si = lax.broadcasted_iota(jnp.int32, (q, q), 1)
    sgn = jnp.where(fwd, 1, -1)
    tri = (si - li) * sgn <= 0
    trit = (li - si) * sgn <= 0
    tri_b = jnp.where(tri, 1.0, 0.0).astype(BF16)
    trit_b = jnp.where(trit, 1.0, 0.0).astype(BF16)
    e_col = _dot_exact_lhs01(tri_b, a_col)
    e_row = _dot_exact_rhs01(a_row, trit_b)
    e_tot = jnp.where(fwd, e_col[q - 1:q, :], e_col[0:1, :])

    expand = exp_ref[...]
    w_dt = _dot_exact_rhs01(dt_col, expand)
    w_dec = _dot_exact_rhs01(jnp.exp(e_tot - e_col), expand)
    w_off = _dot_exact_rhs01(jnp.exp(e_col), expand)
    w_tot = jnp.where(fwd, w_off[q - 1:q, :], w_off[0:1, :])

    xdt = x * w_dt
    xdt_b = xdt.astype(BF16)
    xdec_b = (xdt * w_dec).astype(BF16)
    lane = lax.broadcasted_iota(jnp.int32, (1, LANES), 1)
    lo = lane < SSM_HEAD_DIM

    y_parts = []
    for g in range(SSM_GROUPS):
        bm = xbc[:, SSM_INNER + g * n:SSM_INNER + (g + 1) * n].astype(BF16)
        cm = xbc[:, SSM_INNER + (SSM_GROUPS + g) * n:SSM_INNER + (SSM_GROUPS + g + 1) * n].astype(BF16)
        cb = _dot_nt(cm, bm)
        hg = h_ref[g]
        y_off = _dot(cm, hg.astype(BF16))
        gs = slice(g * gw, (g + 1) * gw)
        h_ref[g] = hg * w_tot[:, gs] + _dot_tn(bm, xdec_b[:, gs])
        for pr in range(hpg // 2):
            h0 = g * hpg + 2 * pr
            cs = slice(h0 * SSM_HEAD_DIM, (h0 + 2) * SSM_HEAD_DIM)
            xp = xdt_b[:, cs]
            acc = None
            for k in range(2):
                h = h0 + k
                lm = jnp.where(tri, jnp.exp(e_col[:, h:h + 1] - e_row[h:h + 1, :]), 0.0)
                mh = (cb * lm).astype(BF16)
                sel = lo if k == 0 else jnp.logical_not(lo)
                t = _dot(mh, jnp.where(sel, xp, jnp.zeros_like(xp)))
                acc = t if acc is None else acc + t
            ys = slice(2 * pr * SSM_HEAD_DIM, (2 * pr + 2) * SSM_HEAD_DIM)
            y_parts.append(acc + y_off[:, ys] * w_off[:, cs])
    y = jnp.concatenate(y_parts, axis=-1)
    y = y + jnp.where(fwd, dskip_ref[...], 0.0) * x

    @pl.when(is_ctx)
    def _():
        yc_ref[0, 0] = y

    @pl.when(jnp.logical_not(is_ctx))
    def _():
        yl_ref[0, 0] = y


def _ssd_scan(xbc_l, xbc_c, dt_l, dt_c, a_log, d_skip):
    b, l, _ = xbc_l.shape
    cl = xbc_c.shape[1]
    q = SSM_CHUNK
    ncl, ncc = l // q, cl // q
    nc = ncl + ncc
    hh = SSM_HEADS
    dtt_l = jnp.swapaxes(dt_l[..., :2 * hh], 1, 2)
    dtt_c = jnp.swapaxes(dt_c[..., :2 * hh], 1, 2)
    expand = jnp.asarray(np.kron(np.eye(hh), np.ones((1, SSM_HEAD_DIM))), BF16)
    dskip = jnp.repeat(d_skip.astype(F32), SSM_HEAD_DIM).reshape(1, SSM_INNER)

    def lat_chunk(d, c):
        cc = jnp.maximum(c - ncc, 0)
        return jnp.where(d == 0, cc, ncl - 1 - cc)

    def ctx_chunk(d, c):
        cc = jnp.minimum(c, ncc - 1)
        return jnp.where(d == 0, cc, ncc - 1 - cc)

    in_specs = [pl.BlockSpec((1, q, SSM_XBC), lambda i, d, c: (i, lat_chunk(d, c), 0)),
                pl.BlockSpec((1, q, SSM_XBC), lambda i, d, c: (i, ctx_chunk(d, c), 0)),
                pl.BlockSpec((1, q, LANES), lambda i, d, c: (i, lat_chunk(d, c), 0)),
                pl.BlockSpec((1, q, LANES), lambda i, d, c: (i, ctx_chunk(d, c), 0)),
                pl.BlockSpec((1, 2 * hh, q), lambda i, d, c: (i, 0, lat_chunk(d, c))),
                pl.BlockSpec((1, 2 * hh, q), lambda i, d, c: (i, 0, ctx_chunk(d, c))),
                pl.BlockSpec((2, hh), lambda i, d, c: (0, 0)),
                pl.BlockSpec((hh, 2), lambda i, d, c: (0, 0)),
                pl.BlockSpec((1, SSM_INNER), lambda i, d, c: (0, 0)),
                pl.BlockSpec((hh, SSM_INNER), lambda i, d, c: (0, 0))]
    out_specs = [pl.BlockSpec((1, 1, q, SSM_INNER), lambda i, d, c: (i, d, lat_chunk(d, c), 0)),
                 pl.BlockSpec((1, 1, q, SSM_INNER), lambda i, d, c: (i, d, ctx_chunk(d, c), 0))]
    out_shape = [jax.ShapeDtypeStruct((b, 2, l, SSM_INNER), F32),
                 jax.ShapeDtypeStruct((b, 2, cl, SSM_INNER), F32)]
    gw = (SSM_HEADS // SSM_GROUPS) * SSM_HEAD_DIM
    return pl.pallas_call(
        functools.partial(_ssd_kernel, ncc=ncc),
        grid=(b, 2, nc),
        in_specs=in_specs,
        out_specs=out_specs,
        out_shape=out_shape,
        scratch_shapes=[pltpu.VMEM((SSM_GROUPS, SSM_STATE, gw), F32)],
        compiler_params=_cparams(("parallel", "arbitrary", "arbitrary")),
        name="ssd_scan",
    )(xbc_l, xbc_c, dt_l, dt_c, dtt_l, dtt_c, a_log.astype(F32), a_log.astype(F32).T, dskip, expand)


def _merge_kernel(x_ref, g1_ref, gate_ref, att_ref, ys_ref, z_ref, cv_ref, ng_ref,
                  wa_ref, ws_ref, wc_ref, bc_ref, wo_ref, o_ref):
    d = D_MODEL
    ys = (ys_ref[0, 0] + ys_ref[0, 1]) * _silu(z_ref[0].astype(F32))
    gsz = SSM_INNER // SSM_GROUPS
    parts = []
    for g in range(SSM_GROUPS):
        yg = ys[:, g * gsz:(g + 1) * gsz]
        parts.append(yg * lax.rsqrt(jnp.mean(yg * yg, axis=-1, keepdims=True) + EPS))
    yn = (jnp.concatenate(parts, axis=-1) * ng_ref[...]).astype(BF16)
    p_s = _dot(yn, ws_ref[...])
    p_a = _dot(att_ref[0], wa_ref[...])
    p_c = _dot(cv_ref[0], wc_ref[...]) + bc_ref[...]
    gl = gate_ref[0].astype(F32)
    m = (_sigmoid(gl[:, 0:d]) * p_a + _sigmoid(gl[:, d:2 * d]) * p_s
         + _sigmoid(gl[:, 2 * d:3 * d]) * p_c).astype(BF16)
    o_ref[0] = x_ref[0] + g1_ref[0] * _dot(m, wo_ref[...])


def _merge(x, g1, packed, y_att, y_ssm2, y_cv, norm_g, wa, ws, wc, bc, wo):
    b, l, d = x.shape
    tm = 256
    bm = g1.shape[0]
    g_map = (lambda i, t: (i, 0, 0)) if bm > 1 else (lambda i, t: (0, 0, 0))
    const = lambda i, t: (0, 0)
    one = pl.Buffered(1)
    in_specs = [pl.BlockSpec((1, tm, d), lambda i, t: (i, t, 0)),
                pl.BlockSpec((1, 1, d), g_map),
                pl.BlockSpec((1, tm, 3 * d), lambda i, t: (i, t, OFF_GATE // (3 * d))),
                pl.BlockSpec((1, tm, NA_WIDTH), lambda i, t: (i, t, 0)),
                pl.BlockSpec((1, 2, tm, SSM_INNER), lambda i, t: (i, 0, t, 0)),
                pl.BlockSpec((1, tm, SSM_INNER), lambda i, t: (i, t, OFF_Z // SSM_INNER)),
                pl.BlockSpec((1, tm, CONV_CH), lambda i, t: (i, t, 0)),
                pl.BlockSpec((1, SSM_INNER), const),
                pl.BlockSpec((NA_WIDTH, d), const, pipeline_mode=one),
                pl.BlockSpec((SSM_INNER, d), const, pipeline_mode=one),
                pl.BlockSpec((CONV_CH, d), const, pipeline_mode=one),
                pl.BlockSpec((1, d), const),
                pl.BlockSpec((d, d), const, pipeline_mode=one)]
    return pl.pallas_call(
        _merge_kernel,
        grid=(b, l // tm),
        in_specs=in_specs,
        out_specs=pl.BlockSpec((1, tm, d), lambda i, t: (i, t, 0)),
        out_shape=jax.ShapeDtypeStruct((b, l, d), F32),
        compiler_params=_cparams(("parallel", "parallel")),
        name="merge",
    )(x, g1.reshape(bm, 1, d), packed, y_att, y_ssm2, packed, y_cv, norm_g.reshape(1, -1),
      wa, ws, wc, bc.reshape(1, d), wo)


PEER_TQ = 256
NEG_INF = float("-inf")


def _sort_network(n):
    pairs = []

    def merge(lo, hi, r):
        step = r * 2
        if step < hi - lo:
            merge(lo, hi, step)
            merge(lo + r, hi, step)
            pairs.extend((i, i + r) for i in range(lo + r, hi - r, step))
        else:
            pairs.append((lo, lo + r))

    def sort(lo, hi):
        if hi - lo >= 1:
            mid = lo + (hi - lo) // 2
            sort(lo, mid)
            sort(mid + 1, hi)
            merge(lo, hi, 1)

    sort(0, n - 1)
    return pairs


def _exchange(v, i, j):
    a, b = v[i], v[j]
    v[i], v[j] = jnp.maximum(a, b), jnp.minimum(a, b)


def _top_k_sorted(v):
    k = len(v)
    v = list(v)
    for i, j in _sort_network(k):
        _exchange(v, i, j)
    for shift in (4, 2, 1):
        other = [pltpu.roll(x, shift, axis=0) for x in v]
        v = [jnp.maximum(v[i], other[k - 1 - i]) for i in range(k)]
        dist = k // 2
        while dist >= 1:
            for i in range(k):
                if i & dist == 0:
                    _exchange(v, i, i + dist)
            dist //= 2
    return v


def _peer_select_kernel(q_ref, keys_ref, n1_ref, e1_ref, r2_ref, e2_ref, s_ref, top_ref):
    tq = PEER_TQ
    k = PEER_TOPK
    sub = lax.broadcasted_iota(jnp.int32, (8, tq), 0)

    def per_set(i, carry):
        c0 = pl.multiple_of(i * PEER_KEY_DIM, PEER_KEY_DIM)
        s = _dot_nt(keys_ref[i], q_ref[:, pl.ds(c0, PEER_KEY_DIM)])
        s_ref[i] = s
        top = _top_k_sorted([s[8 * r:8 * r + 8, :] for r in range(PEER_NKEYS // 8)])
        for r in range(k):
            top_ref[i, r] = top[r]
        return carry

    lax.fori_loop(0, 2 * PEER_HEADS, per_set, 0)

    def spread(rows):
        out = rows[7]
        for s in range(6, -1, -1):
            out = jnp.where(sub == s, rows[s], out)
        return out

    def per_head(h, carry):
        t1 = [top_ref[2 * h, r] for r in range(k)]
        t2 = [top_ref[2 * h + 1, r] for r in range(k)]
        p2a, p2b, p1b = spread(t2[:8]), spread(t2[8:]), spread(t1[8:])
        cand = [t1[0] + p2a, t1[0] + p2b] + [t1[i] + p2a for i in range(1, 8)] + [p1b + t2[0]]
        cand += [jnp.full((8, tq), NEG_INF, F32)] * (k - len(cand))
        best = _top_k_sorted(cand)
        z = jnp.ones((8, tq), F32)
        for r in range(1, k):
            z = z + jnp.exp(best[r] - best[0])
        thr, rz = best[k - 1], 1.0 / z
        one, zero = jnp.ones((8, tq), F32), jnp.zeros((8, tq), F32)
        for g2 in range(PEER_NKEYS // 16):
            ranks, e2s = [], []
            for g in (2 * g2, 2 * g2 + 1):
                s1 = s_ref[2 * h, 8 * g:8 * g + 8, :]
                s2 = s_ref[2 * h + 1, 8 * g:8 * g + 8, :]
                n, r = zero, zero
                for jj in range(k):
                    n = n + jnp.where(s1 + t2[jj] >= thr, one, zero)
                    r = r + jnp.where(t2[jj] > s2, one, zero)
                n1_ref[h, g] = n
                e1_ref[h, g] = jnp.exp(s1 - t1[0]) * rz
                ranks.append(r)
                e2s.append(jnp.exp(s2 - t2[0]))
            r2_ref[h, g2] = jnp.concatenate(ranks, axis=0).astype(BF16)
            e2_ref[h, g2] = jnp.concatenate(e2s, axis=0).astype(BF16)
        return carry

    lax.fori_loop(0, PEER_HEADS, per_head, 0)


def _peer_select(q, keys):
    t = q.shape[0]
    tq = PEER_TQ
    nset = 2 * PEER_HEADS
    hh, g8, g16 = PEER_HEADS, PEER_NKEYS // 8, PEER_NKEYS // 16
    return pl.pallas_call(
        _peer_select_kernel,
        grid=(t // tq,),
        in_specs=[pl.BlockSpec((tq, q.shape[1]), lambda i: (i, 0)),
                  pl.BlockSpec((nset, PEER_NKEYS, PEER_KEY_DIM), lambda i: (0, 0, 0))],
        out_specs=[pl.BlockSpec((hh, g8, 8, tq), lambda i: (0, 0, 0, i)),
                   pl.BlockSpec((hh, g8, 8, tq), lambda i: (0, 0, 0, i)),
                   pl.BlockSpec((hh, g16, 16, tq), lambda i: (0, 0, 0, i)),
                   pl.BlockSpec((hh, g16, 16, tq), lambda i: (0, 0, 0, i))],
        out_shape=[jax.ShapeDtypeStruct((hh, g8, 8, t), F32),
                   jax.ShapeDtypeStruct((hh, g8, 8, t), F32),
                   jax.ShapeDtypeStruct((hh, g16, 16, t), BF16),
                   jax.ShapeDtypeStruct((hh, g16, 16, t), BF16)],
        scratch_shapes=[pltpu.VMEM((nset, PEER_NKEYS, tq), F32),
                        pltpu.VMEM((nset, PEER_TOPK, 8, tq), F32)],
        compiler_params=_cparams(("parallel",)),
        name="peer_select",
    )(q, keys)


PEER_TM = 512
PEER_TE = 1024
PEER_KEY_BLOCK = 32
PEER_ROW_BLOCK = 4


def _gelu(x):
    return 0.5 * x * (1.0 + lax.erf(x * np.float32(math.sqrt(0.5))))


def _peer_dense_kernel(h_ref, u_ref, vt_ref, n1_ref, e1_ref, r2_ref, e2_ref, x_ref, g2_ref, o_ref,
                       acc_ref, at_ref, wa_ref, bcn_ref, bce_ref):
    j = pl.program_id(1)
    nsteps = pl.num_programs(1)
    tm = PEER_TM
    te = PEER_TE
    nk = PEER_NKEYS

    @pl.when(j == 0)
    def _():
        acc_ref[...] = jnp.zeros_like(acc_ref)

    na = te // nk
    kb = PEER_KEY_BLOCK
    kg = kb // 16

    at_ref[...] = _dot_nt(u_ref[...], h_ref[...])

    for h, al in itertools.product(range(PEER_HEADS), range(na)):
        bcn_ref[h, al] = jnp.broadcast_to(n1_ref[h, j, al:al + 1, :], (16, tm)).astype(BF16)
        bce_ref[h, al] = jnp.broadcast_to(e1_ref[h, j, al:al + 1, :], (16, tm)).astype(BF16)

    for lc, kq, ab in itertools.product(range(tm // LANES), range(nk // kb), range(na // PEER_ROW_BLOCK)):
        ls = slice(lc * LANES, (lc + 1) * LANES)
        gs = slice(kq * kg, (kq + 1) * kg)
        als = range(ab * PEER_ROW_BLOCK, (ab + 1) * PEER_ROW_BLOCK)
        w = {al: jnp.zeros((kg, 16, LANES), BF16) for al in als}
        for h in range(PEER_HEADS):
            r2 = r2_ref[h, gs, :, ls]
            e2 = e2_ref[h, gs, :, ls]
            for al in als:
                n1 = bcn_ref[h, al, :, ls]
                e1 = bce_ref[h, al, :, ls]
                w[al] = w[al] + jnp.where(r2 < n1, e2, jnp.zeros_like(e2)) * e1
        for al in als:
            rs = slice(al * nk + kq * kb, al * nk + (kq + 1) * kb)
            wa_ref[rs, ls] = (w[al].reshape(kb, LANES).astype(F32) * _gelu(at_ref[rs, ls])).astype(BF16)

    acc_ref[...] += _dot(vt_ref[...], wa_ref[...])

    @pl.when(j == nsteps - 1)
    def _():
        o_ref[...] = x_ref[...] + g2_ref[0] * acc_ref[...].T


def _peer_dense(hh, u, vt, sel, x, g2, tiles_per_row):
    t, d = hh.shape
    ne = u.shape[0]
    tm, te = PEER_TM, PEER_TE
    one = pl.Buffered(1)
    na = te // PEER_NKEYS
    hd, g8, g16 = PEER_HEADS, PEER_NKEYS // 8, PEER_NKEYS // 16
    sel_map = lambda i, j: (0, 0, 0, i)
    return pl.pallas_call(
        _peer_dense_kernel,
        grid=(t // tm, ne // te),
        in_specs=[pl.BlockSpec((tm, d), lambda i, j: (i, 0), pipeline_mode=one),
                  pl.BlockSpec((te, d), lambda i, j: (j, 0)),
                  pl.BlockSpec((d, te), lambda i, j: (0, j)),
                  pl.BlockSpec((hd, g8, 8, tm), sel_map, pipeline_mode=one),
                  pl.BlockSpec((hd, g8, 8, tm), sel_map, pipeline_mode=one),
                  pl.BlockSpec((hd, g16, 16, tm), sel_map, pipeline_mode=one),
                  pl.BlockSpec((hd, g16, 16, tm), sel_map, pipeline_mode=one),
                  pl.BlockSpec((tm, d), lambda i, j: (i, 0), pipeline_mode=one),
                  pl.BlockSpec((1, 1, d), lambda i, j: (i // tiles_per_row, 0, 0))],
        out_specs=pl.BlockSpec((tm, d), lambda i, j: (i, 0)),
        out_shape=jax.ShapeDtypeStruct((t, d), F32),
        scratch_shapes=[pltpu.VMEM((d, tm), F32),
                        pltpu.VMEM((te, tm), F32),
                        pltpu.VMEM((te, tm), BF16),
                        pltpu.VMEM((hd, na, 16, tm), BF16),
                        pltpu.VMEM((hd, na, 16, tm), BF16)],
        compiler_params=_cparams(("parallel", "arbitrary")),
        name="peer_dense",
    )(hh, u, vt, *sel, x, g2)


def _peer_block(x, g, shift, scale, gate2, wq, keys, u, vt):
    b, l, d = x.shape
    t = b * l
    x_rows = x if shift.shape[0] > 1 else x.reshape(1, t, d)
    q, hh = _norm_mod_matmul(x_rows, g, shift, scale, wq, BF16, 512, emit_h=True)
    sel = _peer_select(q.reshape(t, -1), keys)
    bm = gate2.shape[0]
    tiles_per_row = (l // PEER_TM) if bm > 1 else (t // PEER_TM)
    out = _peer_dense(hh.reshape(t, d), u, vt, sel, x.reshape(t, d), gate2.reshape(bm, 1, d), tiles_per_row)
    return out.reshape(b, l, d)


def _rmsnorm_kernel(x_ref, g_ref, o_ref):
    x = x_ref[...]
    o_ref[...] = x * lax.rsqrt(jnp.mean(x * x, axis=-1, keepdims=True) + EPS) * g_ref[...]


def _rmsnorm(x, g):
    t, d = x.shape
    tm = 512
    return pl.pallas_call(
        _rmsnorm_kernel,
        grid=(t // tm,),
        in_specs=[pl.BlockSpec((tm, d), lambda i: (i, 0)), pl.BlockSpec((1, d), lambda i: (0, 0))],
        out_specs=pl.BlockSpec((tm, d), lambda i: (i, 0)),
        out_shape=jax.ShapeDtypeStruct((t, d), F32),
        compiler_params=_cparams(("parallel",)),
        name="final_rmsnorm",
    )(x, g.reshape(1, d))


def _pack_w_in(w_in):
    q, k, v, z, xbc, dt, glu, gate = jnp.split(
        w_in, np.cumsum([NA_WIDTH, NA_WIDTH, NA_WIDTH, SSM_INNER, SSM_XBC, 2 * SSM_HEADS, 2 * CONV_CH]).tolist(),
        axis=-1)
    packed = jnp.concatenate([gate, glu, q, k, v, z, xbc], axis=-1).astype(BF16)
    dt_w = jnp.pad(dt, ((0, 0), (0, LANES - 2 * SSM_HEADS))).astype(BF16)
    return packed, dt_w


def _trunk_layer(xl, xc, c_rows, need_ctx, w_mod, b_mod, norm1_g, norm2_g, w_in, na_rpb, na_wo,
                 ssm_conv_w, ssm_conv_b, ssm_dt_bias, ssm_a_log, ssm_d, ssm_norm_g, ssm_wo,
                 cv_dw_w, cv_dw_b, cv_ln_g, cv_ln_b, cv_wo, cv_bo, w_out, peer_wq, peer_keys, peer_u, peer_v):
    b, s, d = xl.shape
    mod = _modulation(c_rows, w_mod, b_mod)
    sh1, sc1, g1, sh2, sc2, g2 = [mod[:b, i * d:(i + 1) * d] for i in range(6)]
    csh1, csc1, cg1, csh2, csc2, cg2 = [mod[b:b + 1, i * d:(i + 1) * d] for i in range(6)]

    w_pack, w_dt = _pack_w_in(w_in)
    dt_bias = jnp.pad(ssm_dt_bias.astype(F32).reshape(-1), (0, LANES - 2 * SSM_HEADS))

    cb, cl, _ = xc.shape
    xc_flat = xc.reshape(1, cb * cl, d)
    p_l = _norm_mod_matmul(xl, norm1_g, sh1, sc1, w_pack, BF16, 512)
    p_c = _norm_mod_matmul(xc_flat, norm1_g, csh1, csc1, w_pack, BF16, 512).reshape(cb, cl, -1)
    dt_l = _norm_mod_matmul(xl, norm1_g, sh1, sc1, w_dt, F32, LANES, softplus_bias=dt_bias)
    dt_c = _norm_mod_matmul(xc_flat, norm1_g, csh1, csc1, w_dt, F32, LANES,
                            softplus_bias=dt_bias).reshape(cb, cl, -1)

    bias_tab = _na_bias_table(na_rpb, s // GRID_W)
    y_att, y_att_c = _neighbourhood_attention(p_l, p_c, bias_tab, need_ctx)

    xbc_l = _dwconv(p_l, OFF_XBC, None, SSM_XBC, ssm_conv_w, ssm_conv_b, None, None, F32)
    xbc_c = _dwconv(p_c, OFF_XBC, None, SSM_XBC, ssm_conv_w, ssm_conv_b, None, None, F32)
    y_ssm, y_ssm_c = _ssd_scan(xbc_l, xbc_c, dt_l, dt_c, ssm_a_log, ssm_d)

    wa, ws, wc, wo = (na_wo.astype(BF16), ssm_wo.astype(BF16), cv_wo.astype(BF16), w_out.astype(BF16))
    y_cv = _dwconv(p_l, OFF_GLU_A, OFF_GLU_G, CONV_CH, cv_dw_w, cv_dw_b, cv_ln_g, cv_ln_b, BF16)
    xl = _merge(xl, g1, p_l, y_att, y_ssm, y_cv, ssm_norm_g, wa, ws, wc, cv_bo, wo)
    if need_ctx:
        y_cv_c = _dwconv(p_c, OFF_GLU_A, OFF_GLU_G, CONV_CH, cv_dw_w, cv_dw_b, cv_ln_g, cv_ln_b, BF16)
        xc = _merge(xc, cg1, p_c, y_att_c, y_ssm_c, y_cv_c, ssm_norm_g, wa, ws, wc, cv_bo, wo)

    wq = peer_wq.astype(BF16)
    keys = peer_keys.astype(BF16).reshape(2 * PEER_HEADS, PEER_NKEYS, PEER_KEY_DIM)
    u = peer_u.astype(BF16)
    vt = peer_v.astype(BF16).T
    xl = _peer_block(xl, norm2_g, sh2, sc2, g2, wq, keys, u, vt)
    if need_ctx:
        xc = _peer_block(xc, norm2_g, csh2, csc2, cg2, wq, keys, u, vt)
    return xl, xc


def kernel(x, c, ctx, c_ctx, w_mod, b_mod, norm1_g, norm2_g, w_in, na_rpb, na_wo, ssm_conv_w, ssm_conv_b,
           ssm_dt_bias, ssm_A_log, ssm_D, ssm_norm_g, ssm_wo, cv_dw_w, cv_dw_b, cv_ln_g, cv_ln_b, cv_wo,
           cv_bo, w_out, peer_wq, peer_keys, peer_u, peer_v, final_norm_g):
    b, s, d = x.shape
    depth = w_mod.shape[0]
    rows = -(-(b + 1) // 8) * 8
    c_rows = jnp.zeros((rows, d), F32).at[:b].set(c).at[b].set(c_ctx)
    xl, xc = x, ctx
    for l in range(depth):
        xl, xc = _trunk_layer(xl, xc, c_rows, l < depth - 1, w_mod[l], b_mod[l], norm1_g[l], norm2_g[l],
                              w_in[l], na_rpb[l], na_wo[l], ssm_conv_w[l], ssm_conv_b[l], ssm_dt_bias[l],
                              ssm_A_log[l], ssm_D[l], ssm_norm_g[l], ssm_wo[l], cv_dw_w[l], cv_dw_b[l],
                              cv_ln_g[l], cv_ln_b[l], cv_wo[l], cv_bo[l], w_out[l],
                              peer_wq[l], peer_keys[l], peer_u[l], peer_v[l])
    return _rmsnorm(xl.reshape(b * s, d), final_norm_g).reshape(b, s, d)
```

```python
import functools
import itertools
import math

import numpy as np
import jax
import jax.numpy as jnp
from jax import lax
from jax.experimental import pallas as pl
from jax.experimental.pallas import tpu as pltpu

F32 = jnp.float32
BF16 = jnp.bfloat16
EPS = 1e-6

D_MODEL = 2048
GRID_W = 64
NA_HEADS = 16
NA_HEAD_DIM = 64
NA_WIDTH = NA_HEADS * NA_HEAD_DIM
WIN_R = 8
WIN_C = 16
SSM_HEADS = 16
SSM_HEAD_DIM = 64
SSM_INNER = SSM_HEADS * SSM_HEAD_DIM
SSM_GROUPS = 2
SSM_STATE = 128
SSM_XBC = SSM_INNER + 2 * SSM_GROUPS * SSM_STATE
SSM_CONV = 5
SSM_CHUNK = 128
CONV_CH = 1024
CONV_K = 31
PEER_HEADS = 8
PEER_NKEYS = 128
PEER_KEY_DIM = 128
PEER_TOPK = 16
PEER_EXPERTS = PEER_NKEYS * PEER_NKEYS

OFF_GATE = 0
OFF_GLU_A = 6144
OFF_GLU_G = 7168
OFF_Q = 8192
OFF_K = 9216
OFF_V = 10240
OFF_Z = 11264
OFF_XBC = 12288
PACK_COLS = 13824

LANES = 128
VMEM_LIMIT = 56 * 1024 * 1024


def _cparams(sem):
    return pltpu.CompilerParams(dimension_semantics=sem, vmem_limit_bytes=VMEM_LIMIT)


def _dot(a, b):
    return jnp.dot(a, b, preferred_element_type=F32)


def _dot_nt(a, b):
    return lax.dot_general(a, b, (((1,), (1,)), ((), ())), preferred_element_type=F32)


def _dot_tn(a, b):
    return lax.dot_general(a, b, (((0,), (0,)), ((), ())), preferred_element_type=F32)


def _split3(a):
    hi = a.astype(BF16)
    r1 = a - hi.astype(F32)
    mid = r1.astype(BF16)
    lo = (r1 - mid.astype(F32)).astype(BF16)
    return hi, mid, lo


def _dot_exact_rhs01(a, m01):
    hi, mid, lo = _split3(a)
    return _dot(hi, m01) + _dot(mid, m01) + _dot(lo, m01)


def _dot_exact_lhs01(m01, a):
    hi, mid, lo = _split3(a)
    return _dot(m01, hi) + _dot(m01, mid) + _dot(m01, lo)


def _sigmoid(x):
    return 1.0 / (1.0 + jnp.exp(-x))


def _silu(x):
    return x * _sigmoid(x)


def _mod_kernel(c_ref, w_ref, b_ref, o_ref):
    a = _silu(c_ref[...]).astype(BF16)
    o_ref[...] = _dot(a, w_ref[...].astype(BF16)) + b_ref[...]


def _modulation(cs, w_mod, b_mod):
    r, d = cs.shape
    n = w_mod.shape[1]
    tn = 1024
    return pl.pallas_call(
        _mod_kernel,
        grid=(n // tn,),
        in_specs=[pl.BlockSpec((r, d), lambda j: (0, 0)),
                  pl.BlockSpec((d, tn), lambda j: (0, j)),
                  pl.BlockSpec((1, tn), lambda j: (0, j))],
        out_specs=pl.BlockSpec((r, tn), lambda j: (0, j)),
        out_shape=jax.ShapeDtypeStruct((r, n), F32),
        compiler_params=_cparams(("arbitrary",)),
        name="modulation",
    )(cs, w_mod, b_mod.reshape(1, n))


def _nmm_kernel(x_ref, g_ref, sh_ref, sc_ref, w_ref, *rest, softplus_bias, emit_h):
    if softplus_bias:
        bias_ref, rest = rest[0], rest[1:]
    if emit_h:
        o_ref, ho_ref, h_ref = rest
    else:
        o_ref, h_ref = rest

    @pl.when(pl.program_id(2) == 0)
    def _():
        x = x_ref[0]
        ms = jnp.mean(x * x, axis=-1, keepdims=True)
        y = x * lax.rsqrt(ms + EPS) * g_ref[...]
        h = (y * (1.0 + sc_ref[0]) + sh_ref[0]).astype(BF16)
        h_ref[...] = h
        if emit_h:
            ho_ref[0] = h

    acc = _dot(h_ref[...], w_ref[0])
    if softplus_bias:
        t = acc + bias_ref[...]
        acc = jnp.maximum(t, 0.0) + jnp.log1p(jnp.exp(-jnp.abs(t)))
    o_ref[0] = acc.astype(o_ref.dtype)


def _column_blocks(w, tn):
    d, n = w.shape
    return w.reshape(d, n // tn, tn).transpose(1, 0, 2)


def _norm_mod_matmul(x, g, shift, scale, w, out_dtype, tn, softplus_bias=None, emit_h=False):
    b, l, d = x.shape
    n = w.shape[1]
    w = _column_blocks(w, tn)
    tm = min(l, 1024)
    bm = shift.shape[0]
    mod_map = (lambda i, m, j: (i, 0, 0)) if bm > 1 else (lambda i, m, j: (0, 0, 0))
    in_specs = [pl.BlockSpec((1, tm, d), lambda i, m, j: (i, m, 0)),
                pl.BlockSpec((1, d), lambda i, m, j: (0, 0)),
                pl.BlockSpec((1, 1, d), mod_map),
                pl.BlockSpec((1, 1, d), mod_map),
                pl.BlockSpec((1, d, tn), lambda i, m, j: (j, 0, 0))]
    args = [x, g.reshape(1, d), shift.reshape(bm, 1, d), scale.reshape(bm, 1, d), w]
    if softplus_bias is not None:
        in_specs.append(pl.BlockSpec((1, tn), lambda i, m, j: (0, j)))
        args.append(softplus_bias.reshape(1, n))
    out_shape = [jax.ShapeDtypeStruct((b, l, n), out_dtype)]
    out_specs = [pl.BlockSpec((1, tm, tn), lambda i, m, j: (i, m, j))]
    if emit_h:
        out_shape.append(jax.ShapeDtypeStruct((b, l, d), BF16))
        out_specs.append(pl.BlockSpec((1, tm, d), lambda i, m, j: (i, m, 0)))
    res = pl.pallas_call(
        functools.partial(_nmm_kernel, softplus_bias=softplus_bias is not None, emit_h=emit_h),
        grid=(b, l // tm, n // tn),
        in_specs=in_specs,
        out_specs=out_specs,
        out_shape=out_shape,
        scratch_shapes=[pltpu.VMEM((tm, d), BF16)],
        compiler_params=_cparams(("parallel", "parallel", "arbitrary")),
        name="norm_mod_matmul",
    )(*args)
    return res if emit_h else res[0]


def _na_bias_table(rpb, rows):
    wr = min(WIN_R, rows)
    j = np.arange(GRID_W)
    cstart = np.clip(j - WIN_C // 2, 0, GRID_W - WIN_C)
    kc = np.arange(GRID_W)
    mask = (kc[None, :] >= cstart[:, None]) & (kc[None, :] < cstart[:, None] + WIN_C)
    col_off = np.clip(kc[None, :] - j[:, None], -(WIN_C - 1), WIN_C - 1) + WIN_C - 1
    dd = np.arange(wr)[:, None]
    ww = np.arange(wr)[None, :]
    row_idx = ww - dd + WIN_R - 1
    t = rpb.astype(F32)[:, row_idx][:, :, :, col_off]
    t = jnp.where(mask[None, None, None], t, -1e30)
    t = t.transpose(0, 1, 3, 2, 4).reshape(NA_HEADS, wr, GRID_W, wr * GRID_W)
    return t.reshape(NA_HEADS // 2, 2, wr, GRID_W, wr * GRID_W)


NA_ROW_GROUP = 8


def _na_kernel(q_ref, k_ref, v_ref, kc_ref, vc_ref, bias_ref, *rest, rows, wr, with_ctx):
    if with_ctx:
        qc_ref, o_ref, oc_ref = rest
    else:
        (o_ref,) = rest
    lane = lax.broadcasted_iota(jnp.int32, (1, LANES), 1)
    lo = lane < NA_HEAD_DIM
    kc = kc_ref[0]
    vc = vc_ref[0]
    scale = NA_HEAD_DIM ** -0.5
    nwin = wr * GRID_W

    def attend(problems):
        chains = []
        for q, kw, vw, bias_of in problems:
            for h in range(2):
                sel = lo if h == 0 else jnp.logical_not(lo)
                qh = jnp.where(sel, q, jnp.zeros_like(q))
                s_c = _dot_nt(qh, kc) * scale
                s_w = None if kw is None else _dot_nt(qh, kw) * scale + bias_of(h)
                chains.append((s_c, s_w, vw))
        maxes = []
        for s_c, s_w, _ in chains:
            m = jnp.max(s_c, axis=-1, keepdims=True)
            if s_w is not None:
                m = jnp.maximum(m, jnp.max(s_w, axis=-1, keepdims=True))
            maxes.append(m)
        probs = []
        for (s_c, s_w, _), m in zip(chains, maxes):
            probs.append((jnp.exp(s_c - m), None if s_w is None else jnp.exp(s_w - m)))
        outs = []
        for (_, _, vw), (p_c, p_w) in zip(chains, probs):
            den = jnp.sum(p_c, axis=-1, keepdims=True)
            o = _dot(p_c.astype(BF16), vc)
            if p_w is not None:
                den = den + jnp.sum(p_w, axis=-1, keepdims=True)
                o = o + _dot(p_w.astype(BF16), vw)
            outs.append(o / den)
        return [jnp.where(lo, outs[2 * i], outs[2 * i + 1]) for i in range(len(problems))]

    def body(g, carry):
        problems, q0s = [], []
        for rr in range(NA_ROW_GROUP):
            r = g * NA_ROW_GROUP + rr
            rs = jnp.clip(r - wr // 2, 0, rows - wr)
            d = r - rs
            q0 = pl.multiple_of(r * GRID_W, GRID_W)
            k0 = pl.multiple_of(rs * GRID_W, GRID_W)
            problems.append((q_ref[0, pl.ds(q0, GRID_W), :], k_ref[0, pl.ds(k0, nwin), :],
                             v_ref[0, pl.ds(k0, nwin), :], lambda h, d=d: bias_ref[0, h, d]))
            q0s.append(q0)
        for q0, o in zip(q0s, attend(problems)):
            o_ref[0, pl.ds(q0, GRID_W), :] = o.astype(o_ref.dtype)
        return carry

    lax.fori_loop(0, rows // NA_ROW_GROUP, body, 0)
    if with_ctx:
        oc_ref[0] = attend([(qc_ref[0], None, None, None)])[0].astype(oc_ref.dtype)


def _neighbourhood_attention(pl_lat, pl_ctx, bias_tab, with_ctx):
    b, s, _ = pl_lat.shape
    cl = pl_ctx.shape[1]
    rows = s // GRID_W
    wr = min(WIN_R, rows)
    nhp = NA_HEADS // 2
    qb, kb, vb = OFF_Q // LANES, OFF_K // LANES, OFF_V // LANES
    in_specs = [pl.BlockSpec((1, s, LANES), lambda hp, i: (i, 0, qb + hp)),
                pl.BlockSpec((1, s, LANES), lambda hp, i: (i, 0, kb + hp)),
                pl.BlockSpec((1, s, LANES), lambda hp, i: (i, 0, vb + hp)),
                pl.BlockSpec((1, cl, LANES), lambda hp, i: (i, 0, kb + hp)),
                pl.BlockSpec((1, cl, LANES), lambda hp, i: (i, 0, vb + hp)),
                pl.BlockSpec((1, 2, wr, GRID_W, wr * GRID_W), lambda hp, i: (hp, 0, 0, 0, 0))]
    args = [pl_lat, pl_lat, pl_lat, pl_ctx, pl_ctx, bias_tab]
    out_shape = [jax.ShapeDtypeStruct((b, s, NA_WIDTH), BF16)]
    out_specs = [pl.BlockSpec((1, s, LANES), lambda hp, i: (i, 0, hp))]
    if with_ctx:
        in_specs.append(pl.BlockSpec((1, cl, LANES), lambda hp, i: (i, 0, qb + hp)))
        args.append(pl_ctx)
        out_shape.append(jax.ShapeDtypeStruct((b, cl, NA_WIDTH), BF16))
        out_specs.append(pl.BlockSpec((1, cl, LANES), lambda hp, i: (i, 0, hp)))
    res = pl.pallas_call(
        functools.partial(_na_kernel, rows=rows, wr=wr, with_ctx=with_ctx),
        grid=(nhp, b),
        in_specs=in_specs,
        out_specs=out_specs,
        out_shape=out_shape,
        compiler_params=_cparams(("parallel", "parallel")),
        name="neighbourhood_attention",
    )(*args)
    return (res[0], res[1]) if with_ctx else (res[0], None)


CONV_TL = 256
CONV_HALO = 16


def _dwconv_kernel(*refs, taps, ch, glu, tl):
    if glu:
        (a_p, a_c, a_n, g_p, g_c, g_n, w_ref, b_ref, lng_ref, lnb_ref, o_ref, u_ref, y_ref) = refs
    else:
        (a_p, a_c, a_n, w_ref, b_ref, o_ref, u_ref) = refs
    i = pl.program_id(1)
    n = pl.num_programs(1)
    pad = (taps - 1) // 2
    hl = CONV_HALO

    def pre(a, g):
        a = a.astype(F32)
        if glu:
            return a * _sigmoid(g.astype(F32))
        return a

    u_ref[hl:hl + tl, :] = pre(a_c[0], g_c[0] if glu else None)
    top = pre(a_p[0, tl - hl:tl, :], g_p[0, tl - hl:tl, :] if glu else None)
    u_ref[0:hl, :] = jnp.where(i > 0, top, 0.0)
    bot = pre(a_n[0, 0:hl, :], g_n[0, 0:hl, :] if glu else None)
    u_ref[hl + tl:hl + tl + hl, :] = jnp.where(i < n - 1, bot, 0.0)

    for cc in range(ch // LANES):
        cs = slice(cc * LANES, (cc + 1) * LANES)
        acc = jnp.zeros((tl, LANES), F32) + b_ref[:, cs]
        for k in range(taps):
            off = hl - pad + k
            acc = acc + w_ref[k:k + 1, cs] * u_ref[off:off + tl, cs]
        if glu:
            y_ref[:, cs] = acc
        else:
            o_ref[0, :, cs] = _silu(acc).astype(o_ref.dtype)

    if glu:
        y = y_ref[...]
        mu = jnp.mean(y, axis=-1, keepdims=True)
        yc = y - mu
        var = jnp.mean(yc * yc, axis=-1, keepdims=True)
        z = yc * lax.rsqrt(var + EPS) * lng_ref[...] + lnb_ref[...]
        o_ref[0] = _silu(z).astype(o_ref.dtype)


def _dwconv(src, off_a, off_g, ch, w, b, ln_g, ln_b, out_dtype):
    bsz, l, _ = src.shape
    taps = w.shape[0]
    glu = off_g is not None
    tl = CONV_TL
    nt = l // tl
    ca = off_a // ch

    def spec(cb, delta):
        def imap(i, t):
            return (i, jnp.clip(t + delta, 0, nt - 1), cb)
        return pl.BlockSpec((1, tl, ch), imap)

    in_specs = [spec(ca, -1), spec(ca, 0), spec(ca, 1)]
    args = [src, src, src]
    if glu:
        cg = off_g // ch
        in_specs += [spec(cg, -1), spec(cg, 0), spec(cg, 1)]
        args += [src, src, src]
    in_specs += [pl.BlockSpec((taps, ch), lambda i, t: (0, 0)), pl.BlockSpec((1, ch), lambda i, t: (0, 0))]
    args += [w, b.reshape(1, ch)]
    scratch = [pltpu.VMEM((tl + 2 * CONV_HALO, ch), F32)]
    if glu:
        in_specs += [pl.BlockSpec((1, ch), lambda i, t: (0, 0)), pl.BlockSpec((1, ch), lambda i, t: (0, 0))]
        args += [ln_g.reshape(1, ch), ln_b.reshape(1, ch)]
        scratch.append(pltpu.VMEM((tl, ch), F32))
    return pl.pallas_call(
        functools.partial(_dwconv_kernel, taps=taps, ch=ch, glu=glu, tl=tl),
        grid=(bsz, nt),
        in_specs=in_specs,
        out_specs=pl.BlockSpec((1, tl, ch), lambda i, t: (i, t, 0)),
        out_shape=jax.ShapeDtypeStruct((bsz, l, ch), out_dtype),
        scratch_shapes=scratch,
        compiler_params=_cparams(("parallel", "parallel")),
        name="dwconv_glu" if glu else "dwconv_ssm",
    )(*args)


def _ssd_kernel(xl_ref, xc_ref, dtl_ref, dtc_ref, dttl_ref, dttc_ref, alog_ref, alogt_ref, dskip_ref,
                exp_ref, yl_ref, yc_ref, h_ref, *, ncc):
    d = pl.program_id(1)
    c = pl.program_id(2)
    q = SSM_CHUNK
    hh = SSM_HEADS
    hpg = SSM_HEADS // SSM_GROUPS
    gw = hpg * SSM_HEAD_DIM
    n = SSM_STATE
    is_ctx = c < ncc
    fwd = d == 0

    @pl.when(c == 0)
    def _():
        h_ref[...] = jnp.zeros_like(h_ref)

    xbc = jnp.where(is_ctx, xc_ref[0], xl_ref[0]).astype(F32)
    x = xbc[:, :SSM_INNER]
    dt2 = jnp.where(is_ctx, dtc_ref[0], dtl_ref[0])
    dt_col = jnp.where(fwd, dt2[:, 0:hh], dt2[:, hh:2 * hh])
    dtt2 = jnp.where(is_ctx, dttc_ref[0], dttl_ref[0])
    dt_row = jnp.where(fwd, dtt2[0:hh, :], dtt2[hh:2 * hh, :])
    alog = alog_ref[...]
    nega_row = -jnp.exp(jnp.where(fwd, alog[0:1, :], alog[1:2, :]))
    alogt = alogt_ref[...]
    nega_col = -jnp.exp(jnp.where(fwd, alogt[:, 0:1], alogt[:, 1:2]))
    a_col = dt_col * nega_row
    a_row = dt_row * nega_col

    li = lax.broadcasted_iota(jnp.int32, (q, q), 0)
    si = lax.broadcasted_iota(jnp.int32, (q, q), 1)
    sgn = jnp.where(fwd, 1, -1)
    tri = (si - li) * sgn <= 0
    trit = (li - si) * sgn <= 0
    tri_b = jnp.where(tri, 1.0, 0.0).astype(BF16)
    trit_b = jnp.where(trit, 1.0, 0.0).astype(BF16)
    e_col = _dot_exact_lhs01(tri_b, a_col)
    e_row = _dot_exact_rhs01(a_row, trit_b)
    e_tot = jnp.where(fwd, e_col[q - 1:q, :], e_col[0:1, :])

    expand = exp_ref[...]
    w_dt = _dot_exact_rhs01(dt_col, expand)
    w_dec = _dot_exact_rhs01(jnp.exp(e_tot - e_col), expand)
    w_off = _dot_exact_rhs01(jnp.exp(e_col), expand)
    w_tot = jnp.where(fwd, w_off[q - 1:q, :], w_off[0:1, :])

    xdt = x * w_dt
    xdt_b = xdt.astype(BF16)
    xdec_b = (xdt * w_dec).astype(BF16)
    lane = lax.broadcasted_iota(jnp.int32, (1, LANES), 1)
    lo = lane < SSM_HEAD_DIM

    y_parts = []
    for g in range(SSM_GROUPS):
        bm = xbc[:, SSM_INNER + g * n:SSM_INNER + (g + 1) * n].astype(BF16)
        cm = xbc[:, SSM_INNER + (SSM_GROUPS + g) * n:SSM_INNER + (SSM_GROUPS + g + 1) * n].astype(BF16)
        cb = _dot_nt(cm, bm)
        hg = h_ref[g]
        y_off = _dot(cm, hg.astype(BF16))
        gs = slice(g * gw, (g + 1) * gw)
        h_ref[g] = hg * w_tot[:, gs] + _dot_tn(bm, xdec_b[:, gs])
        for pr in range(hpg // 2):
            h0 = g * hpg + 2 * pr
            cs = slice(h0 * SSM_HEAD_DIM, (h0 + 2) * SSM_HEAD_DIM)
            xp = xdt_b[:, cs]
            acc = None
            for k in range(2):
                h = h0 + k
                lm = jnp.where(tri, jnp.exp(e_col[:, h:h + 1] - e_row[h:h + 1, :]), 0.0)
                mh = (cb * lm).astype(BF16)
                sel = lo if k == 0 else jnp.logical_not(lo)
                t = _dot(mh, jnp.where(sel, xp, jnp.zeros_like(xp)))
                acc = t if acc is None else acc + t
            ys = slice(2 * pr * SSM_HEAD_DIM, (2 * pr + 2) * SSM_HEAD_DIM)
            y_parts.append(acc + y_off[:, ys] * w_off[:, cs])
    y = jnp.concatenate(y_parts, axis=-1)
    y = y + jnp.where(fwd, dskip_ref[...], 0.0) * x

    @pl.when(is_ctx)
    def _():
        yc_ref[0, 0] = y

    @pl.when(jnp.logical_not(is_ctx))
    def _():
        yl_ref[0, 0] = y


def _ssd_scan(xbc_l, xbc_c, dt_l, dt_c, a_log, d_skip):
    b, l, _ = xbc_l.shape
    cl = xbc_c.shape[1]
    q = SSM_CHUNK
    ncl, ncc = l // q, cl // q
    nc = ncl + ncc
    hh = SSM_HEADS
    dtt_l = jnp.swapaxes(dt_l[..., :2 * hh], 1, 2)
    dtt_c = jnp.swapaxes(dt_c[..., :2 * hh], 1, 2)
    expand = jnp.asarray(np.kron(np.eye(hh), np.ones((1, SSM_HEAD_DIM))), BF16)
    dskip = jnp.repeat(d_skip.astype(F32), SSM_HEAD_DIM).reshape(1, SSM_INNER)

    def lat_chunk(d, c):
        cc = jnp.maximum(c - ncc, 0)
        return jnp.where(d == 0, cc, ncl - 1 - cc)

    def ctx_chunk(d, c):
        cc = jnp.minimum(c, ncc - 1)
        return jnp.where(d == 0, cc, ncc - 1 - cc)

    in_specs = [pl.BlockSpec((1, q, SSM_XBC), lambda i, d, c: (i, lat_chunk(d, c), 0)),
                pl.BlockSpec((1, q, SSM_XBC), lambda i, d, c: (i, ctx_chunk(d, c), 0)),
                pl.BlockSpec((1, q, LANES), lambda i, d, c: (i, lat_chunk(d, c), 0)),
                pl.BlockSpec((1, q, LANES), lambda i, d, c: (i, ctx_chunk(d, c), 0)),
                pl.BlockSpec((1, 2 * hh, q), lambda i, d, c: (i, 0, lat_chunk(d, c))),
                pl.BlockSpec((1, 2 * hh, q), lambda i, d, c: (i, 0, ctx_chunk(d, c))),
                pl.BlockSpec((2, hh), lambda i, d, c: (0, 0)),
                pl.BlockSpec((hh, 2), lambda i, d, c: (0, 0)),
                pl.BlockSpec((1, SSM_INNER), lambda i, d, c: (0, 0)),
                pl.BlockSpec((hh, SSM_INNER), lambda i, d, c: (0, 0))]
    out_specs = [pl.BlockSpec((1, 1, q, SSM_INNER), lambda i, d, c: (i, d, lat_chunk(d, c), 0)),
                 pl.BlockSpec((1, 1, q, SSM_INNER), lambda i, d, c: (i, d, ctx_chunk(d, c), 0))]
    out_shape = [jax.ShapeDtypeStruct((b, 2, l, SSM_INNER), F32),
                 jax.ShapeDtypeStruct((b, 2, cl, SSM_INNER), F32)]
    gw = (SSM_HEADS // SSM_GROUPS) * SSM_HEAD_DIM
    return pl.pallas_call(
        functools.partial(_ssd_kernel, ncc=ncc),
        grid=(b, 2, nc),
        in_specs=in_specs,
        out_specs=out_specs,
        out_shape=out_shape,
        scratch_shapes=[pltpu.VMEM((SSM_GROUPS, SSM_STATE, gw), F32)],
        compiler_params=_cparams(("parallel", "arbitrary", "arbitrary")),
        name="ssd_scan",
    )(xbc_l, xbc_c, dt_l, dt_c, dtt_l, dtt_c, a_log.astype(F32), a_log.astype(F32).T, dskip, expand)


def _merge_kernel(x_ref, g1_ref, gate_ref, att_ref, ys_ref, z_ref, cv_ref, ng_ref,
                  wa_ref, ws_ref, wc_ref, bc_ref, wo_ref, o_ref):
    d = D_MODEL
    ys = (ys_ref[0, 0] + ys_ref[0, 1]) * _silu(z_ref[0].astype(F32))
    gsz = SSM_INNER // SSM_GROUPS
    parts = []
    for g in range(SSM_GROUPS):
        yg = ys[:, g * gsz:(g + 1) * gsz]
        parts.append(yg * lax.rsqrt(jnp.mean(yg * yg, axis=-1, keepdims=True) + EPS))
    yn = (jnp.concatenate(parts, axis=-1) * ng_ref[...]).astype(BF16)
    p_s = _dot(yn, ws_ref[...])
    p_a = _dot(att_ref[0], wa_ref[...])
    p_c = _dot(cv_ref[0], wc_ref[...]) + bc_ref[...]
    gl = gate_ref[0].astype(F32)
    m = (_sigmoid(gl[:, 0:d]) * p_a + _sigmoid(gl[:, d:2 * d]) * p_s
         + _sigmoid(gl[:, 2 * d:3 * d]) * p_c).astype(BF16)
    o_ref[0] = x_ref[0] + g1_ref[0] * _dot(m, wo_ref[...])


def _merge(x, g1, packed, y_att, y_ssm2, y_cv, norm_g, wa, ws, wc, bc, wo):
    b, l, d = x.shape
    tm = 256
    bm = g1.shape[0]
    g_map = (lambda i, t: (i, 0, 0)) if bm > 1 else (lambda i, t: (0, 0, 0))
    const = lambda i, t: (0, 0)
    one = pl.Buffered(1)
    in_specs = [pl.BlockSpec((1, tm, d), lambda i, t: (i, t, 0)),
                pl.BlockSpec((1, 1, d), g_map),
                pl.BlockSpec((1, tm, 3 * d), lambda i, t: (i, t, OFF_GATE // (3 * d))),
                pl.BlockSpec((1, tm, NA_WIDTH), lambda i, t: (i, t, 0)),
                pl.BlockSpec((1, 2, tm, SSM_INNER), lambda i, t: (i, 0, t, 0)),
                pl.BlockSpec((1, tm, SSM_INNER), lambda i, t: (i, t, OFF_Z // SSM_INNER)),
                pl.BlockSpec((1, tm, CONV_CH), lambda i, t: (i, t, 0)),
                pl.BlockSpec((1, SSM_INNER), const),
                pl.BlockSpec((NA_WIDTH, d), const, pipeline_mode=one),
                pl.BlockSpec((SSM_INNER, d), const, pipeline_mode=one),
                pl.BlockSpec((CONV_CH, d), const, pipeline_mode=one),
                pl.BlockSpec((1, d), const),
                pl.BlockSpec((d, d), const, pipeline_mode=one)]
    return pl.pallas_call(
        _merge_kernel,
        grid=(b, l // tm),
        in_specs=in_specs,
        out_specs=pl.BlockSpec((1, tm, d), lambda i, t: (i, t, 0)),
        out_shape=jax.ShapeDtypeStruct((b, l, d), F32),
        compiler_params=_cparams(("parallel", "parallel")),
        name="merge",
    )(x, g1.reshape(bm, 1, d), packed, y_att, y_ssm2, packed, y_cv, norm_g.reshape(1, -1),
      wa, ws, wc, bc.reshape(1, d), wo)


PEER_TQ = 256
NEG_INF = float("-inf")


def _sort_network(n):
    pairs = []

    def merge(lo, hi, r):
        step = r * 2
        if step < hi - lo:
            merge(lo, hi, step)
            merge(lo + r, hi, step)
            pairs.extend((i, i + r) for i in range(lo + r, hi - r, step))
        else:
            pairs.append((lo, lo + r))

    def sort(lo, hi):
        if hi - lo >= 1:
            mid = lo + (hi - lo) // 2
            sort(lo, mid)
            sort(mid + 1, hi)
            merge(lo, hi, 1)

    sort(0, n - 1)
    return pairs


def _exchange(v, i, j):
    a, b = v[i], v[j]
    v[i], v[j] = jnp.maximum(a, b), jnp.minimum(a, b)


def _top_k_sorted(v):
    k = len(v)
    v = list(v)
    for i, j in _sort_network(k):
        _exchange(v, i, j)
    for shift in (4, 2, 1):
        other = [pltpu.roll(x, shift, axis=0) for x in v]
        v = [jnp.maximum(v[i], other[k - 1 - i]) for i in range(k)]
        dist = k // 2
        while dist >= 1:
            for i in range(k):
                if i & dist == 0:
                    _exchange(v, i, i + dist)
            dist //= 2
    return v


def _peer_select_kernel(q_ref, keys_ref, n1_ref, e1_ref, r2_ref, e2_ref, s_ref, top_ref):
    tq = PEER_TQ
    k = PEER_TOPK
    sub = lax.broadcasted_iota(jnp.int32, (8, tq), 0)

    def per_set(i, carry):
        c0 = pl.multiple_of(i * PEER_KEY_DIM, PEER_KEY_DIM)
        s = _dot_nt(keys_ref[i], q_ref[:, pl.ds(c0, PEER_KEY_DIM)])
        s_ref[i] = s
        top = _top_k_sorted([s[8 * r:8 * r + 8, :] for r in range(PEER_NKEYS // 8)])
        for r in range(k):
            top_ref[i, r] = top[r]
        return carry

    lax.fori_loop(0, 2 * PEER_HEADS, per_set, 0)

    def spread(rows):
        out = rows[7]
        for s in range(6, -1, -1):
            out = jnp.where(sub == s, rows[s], out)
        return out

    def per_head(h, carry):
        t1 = [top_ref[2 * h, r] for r in range(k)]
        t2 = [top_ref[2 * h + 1, r] for r in range(k)]
        p2a, p2b, p1b = spread(t2[:8]), spread(t2[8:]), spread(t1[8:])
        cand = [t1[0] + p2a, t1[0] + p2b] + [t1[i] + p2a for i in range(1, 8)] + [p1b + t2[0]]
        cand += [jnp.full((8, tq), NEG_INF, F32)] * (k - len(cand))
        best = _top_k_sorted(cand)
        z = jnp.ones((8, tq), F32)
        for r in range(1, k):
            z = z + jnp.exp(best[r] - best[0])
        thr, rz = best[k - 1], 1.0 / z
        one, zero = jnp.ones((8, tq), F32), jnp.zeros((8, tq), F32)
        for g2 in range(PEER_NKEYS // 16):
            ranks, e2s = [], []
            for g in (2 * g2, 2 * g2 + 1):
                s1 = s_ref[2 * h, 8 * g:8 * g + 8, :]
                s2 = s_ref[2 * h + 1, 8 * g:8 * g + 8, :]
                n, r = zero, zero
                for jj in range(k):
                    n = n + jnp.where(s1 + t2[jj] >= thr, one, zero)
                    r = r + jnp.where(t2[jj] > s2, one, zero)
                n1_ref[h, g] = n
                e1_ref[h, g] = jnp.exp(s1 - t1[0]) * rz
                ranks.append(r)
                e2s.append(jnp.exp(s2 - t2[0]))
            r2_ref[h, g2] = jnp.concatenate(ranks, axis=0).astype(BF16)
            e2_ref[h, g2] = jnp.concatenate(e2s, axis=0).astype(BF16)
        return carry

    lax.fori_loop(0, PEER_HEADS, per_head, 0)


def _peer_select(q, keys):
    t = q.shape[0]
    tq = PEER_TQ
    nset = 2 * PEER_HEADS
    hh, g8, g16 = PEER_HEADS, PEER_NKEYS // 8, PEER_NKEYS // 16
    return pl.pallas_call(
        _peer_select_kernel,
        grid=(t // tq,),
        in_specs=[pl.BlockSpec((tq, q.shape[1]), lambda i: (i, 0)),
                  pl.BlockSpec((nset, PEER_NKEYS, PEER_KEY_DIM), lambda i: (0, 0, 0))],
        out_specs=[pl.BlockSpec((hh, g8, 8, tq), lambda i: (0, 0, 0, i)),
                   pl.BlockSpec((hh, g8, 8, tq), lambda i: (0, 0, 0, i)),
                   pl.BlockSpec((hh, g16, 16, tq), lambda i: (0, 0, 0, i)),
                   pl.BlockSpec((hh, g16, 16, tq), lambda i: (0, 0, 0, i))],
        out_shape=[jax.ShapeDtypeStruct((hh, g8, 8, t), F32),
                   jax.ShapeDtypeStruct((hh, g8, 8, t), F32),
                   jax.ShapeDtypeStruct((hh, g16, 16, t), BF16),
                   jax.ShapeDtypeStruct((hh, g16, 16, t), BF16)],
        scratch_shapes=[pltpu.VMEM((nset, PEER_NKEYS, tq), F32),
                        pltpu.VMEM((nset, PEER_TOPK, 8, tq), F32)],
        compiler_params=_cparams(("parallel",)),
        name="peer_select",
    )(q, keys)


PEER_TM = 512
PEER_TE = 1024
PEER_KEY_BLOCK = 32
PEER_ROW_BLOCK = 4


def _gelu(x):
    return 0.5 * x * (1.0 + lax.erf(x * np.float32(math.sqrt(0.5))))


def _peer_dense_kernel(h_ref, u_ref, vt_ref, n1_ref, e1_ref, r2_ref, e2_ref, x_ref, g2_ref, o_ref,
                       acc_ref, at_ref, wa_ref, bcn_ref, bce_ref):
    j = pl.program_id(1)
    nsteps = pl.num_programs(1)
    tm = PEER_TM
    te = PEER_TE
    nk = PEER_NKEYS

    @pl.when(j == 0)
    def _():
        acc_ref[...] = jnp.zeros_like(acc_ref)

    na = te // nk
    kb = PEER_KEY_BLOCK
    kg = kb // 16

    at_ref[...] = _dot_nt(u_ref[...], h_ref[...])

    for h, al in itertools.product(range(PEER_HEADS), range(na)):
        bcn_ref[h, al] = jnp.broadcast_to(n1_ref[h, j, al:al + 1, :], (16, tm)).astype(BF16)
        bce_ref[h, al] = jnp.broadcast_to(e1_ref[h, j, al:al + 1, :], (16, tm)).astype(BF16)

    for lc, kq, ab in itertools.product(range(tm // LANES), range(nk // kb), range(na // PEER_ROW_BLOCK)):
        ls = slice(lc * LANES, (lc + 1) * LANES)
        gs = slice(kq * kg, (kq + 1) * kg)
        als = range(ab * PEER_ROW_BLOCK, (ab + 1) * PEER_ROW_BLOCK)
        w = {al: jnp.zeros((kg, 16, LANES), BF16) for al in als}
        for h in range(PEER_HEADS):
            r2 = r2_ref[h, gs, :, ls]
            e2 = e2_ref[h, gs, :, ls]
            for al in als:
                n1 = bcn_ref[h, al, :, ls]
                e1 = bce_ref[h, al, :, ls]
                w[al] = w[al] + jnp.where(r2 < n1, e2, jnp.zeros_like(e2)) * e1
        for al in als:
            rs = slice(al * nk + kq * kb, al * nk + (kq + 1) * kb)
            wa_ref[rs, ls] = (w[al].reshape(kb, LANES).astype(F32) * _gelu(at_ref[rs, ls])).astype(BF16)

    acc_ref[...] += _dot(vt_ref[0], wa_ref[...])

    @pl.when(j == nsteps - 1)
    def _():
        o_ref[...] = x_ref[...] + g2_ref[0] * acc_ref[...].T


def _peer_dense(hh, u, vt, sel, x, g2, tiles_per_row):
    t, d = hh.shape
    ne = u.shape[0]
    tm, te = PEER_TM, PEER_TE
    one = pl.Buffered(1)
    na = te // PEER_NKEYS
    hd, g8, g16 = PEER_HEADS, PEER_NKEYS // 8, PEER_NKEYS // 16
    sel_map = lambda i, j: (0, 0, 0, i)
    return pl.pallas_call(
        _peer_dense_kernel,
        grid=(t // tm, ne // te),
        in_specs=[pl.BlockSpec((tm, d), lambda i, j: (i, 0), pipeline_mode=one),
                  pl.BlockSpec((te, d), lambda i, j: (j, 0)),
                  pl.BlockSpec((1, d, te), lambda i, j: (j, 0, 0)),
                  pl.BlockSpec((hd, g8, 8, tm), sel_map, pipeline_mode=one),
                  pl.BlockSpec((hd, g8, 8, tm), sel_map, pipeline_mode=one),
                  pl.BlockSpec((hd, g16, 16, tm), sel_map, pipeline_mode=one),
                  pl.BlockSpec((hd, g16, 16, tm), sel_map, pipeline_mode=one),
                  pl.BlockSpec((tm, d), lambda i, j: (i, 0), pipeline_mode=one),
                  pl.BlockSpec((1, 1, d), lambda i, j: (i // tiles_per_row, 0, 0))],
        out_specs=pl.BlockSpec((tm, d), lambda i, j: (i, 0)),
        out_shape=jax.ShapeDtypeStruct((t, d), F32),
        scratch_shapes=[pltpu.VMEM((d, tm), F32),
                        pltpu.VMEM((te, tm), F32),
                        pltpu.VMEM((te, tm), BF16),
                        pltpu.VMEM((hd, na, 16, tm), BF16),
                        pltpu.VMEM((hd, na, 16, tm), BF16)],
        compiler_params=_cparams(("parallel", "arbitrary")),
        name="peer_dense",
    )(hh, u, vt, *sel, x, g2)


def _peer_block(x, g, shift, scale, gate2, wq, keys, u, vt):
    b, l, d = x.shape
    t = b * l
    x_rows = x if shift.shape[0] > 1 else x.reshape(1, t, d)
    q, hh = _norm_mod_matmul(x_rows, g, shift, scale, wq, BF16, 512, emit_h=True)
    sel = _peer_select(q.reshape(t, -1), keys)
    bm = gate2.shape[0]
    tiles_per_row = (l // PEER_TM) if bm > 1 else (t // PEER_TM)
    out = _peer_dense(hh.reshape(t, d), u, vt, sel, x.reshape(t, d), gate2.reshape(bm, 1, d), tiles_per_row)
    return out.reshape(b, l, d)


def _rmsnorm_kernel(x_ref, g_ref, o_ref):
    x = x_ref[...]
    o_ref[...] = x * lax.rsqrt(jnp.mean(x * x, axis=-1, keepdims=True) + EPS) * g_ref[...]


def _rmsnorm(x, g):
    t, d = x.shape
    tm = 512
    return pl.pallas_call(
        _rmsnorm_kernel,
        grid=(t // tm,),
        in_specs=[pl.BlockSpec((tm, d), lambda i: (i, 0)), pl.BlockSpec((1, d), lambda i: (0, 0))],
        out_specs=pl.BlockSpec((tm, d), lambda i: (i, 0)),
        out_shape=jax.ShapeDtypeStruct((t, d), F32),
        compiler_params=_cparams(("parallel",)),
        name="final_rmsnorm",
    )(x, g.reshape(1, d))


def _pack_w_in(w_in):
    q, k, v, z, xbc, dt, glu, gate = jnp.split(
        w_in, np.cumsum([NA_WIDTH, NA_WIDTH, NA_WIDTH, SSM_INNER, SSM_XBC, 2 * SSM_HEADS, 2 * CONV_CH]).tolist(),
        axis=-1)
    packed = jnp.concatenate([gate, glu, q, k, v, z, xbc], axis=-1).astype(BF16)
    dt_w = jnp.pad(dt, ((0, 0), (0, LANES - 2 * SSM_HEADS))).astype(BF16)
    return packed, dt_w


def _trunk_layer(xl, xc, c_rows, need_ctx, w_mod, b_mod, norm1_g, norm2_g, w_in, na_rpb, na_wo,
                 ssm_conv_w, ssm_conv_b, ssm_dt_bias, ssm_a_log, ssm_d, ssm_norm_g, ssm_wo,
                 cv_dw_w, cv_dw_b, cv_ln_g, cv_ln_b, cv_wo, cv_bo, w_out, peer_wq, peer_keys, peer_u, peer_v):
    b, s, d = xl.shape
    mod = _modulation(c_rows, w_mod, b_mod)
    sh1, sc1, g1, sh2, sc2, g2 = [mod[:b, i * d:(i + 1) * d] for i in range(6)]
    csh1, csc1, cg1, csh2, csc2, cg2 = [mod[b:b + 1, i * d:(i + 1) * d] for i in range(6)]

    w_pack, w_dt = _pack_w_in(w_in)
    dt_bias = jnp.pad(ssm_dt_bias.astype(F32).reshape(-1), (0, LANES - 2 * SSM_HEADS))

    cb, cl, _ = xc.shape
    xc_flat = xc.reshape(1, cb * cl, d)
    p_l = _norm_mod_matmul(xl, norm1_g, sh1, sc1, w_pack, BF16, 512)
    p_c = _norm_mod_matmul(xc_flat, norm1_g, csh1, csc1, w_pack, BF16, 512).reshape(cb, cl, -1)
    dt_l = _norm_mod_matmul(xl, norm1_g, sh1, sc1, w_dt, F32, LANES, softplus_bias=dt_bias)
    dt_c = _norm_mod_matmul(xc_flat, norm1_g, csh1, csc1, w_dt, F32, LANES,
                            softplus_bias=dt_bias).reshape(cb, cl, -1)

    bias_tab = _na_bias_table(na_rpb, s // GRID_W)
    y_att, y_att_c = _neighbourhood_attention(p_l, p_c, bias_tab, need_ctx)

    xbc_l = _dwconv(p_l, OFF_XBC, None, SSM_XBC, ssm_conv_w, ssm_conv_b, None, None, F32)
    xbc_c = _dwconv(p_c, OFF_XBC, None, SSM_XBC, ssm_conv_w, ssm_conv_b, None, None, F32)
    y_ssm, y_ssm_c = _ssd_scan(xbc_l, xbc_c, dt_l, dt_c, ssm_a_log, ssm_d)

    wa, ws, wc, wo = (na_wo.astype(BF16), ssm_wo.astype(BF16), cv_wo.astype(BF16), w_out.astype(BF16))
    y_cv = _dwconv(p_l, OFF_GLU_A, OFF_GLU_G, CONV_CH, cv_dw_w, cv_dw_b, cv_ln_g, cv_ln_b, BF16)
    xl = _merge(xl, g1, p_l, y_att, y_ssm, y_cv, ssm_norm_g, wa, ws, wc, cv_bo, wo)
    if need_ctx:
        y_cv_c = _dwconv(p_c, OFF_GLU_A, OFF_GLU_G, CONV_CH, cv_dw_w, cv_dw_b, cv_ln_g, cv_ln_b, BF16)
        xc = _merge(xc, cg1, p_c, y_att_c, y_ssm_c, y_cv_c, ssm_norm_g, wa, ws, wc, cv_bo, wo)

    wq = peer_wq.astype(BF16)
    keys = peer_keys.astype(BF16).reshape(2 * PEER_HEADS, PEER_NKEYS, PEER_KEY_DIM)
    u = peer_u.astype(BF16)
    vt = peer_v.astype(BF16).reshape(PEER_EXPERTS // PEER_TE, PEER_TE, d).transpose(0, 2, 1)
    xl = _peer_block(xl, norm2_g, sh2, sc2, g2, wq, keys, u, vt)
    if need_ctx:
        xc = _peer_block(xc, norm2_g, csh2, csc2, cg2, wq, keys, u, vt)
    return xl, xc


def kernel(x, c, ctx, c_ctx, w_mod, b_mod, norm1_g, norm2_g, w_in, na_rpb, na_wo, ssm_conv_w, ssm_conv_b,
           ssm_dt_bias, ssm_A_log, ssm_D, ssm_norm_g, ssm_wo, cv_dw_w, cv_dw_b, cv_ln_g, cv_ln_b, cv_wo,
           cv_bo, w_out, peer_wq, peer_keys, peer_u, peer_v, final_norm_g):
    b, s, d = x.shape
    depth = w_mod.shape[0]
    rows = -(-(b + 1) // 8) * 8
    c_rows = jnp.zeros((rows, d), F32).at[:b].set(c).at[b].set(c_ctx)
    xl, xc = x, ctx
    for l in range(depth):
        xl, xc = _trunk_layer(xl, xc, c_rows, l < depth - 1, w_mod[l], b_mod[l], norm1_g[l], norm2_g[l],
                              w_in[l], na_rpb[l], na_wo[l], ssm_conv_w[l], ssm_conv_b[l], ssm_dt_bias[l],
                              ssm_A_log[l], ssm_D[l], ssm_norm_g[l], ssm_wo[l], cv_dw_w[l], cv_dw_b[l],
                              cv_ln_g[l], cv_ln_b[l], cv_wo[l], cv_bo[l], w_out[l],
                              peer_wq[l], peer_keys[l], peer_u[l], peer_v[l])
    return _rmsnorm(xl.reshape(b * s, d), final_norm_g).reshape(b, s, d)
```

```python
import functools
import itertools
import math

import numpy as np
import jax
import jax.numpy as jnp
from jax import lax
from jax.experimental import pallas as pl
from jax.experimental.pallas import tpu as pltpu

F32 = jnp.float32
BF16 = jnp.bfloat16
EPS = 1e-6

D_MODEL = 2048
GRID_W = 64
NA_HEADS = 16
NA_HEAD_DIM = 64
NA_WIDTH = NA_HEADS * NA_HEAD_DIM
WIN_R = 8
WIN_C = 16
SSM_HEADS = 16
SSM_HEAD_DIM = 64
SSM_INNER = SSM_HEADS * SSM_HEAD_DIM
SSM_GROUPS = 2
SSM_STATE = 128
SSM_XBC = SSM_INNER + 2 * SSM_GROUPS * SSM_STATE
SSM_CONV = 5
SSM_CHUNK = 128
CONV_CH = 1024
CONV_K = 31
PEER_HEADS = 8
PEER_NKEYS = 128
PEER_KEY_DIM = 128
PEER_TOPK = 16
PEER_EXPERTS = PEER_NKEYS * PEER_NKEYS

OFF_GATE = 0
OFF_GLU_A = 6144
OFF_GLU_G = 7168
OFF_Q = 8192
OFF_K = 9216
OFF_V = 10240
OFF_Z = 11264
OFF_XBC = 12288
PACK_COLS = 13824

LANES = 128
VMEM_LIMIT = 56 * 1024 * 1024


def _cparams(sem):
    return pltpu.CompilerParams(dimension_semantics=sem, vmem_limit_bytes=VMEM_LIMIT)


def _dot(a, b):
    return jnp.dot(a, b, preferred_element_type=F32)


def _dot_nt(a, b):
    return lax.dot_general(a, b, (((1,), (1,)), ((), ())), preferred_element_type=F32)


def _dot_tn(a, b):
    return lax.dot_general(a, b, (((0,), (0,)), ((), ())), preferred_element_type=F32)


def _split3(a):
    hi = a.astype(BF16)
    r1 = a - hi.astype(F32)
    mid = r1.astype(BF16)
    lo = (r1 - mid.astype(F32)).astype(BF16)
    return hi, mid, lo


def _dot_exact_rhs01(a, m01):
    hi, mid, lo = _split3(a)
    return _dot(hi, m01) + _dot(mid, m01) + _dot(lo, m01)


def _dot_exact_lhs01(m01, a):
    hi, mid, lo = _split3(a)
    return _dot(m01, hi) + _dot(m01, mid) + _dot(m01, lo)


def _sigmoid(x):
    return 1.0 / (1.0 + jnp.exp(-x))


def _silu(x):
    return x * _sigmoid(x)


def _mod_kernel(c_ref, w_ref, b_ref, o_ref):
    a = _silu(c_ref[...]).astype(BF16)
    o_ref[...] = _dot(a, w_ref[...].astype(BF16)) + b_ref[...]


def _modulation(cs, w_mod, b_mod):
    r, d = cs.shape
    n = w_mod.shape[1]
    tn = 1024
    return pl.pallas_call(
        _mod_kernel,
        grid=(n // tn,),
        in_specs=[pl.BlockSpec((r, d), lambda j: (0, 0)),
                  pl.BlockSpec((d, tn), lambda j: (0, j)),
                  pl.BlockSpec((1, tn), lambda j: (0, j))],
        out_specs=pl.BlockSpec((r, tn), lambda j: (0, j)),
        out_shape=jax.ShapeDtypeStruct((r, n), F32),
        compiler_params=_cparams(("arbitrary",)),
        name="modulation",
    )(cs, w_mod, b_mod.reshape(1, n))


def _nmm_kernel(x_ref, g_ref, sh_ref, sc_ref, w_ref, *rest, softplus_bias, emit_h):
    if softplus_bias:
        bias_ref, rest = rest[0], rest[1:]
    if emit_h:
        o_ref, ho_ref, h_ref = rest
    else:
        o_ref, h_ref = rest

    @pl.when(pl.program_id(2) == 0)
    def _():
        x = x_ref[0]
        ms = jnp.mean(x * x, axis=-1, keepdims=True)
        y = x * lax.rsqrt(ms + EPS) * g_ref[...]
        h = (y * (1.0 + sc_ref[0]) + sh_ref[0]).astype(BF16)
        h_ref[...] = h
        if emit_h:
            ho_ref[0] = h

    acc = _dot(h_ref[...], w_ref[0])
    if softplus_bias:
        t = acc + bias_ref[...]
        acc = jnp.maximum(t, 0.0) + jnp.log1p(jnp.exp(-jnp.abs(t)))
    o_ref[0] = acc.astype(o_ref.dtype)


def _column_blocks(w, tn):
    d, n = w.shape
    return w.reshape(d, n // tn, tn).transpose(1, 0, 2)


def _norm_mod_matmul(x, g, shift, scale, w, out_dtype, tn, softplus_bias=None, emit_h=False):
    b, l, d = x.shape
    n = w.shape[1]
    w = _column_blocks(w, tn)
    tm = min(l, 1024)
    bm = shift.shape[0]
    mod_map = (lambda i, m, j: (i, 0, 0)) if bm > 1 else (lambda i, m, j: (0, 0, 0))
    in_specs = [pl.BlockSpec((1, tm, d), lambda i, m, j: (i, m, 0)),
                pl.BlockSpec((1, d), lambda i, m, j: (0, 0)),
                pl.BlockSpec((1, 1, d), mod_map),
                pl.BlockSpec((1, 1, d), mod_map),
                pl.BlockSpec((1, d, tn), lambda i, m, j: (j, 0, 0))]
    args = [x, g.reshape(1, d), shift.reshape(bm, 1, d), scale.reshape(bm, 1, d), w]
    if softplus_bias is not None:
        in_specs.append(pl.BlockSpec((1, tn), lambda i, m, j: (0, j)))
        args.append(softplus_bias.reshape(1, n))
    out_shape = [jax.ShapeDtypeStruct((b, l, n), out_dtype)]
    out_specs = [pl.BlockSpec((1, tm, tn), lambda i, m, j: (i, m, j))]
    if emit_h:
        out_shape.append(jax.ShapeDtypeStruct((b, l, d), BF16))
        out_specs.append(pl.BlockSpec((1, tm, d), lambda i, m, j: (i, m, 0)))
    res = pl.pallas_call(
        functools.partial(_nmm_kernel, softplus_bias=softplus_bias is not None, emit_h=emit_h),
        grid=(b, l // tm, n // tn),
        in_specs=in_specs,
        out_specs=out_specs,
        out_shape=out_shape,
        scratch_shapes=[pltpu.VMEM((tm, d), BF16)],
        compiler_params=_cparams(("parallel", "parallel", "arbitrary")),
        name="norm_mod_matmul",
    )(*args)
    return res if emit_h else res[0]


def _na_bias_table(rpb, rows):
    wr = min(WIN_R, rows)
    j = np.arange(GRID_W)
    cstart = np.clip(j - WIN_C // 2, 0, GRID_W - WIN_C)
    kc = np.arange(GRID_W)
    mask = (kc[None, :] >= cstart[:, None]) & (kc[None, :] < cstart[:, None] + WIN_C)
    col_off = np.clip(kc[None, :] - j[:, None], -(WIN_C - 1), WIN_C - 1) + WIN_C - 1
    dd = np.arange(wr)[:, None]
    ww = np.arange(wr)[None, :]
    row_idx = ww - dd + WIN_R - 1
    t = rpb.astype(F32)[:, row_idx][:, :, :, col_off]
    t = jnp.where(mask[None, None, None], t, -1e30)
    t = t.transpose(0, 1, 3, 2, 4).reshape(NA_HEADS, wr, GRID_W, wr * GRID_W)
    return t.reshape(NA_HEADS // 2, 2, wr, GRID_W, wr * GRID_W)


NA_ROW_GROUP = 8


def _na_kernel(q_ref, k_ref, v_ref, kc_ref, vc_ref, bias_ref, *rest, rows, wr, with_ctx):
    if with_ctx:
        qc_ref, o_ref, oc_ref = rest
    else:
        (o_ref,) = rest
    lane = lax.broadcasted_iota(jnp.int32, (1, LANES), 1)
    lo = lane < NA_HEAD_DIM
    kc = kc_ref[0]
    vc = vc_ref[0]
    scale = NA_HEAD_DIM ** -0.5
    nwin = wr * GRID_W

    def attend(problems):
        chains = []
        for q, kw, vw, bias_of in problems:
            for h in range(2):
                sel = lo if h == 0 else jnp.logical_not(lo)
                qh = jnp.where(sel, q, jnp.zeros_like(q))
                s_c = _dot_nt(qh, kc) * scale
                s_w = None if kw is None else _dot_nt(qh, kw) * scale + bias_of(h)
                chains.append((s_c, s_w, vw))
        maxes = []
        for s_c, s_w, _ in chains:
            m = jnp.max(s_c, axis=-1, keepdims=True)
            if s_w is not None:
                m = jnp.maximum(m, jnp.max(s_w, axis=-1, keepdims=True))
            maxes.append(m)
        probs = []
        for (s_c, s_w, _), m in zip(chains, maxes):
            probs.append((jnp.exp(s_c - m), None if s_w is None else jnp.exp(s_w - m)))
        outs = []
        for (_, _, vw), (p_c, p_w) in zip(chains, probs):
            den = jnp.sum(p_c, axis=-1, keepdims=True)
            o = _dot(p_c.astype(BF16), vc)
            if p_w is not None:
                den = den + jnp.sum(p_w, axis=-1, keepdims=True)
                o = o + _dot(p_w.astype(BF16), vw)
            outs.append(o / den)
        return [jnp.where(lo, outs[2 * i], outs[2 * i + 1]) for i in range(len(problems))]

    def body(g, carry):
        problems, q0s = [], []
        for rr in range(NA_ROW_GROUP):
            r = g * NA_ROW_GROUP + rr
            rs = jnp.clip(r - wr // 2, 0, rows - wr)
            d = r - rs
            q0 = pl.multiple_of(r * GRID_W, GRID_W)
            k0 = pl.multiple_of(rs * GRID_W, GRID_W)
            problems.append((q_ref[0, pl.ds(q0, GRID_W), :], k_ref[0, pl.ds(k0, nwin), :],
                             v_ref[0, pl.ds(k0, nwin), :], lambda h, d=d: bias_ref[0, h, d]))
            q0s.append(q0)
        for q0, o in zip(q0s, attend(problems)):
            o_ref[0, pl.ds(q0, GRID_W), :] = o.astype(o_ref.dtype)
        return carry

    lax.fori_loop(0, rows // NA_ROW_GROUP, body, 0)
    if with_ctx:
        oc_ref[0] = attend([(qc_ref[0], None, None, None)])[0].astype(oc_ref.dtype)


def _neighbourhood_attention(pl_lat, pl_ctx, bias_tab, with_ctx):
    b, s, _ = pl_lat.shape
    cl = pl_ctx.shape[1]
    rows = s // GRID_W
    wr = min(WIN_R, rows)
    nhp = NA_HEADS // 2
    qb, kb, vb = OFF_Q // LANES, OFF_K // LANES, OFF_V // LANES
    in_specs = [pl.BlockSpec((1, s, LANES), lambda hp, i: (i, 0, qb + hp)),
                pl.BlockSpec((1, s, LANES), lambda hp, i: (i, 0, kb + hp)),
                pl.BlockSpec((1, s, LANES), lambda hp, i: (i, 0, vb + hp)),
                pl.BlockSpec((1, cl, LANES), lambda hp, i: (i, 0, kb + hp)),
                pl.BlockSpec((1, cl, LANES), lambda hp, i: (i, 0, vb + hp)),
                pl.BlockSpec((1, 2, wr, GRID_W, wr * GRID_W), lambda hp, i: (hp, 0, 0, 0, 0))]
    args = [pl_lat, pl_lat, pl_lat, pl_ctx, pl_ctx, bias_tab]
    out_shape = [jax.ShapeDtypeStruct((b, s, NA_WIDTH), BF16)]
    out_specs = [pl.BlockSpec((1, s, LANES), lambda hp, i: (i, 0, hp))]
    if with_ctx:
        in_specs.append(pl.BlockSpec((1, cl, LANES), lambda hp, i: (i, 0, qb + hp)))
        args.append(pl_ctx)
        out_shape.append(jax.ShapeDtypeStruct((b, cl, NA_WIDTH), BF16))
        out_specs.append(pl.BlockSpec((1, cl, LANES), lambda hp, i: (i, 0, hp)))
    res = pl.pallas_call(
        functools.partial(_na_kernel, rows=rows, wr=wr, with_ctx=with_ctx),
        grid=(nhp, b),
        in_specs=in_specs,
        out_specs=out_specs,
        out_shape=out_shape,
        compiler_params=_cparams(("parallel", "parallel")),
        name="neighbourhood_attention",
    )(*args)
    return (res[0], res[1]) if with_ctx else (res[0], None)


CONV_TL = 256
CONV_HALO = 16


def _dwconv_kernel(*refs, taps, ch, glu, tl):
    if glu:
        (a_p, a_c, a_n, g_p, g_c, g_n, w_ref, b_ref, lng_ref, lnb_ref, o_ref, u_ref, y_ref) = refs
    else:
        (a_p, a_c, a_n, w_ref, b_ref, o_ref, u_ref) = refs
    i = pl.program_id(1)
    n = pl.num_programs(1)
    pad = (taps - 1) // 2
    hl = CONV_HALO

    def pre(a, g):
        a = a.astype(F32)
        if glu:
            return a * _sigmoid(g.astype(F32))
        return a

    u_ref[hl:hl + tl, :] = pre(a_c[0], g_c[0] if glu else None)
    top = pre(a_p[0, tl - hl:tl, :], g_p[0, tl - hl:tl, :] if glu else None)
    u_ref[0:hl, :] = jnp.where(i > 0, top, 0.0)
    bot = pre(a_n[0, 0:hl, :], g_n[0, 0:hl, :] if glu else None)
    u_ref[hl + tl:hl + tl + hl, :] = jnp.where(i < n - 1, bot, 0.0)

    for cc in range(ch // LANES):
        cs = slice(cc * LANES, (cc + 1) * LANES)
        acc = jnp.zeros((tl, LANES), F32) + b_ref[:, cs]
        for k in range(taps):
            off = hl - pad + k
            acc = acc + w_ref[k:k + 1, cs] * u_ref[off:off + tl, cs]
        if glu:
            y_ref[:, cs] = acc
        else:
            o_ref[0, :, cs] = _silu(acc).astype(o_ref.dtype)

    if glu:
        y = y_ref[...]
        mu = jnp.mean(y, axis=-1, keepdims=True)
        yc = y - mu
        var = jnp.mean(yc * yc, axis=-1, keepdims=True)
        z = yc * lax.rsqrt(var + EPS) * lng_ref[...] + lnb_ref[...]
        o_ref[0] = _silu(z).astype(o_ref.dtype)


def _dwconv(src, off_a, off_g, ch, w, b, ln_g, ln_b, out_dtype):
    bsz, l, _ = src.shape
    taps = w.shape[0]
    glu = off_g is not None
    tl = CONV_TL
    nt = l // tl
    ca = off_a // ch

    def spec(cb, delta):
        def imap(i, t):
            return (i, jnp.clip(t + delta, 0, nt - 1), cb)
        return pl.BlockSpec((1, tl, ch), imap)

    in_specs = [spec(ca, -1), spec(ca, 0), spec(ca, 1)]
    args = [src, src, src]
    if glu:
        cg = off_g // ch
        in_specs += [spec(cg, -1), spec(cg, 0), spec(cg, 1)]
        args += [src, src, src]
    in_specs += [pl.BlockSpec((taps, ch), lambda i, t: (0, 0)), pl.BlockSpec((1, ch), lambda i, t: (0, 0))]
    args += [w, b.reshape(1, ch)]
    scratch = [pltpu.VMEM((tl + 2 * CONV_HALO, ch), F32)]
    if glu:
        in_specs += [pl.BlockSpec((1, ch), lambda i, t: (0, 0)), pl.BlockSpec((1, ch), lambda i, t: (0, 0))]
        args += [ln_g.reshape(1, ch), ln_b.reshape(1, ch)]
        scratch.append(pltpu.VMEM((tl, ch), F32))
    return pl.pallas_call(
        functools.partial(_dwconv_kernel, taps=taps, ch=ch, glu=glu, tl=tl),
        grid=(bsz, nt),
        in_specs=in_specs,
        out_specs=pl.BlockSpec((1, tl, ch), lambda i, t: (i, t, 0)),
        out_shape=jax.ShapeDtypeStruct((bsz, l, ch), out_dtype),
        scratch_shapes=scratch,
        compiler_params=_cparams(("parallel", "parallel")),
        name="dwconv_glu" if glu else "dwconv_ssm",
    )(*args)


def _ssd_kernel(xl_ref, xc_ref, dtl_ref, dtc_ref, dttl_ref, dttc_ref, alog_ref, alogt_ref, dskip_ref,
                exp_ref, yl_ref, yc_ref, h_ref, *, ncc):
    d = pl.program_id(1)
    c = pl.program_id(2)
    q = SSM_CHUNK
    hh = SSM_HEADS
    hpg = SSM_HEADS // SSM_GROUPS
    gw = hpg * SSM_HEAD_DIM
    n = SSM_STATE
    is_ctx = c < ncc
    fwd = d == 0

    @pl.when(c == 0)
    def _():
        h_ref[...] = jnp.zeros_like(h_ref)

    xbc = jnp.where(is_ctx, xc_ref[0], xl_ref[0]).astype(F32)
    x = xbc[:, :SSM_INNER]
    dt2 = jnp.where(is_ctx, dtc_ref[0], dtl_ref[0])
    dt_col = jnp.where(fwd, dt2[:, 0:hh], dt2[:, hh:2 * hh])
    dtt2 = jnp.where(is_ctx, dttc_ref[0], dttl_ref[0])
    dt_row = jnp.where(fwd, dtt2[0:hh, :], dtt2[hh:2 * hh, :])
    alog = alog_ref[...]
    nega_row = -jnp.exp(jnp.where(fwd, alog[0:1, :], alog[1:2, :]))
    alogt = alogt_ref[...]
    nega_col = -jnp.exp(jnp.where(fwd, alogt[:, 0:1], alogt[:, 1:2]))
    a_col = dt_col * nega_row
    a_row = dt_row * nega_col

    li = lax.broadcasted_iota(jnp.int32, (q, q), 0)
    si = lax.broadcasted_iota(jnp.int32, (q, q), 1)
    sgn = jnp.where(fwd, 1, -1)
    tri = (si - li) * sgn <= 0
    trit = (li - si) * sgn <= 0
    tri_b = jnp.where(tri, 1.0, 0.0).astype(BF16)
    trit_b = jnp.where(trit, 1.0, 0.0).astype(BF16)
    e_col = _dot_exact_lhs01(tri_b, a_col)
    e_row = _dot_exact_rhs01(a_row, trit_b)
    e_tot = jnp.where(fwd, e_col[q - 1:q, :], e_col[0:1, :])

    expand = exp_ref[...]
    w_dt = _dot_exact_rhs01(dt_col, expand)
    w_dec = _dot_exact_rhs01(jnp.exp(e_tot - e_col), expand)
    w_off = _dot_exact_rhs01(jnp.exp(e_col), expand)
    w_tot = jnp.where(fwd, w_off[q - 1:q, :], w_off[0:1, :])

    xdt = x * w_dt
    xdt_b = xdt.astype(BF16)
    xdec_b = (xdt * w_dec).astype(BF16)
    lane = lax.broadcasted_iota(jnp.int32, (1, LANES), 1)
    lo = lane < SSM_HEAD_DIM

    y_parts = []
    for g in range(SSM_GROUPS):
        bm = xbc[:, SSM_INNER + g * n:SSM_INNER + (g + 1) * n].astype(BF16)
        cm = xbc[:, SSM_INNER + (SSM_GROUPS + g) * n:SSM_INNER + (SSM_GROUPS + g + 1) * n].astype(BF16)
        cb = _dot_nt(cm, bm)
        hg = h_ref[g]
        y_off = _dot(cm, hg.astype(BF16))
        gs = slice(g * gw, (g + 1) * gw)
        h_ref[g] = hg * w_tot[:, gs] + _dot_tn(bm, xdec_b[:, gs])
        for pr in range(hpg // 2):
            h0 = g * hpg + 2 * pr
            cs = slice(h0 * SSM_HEAD_DIM, (h0 + 2) * SSM_HEAD_DIM)
            xp = xdt_b[:, cs]
            acc = None
            for k in range(2):
                h = h0 + k
                lm = jnp.where(tri, jnp.exp(e_col[:, h:h + 1] - e_row[h:h + 1, :]), 0.0)
                mh = (cb * lm).astype(BF16)
                sel = lo if k == 0 else jnp.logical_not(lo)
                t = _dot(mh, jnp.where(sel, xp, jnp.zeros_like(xp)))
                acc = t if acc is None else acc + t
            ys = slice(2 * pr * SSM_HEAD_DIM, (2 * pr + 2) * SSM_HEAD_DIM)
            y_parts.append(acc + y_off[:, ys] * w_off[:, cs])
    y = jnp.concatenate(y_parts, axis=-1)
    y = y + jnp.where(fwd, dskip_ref[...], 0.0) * x

    @pl.when(is_ctx)
    def _():
        yc_ref[0, 0] = y

    @pl.when(jnp.logical_not(is_ctx))
    def _():
        yl_ref[0, 0] = y


def _ssd_scan(xbc_l, xbc_c, dt_l, dt_c, a_log, d_skip):
    b, l, _ = xbc_l.shape
    cl = xbc_c.shape[1]
    q = SSM_CHUNK
    ncl, ncc = l // q, cl // q
    nc = ncl + ncc
    hh = SSM_HEADS
    dtt_l = jnp.swapaxes(dt_l[..., :2 * hh], 1, 2)
    dtt_c = jnp.swapaxes(dt_c[..., :2 * hh], 1, 2)
    expand = jnp.asarray(np.kron(np.eye(hh), np.ones((1, SSM_HEAD_DIM))), BF16)
    dskip = jnp.repeat(d_skip.astype(F32), SSM_HEAD_DIM).reshape(1, SSM_INNER)

    def lat_chunk(d, c):
        cc = jnp.maximum(c - ncc, 0)
        return jnp.where(d == 0, cc, ncl - 1 - cc)

    def ctx_chunk(d, c):
        cc = jnp.minimum(c, ncc - 1)
        return jnp.where(d == 0, cc, ncc - 1 - cc)

    in_specs = [pl.BlockSpec((1, q, SSM_XBC), lambda i, d, c: (i, lat_chunk(d, c), 0)),
                pl.BlockSpec((1, q, SSM_XBC), lambda i, d, c: (i, ctx_chunk(d, c), 0)),
                pl.BlockSpec((1, q, LANES), lambda i, d, c: (i, lat_chunk(d, c), 0)),
                pl.BlockSpec((1, q, LANES), lambda i, d, c: (i, ctx_chunk(d, c), 0)),
                pl.BlockSpec((1, 2 * hh, q), lambda i, d, c: (i, 0, lat_chunk(d, c))),
                pl.BlockSpec((1, 2 * hh, q), lambda i, d, c: (i, 0, ctx_chunk(d, c))),
                pl.BlockSpec((2, hh), lambda i, d, c: (0, 0)),
                pl.BlockSpec((hh, 2), lambda i, d, c: (0, 0)),
                pl.BlockSpec((1, SSM_INNER), lambda i, d, c: (0, 0)),
                pl.BlockSpec((hh, SSM_INNER), lambda i, d, c: (0, 0))]
    out_specs = [pl.BlockSpec((1, 1, q, SSM_INNER), lambda i, d, c: (i, d, lat_chunk(d, c), 0)),
                 pl.BlockSpec((1, 1, q, SSM_INNER), lambda i, d, c: (i, d, ctx_chunk(d, c), 0))]
    out_shape = [jax.ShapeDtypeStruct((b, 2, l, SSM_INNER), F32),
                 jax.ShapeDtypeStruct((b, 2, cl, SSM_INNER), F32)]
    gw = (SSM_HEADS // SSM_GROUPS) * SSM_HEAD_DIM
    return pl.pallas_call(
        functools.partial(_ssd_kernel, ncc=ncc),
        grid=(b, 2, nc),
        in_specs=in_specs,
        out_specs=out_specs,
        out_shape=out_shape,
        scratch_shapes=[pltpu.VMEM((SSM_GROUPS, SSM_STATE, gw), F32)],
        compiler_params=_cparams(("parallel", "arbitrary", "arbitrary")),
        name="ssd_scan",
    )(xbc_l, xbc_c, dt_l, dt_c, dtt_l, dtt_c, a_log.astype(F32), a_log.astype(F32).T, dskip, expand)


def _merge_kernel(x_ref, g1_ref, gate_ref, att_ref, ys_ref, z_ref, cv_ref, ng_ref,
                  wa_ref, ws_ref, wc_ref, bc_ref, wo_ref, o_ref):
    d = D_MODEL
    ys = (ys_ref[0, 0] + ys_ref[0, 1]) * _silu(z_ref[0].astype(F32))
    gsz = SSM_INNER // SSM_GROUPS
    parts = []
    for g in range(SSM_GROUPS):
        yg = ys[:, g * gsz:(g + 1) * gsz]
        parts.append(yg * lax.rsqrt(jnp.mean(yg * yg, axis=-1, keepdims=True) + EPS))
    yn = (jnp.concatenate(parts, axis=-1) * ng_ref[...]).astype(BF16)
    p_s = _dot(yn, ws_ref[...])
    p_a = _dot(att_ref[0], wa_ref[...])
    p_c = _dot(cv_ref[0], wc_ref[...]) + bc_ref[...]
    gl = gate_ref[0].astype(F32)
    m = (_sigmoid(gl[:, 0:d]) * p_a + _sigmoid(gl[:, d:2 * d]) * p_s
         + _sigmoid(gl[:, 2 * d:3 * d]) * p_c).astype(BF16)
    o_ref[0] = x_ref[0] + g1_ref[0] * _dot(m, wo_ref[...])


def _merge(x, g1, packed, y_att, y_ssm2, y_cv, norm_g, wa, ws, wc, bc, wo):
    b, l, d = x.shape
    tm = 256
    bm = g1.shape[0]
    g_map = (lambda i, t: (i, 0, 0)) if bm > 1 else (lambda i, t: (0, 0, 0))
    const = lambda i, t: (0, 0)
    one = pl.Buffered(1)
    in_specs = [pl.BlockSpec((1, tm, d), lambda i, t: (i, t, 0)),
                pl.BlockSpec((1, 1, d), g_map),
                pl.BlockSpec((1, tm, 3 * d), lambda i, t: (i, t, OFF_GATE // (3 * d))),
                pl.BlockSpec((1, tm, NA_WIDTH), lambda i, t: (i, t, 0)),
                pl.BlockSpec((1, 2, tm, SSM_INNER), lambda i, t: (i, 0, t, 0)),
                pl.BlockSpec((1, tm, SSM_INNER), lambda i, t: (i, t, OFF_Z // SSM_INNER)),
                pl.BlockSpec((1, tm, CONV_CH), lambda i, t: (i, t, 0)),
                pl.BlockSpec((1, SSM_INNER), const),
                pl.BlockSpec((NA_WIDTH, d), const, pipeline_mode=one),
                pl.BlockSpec((SSM_INNER, d), const, pipeline_mode=one),
                pl.BlockSpec((CONV_CH, d), const, pipeline_mode=one),
                pl.BlockSpec((1, d), const),
                pl.BlockSpec((d, d), const, pipeline_mode=one)]
    return pl.pallas_call(
        _merge_kernel,
        grid=(b, l // tm),
        in_specs=in_specs,
        out_specs=pl.BlockSpec((1, tm, d), lambda i, t: (i, t, 0)),
        out_shape=jax.ShapeDtypeStruct((b, l, d), F32),
        compiler_params=_cparams(("parallel", "parallel")),
        name="merge",
    )(x, g1.reshape(bm, 1, d), packed, y_att, y_ssm2, packed, y_cv, norm_g.reshape(1, -1),
      wa, ws, wc, bc.reshape(1, d), wo)


PEER_TQ = 256
NEG_INF = float("-inf")


def _sort_network(n):
    pairs = []

    def merge(lo, hi, r):
        step = r * 2
        if step < hi - lo:
            merge(lo, hi, step)
            merge(lo + r, hi, step)
            pairs.extend((i, i + r) for i in range(lo + r, hi - r, step))
        else:
            pairs.append((lo, lo + r))

    def sort(lo, hi):
        if hi - lo >= 1:
            mid = lo + (hi - lo) // 2
            sort(lo, mid)
            sort(mid + 1, hi)
            merge(lo, hi, 1)

    sort(0, n - 1)
    return pairs


def _exchange(v, i, j):
    a, b = v[i], v[j]
    v[i], v[j] = jnp.maximum(a, b), jnp.minimum(a, b)


def _top_k_sorted(v):
    k = len(v)
    v = list(v)
    for i, j in _sort_network(k):
        _exchange(v, i, j)
    for shift in (4, 2, 1):
        other = [pltpu.roll(x, shift, axis=0) for x in v]
        v = [jnp.maximum(v[i], other[k - 1 - i]) for i in range(k)]
        dist = k // 2
        while dist >= 1:
            for i in range(k):
                if i & dist == 0:
                    _exchange(v, i, i + dist)
            dist //= 2
    return v


def _peer_select_kernel(q_ref, keys_ref, n1_ref, e1_ref, r2_ref, e2_ref, s_ref, top_ref):
    tq = PEER_TQ
    k = PEER_TOPK
    sub = lax.broadcasted_iota(jnp.int32, (8, tq), 0)

    def per_set(i, carry):
        c0 = pl.multiple_of(i * PEER_KEY_DIM, PEER_KEY_DIM)
        s = _dot_nt(keys_ref[i], q_ref[:, pl.ds(c0, PEER_KEY_DIM)])
        s_ref[i] = s
        top = _top_k_sorted([s[8 * r:8 * r + 8, :] for r in range(PEER_NKEYS // 8)])
        for r in range(k):
            top_ref[i, r] = top[r]
        return carry

    lax.fori_loop(0, 2 * PEER_HEADS, per_set, 0)

    def spread(rows):
        out = rows[7]
        for s in range(6, -1, -1):
            out = jnp.where(sub == s, rows[s], out)
        return out

    def per_head(h, carry):
        t1 = [top_ref[2 * h, r] for r in range(k)]
        t2 = [top_ref[2 * h + 1, r] for r in range(k)]
        p2a, p2b, p1b = spread(t2[:8]), spread(t2[8:]), spread(t1[8:])
        cand = [t1[0] + p2a, t1[0] + p2b] + [t1[i] + p2a for i in range(1, 8)] + [p1b + t2[0]]
        cand += [jnp.full((8, tq), NEG_INF, F32)] * (k - len(cand))
        best = _top_k_sorted(cand)
        z = jnp.ones((8, tq), F32)
        for r in range(1, k):
            z = z + jnp.exp(best[r] - best[0])
        thr, rz = best[k - 1], 1.0 / z
        one, zero = jnp.ones((8, tq), F32), jnp.zeros((8, tq), F32)
        for g2 in range(PEER_NKEYS // 16):
            ranks, e2s = [], []
            for g in (2 * g2, 2 * g2 + 1):
                s1 = s_ref[2 * h, 8 * g:8 * g + 8, :]
                s2 = s_ref[2 * h + 1, 8 * g:8 * g + 8, :]
                n, r = zero, zero
                for jj in range(k):
                    n = n + jnp.where(s1 + t2[jj] >= thr, one, zero)
                    r = r + jnp.where(t2[jj] > s2, one, zero)
                n1_ref[h, g] = n
                e1_ref[h, g] = jnp.exp(s1 - t1[0]) * rz
                ranks.append(r)
                e2s.append(jnp.exp(s2 - t2[0]))
            r2_ref[h, g2] = jnp.concatenate(ranks, axis=0).astype(BF16)
            e2_ref[h, g2] = jnp.concatenate(e2s, axis=0).astype(BF16)
        return carry

    lax.fori_loop(0, PEER_HEADS, per_head, 0)


def _peer_select(q, keys):
    t = q.shape[0]
    tq = PEER_TQ
    nset = 2 * PEER_HEADS
    hh, g8, g16 = PEER_HEADS, PEER_NKEYS // 8, PEER_NKEYS // 16
    return pl.pallas_call(
        _peer_select_kernel,
        grid=(t // tq,),
        in_specs=[pl.BlockSpec((tq, q.shape[1]), lambda i: (i, 0)),
                  pl.BlockSpec((nset, PEER_NKEYS, PEER_KEY_DIM), lambda i: (0, 0, 0))],
        out_specs=[pl.BlockSpec((hh, g8, 8, tq), lambda i: (0, 0, 0, i)),
                   pl.BlockSpec((hh, g8, 8, tq), lambda i: (0, 0, 0, i)),
                   pl.BlockSpec((hh, g16, 16, tq), lambda i: (0, 0, 0, i)),
                   pl.BlockSpec((hh, g16, 16, tq), lambda i: (0, 0, 0, i))],
        out_shape=[jax.ShapeDtypeStruct((hh, g8, 8, t), F32),
                   jax.ShapeDtypeStruct((hh, g8, 8, t), F32),
                   jax.ShapeDtypeStruct((hh, g16, 16, t), BF16),
                   jax.ShapeDtypeStruct((hh, g16, 16, t), BF16)],
        scratch_shapes=[pltpu.VMEM((nset, PEER_NKEYS, tq), F32),
                        pltpu.VMEM((nset, PEER_TOPK, 8, tq), F32)],
        compiler_params=_cparams(("parallel",)),
        name="peer_select",
    )(q, keys)


PEER_TM = 512
PEER_TE = 1024
PEER_KEY_BLOCK = 32
PEER_ROW_BLOCK = 4


def _gelu(x):
    return 0.5 * x * (1.0 + lax.erf(x * np.float32(math.sqrt(0.5))))


def _peer_dense_kernel(h_ref, u_ref, vt_ref, n1_ref, e1_ref, r2_ref, e2_ref, x_ref, g2_ref, o_ref,
                       acc_ref, at_ref, wa_ref, bcn_ref, bce_ref):
    j = pl.program_id(1)
    nsteps = pl.num_programs(1)
    tm = PEER_TM
    te = PEER_TE
    nk = PEER_NKEYS

    @pl.when(j == 0)
    def _():
        acc_ref[...] = jnp.zeros_like(acc_ref)

    na = te // nk
    kb = PEER_KEY_BLOCK
    kg = kb // 16
    nkq = nk // kb

    at_ref[...] = _dot_nt(u_ref[...], h_ref[...])

    for h, al in itertools.product(range(PEER_HEADS), range(na)):
        bcn_ref[h, al] = jnp.broadcast_to(n1_ref[h, j, al:al + 1, :], (16, tm)).astype(BF16)
        bce_ref[h, al] = jnp.broadcast_to(e1_ref[h, j, al:al + 1, :], (16, tm)).astype(BF16)

    def gate_tile(it, carry):
        lc = it // nkq
        kq = it % nkq
        ls = pl.ds(pl.multiple_of(lc * LANES, LANES), LANES)
        for ab in range(na // PEER_ROW_BLOCK):
            als = range(ab * PEER_ROW_BLOCK, (ab + 1) * PEER_ROW_BLOCK)
            w = {(al, t): jnp.zeros((16, LANES), BF16) for al in als for t in range(kg)}
            for h in range(PEER_HEADS):
                r2 = [r2_ref[h, kq * kg + t, :, ls] for t in range(kg)]
                e2 = [e2_ref[h, kq * kg + t, :, ls] for t in range(kg)]
                for al in als:
                    n1 = bcn_ref[h, al, :, ls]
                    e1 = bce_ref[h, al, :, ls]
                    for t in range(kg):
                        w[al, t] = w[al, t] + jnp.where(r2[t] < n1, e2[t], jnp.zeros_like(e2[t])) * e1
            for al in als:
                rs = pl.ds(pl.multiple_of(al * nk + kq * kb, kb), kb)
                wt = jnp.concatenate([w[al, t] for t in range(kg)], axis=0)
                wa_ref[rs, ls] = (wt.astype(F32) * _gelu(at_ref[rs, ls])).astype(BF16)
        return carry

    lax.fori_loop(0, (tm // LANES) * nkq, gate_tile, 0)

    acc_ref[...] += _dot(vt_ref[0], wa_ref[...])

    @pl.when(j == nsteps - 1)
    def _():
        o_ref[...] = x_ref[...] + g2_ref[0] * acc_ref[...].T


def _peer_dense(hh, u, vt, sel, x, g2, tiles_per_row):
    t, d = hh.shape
    ne = u.shape[0]
    tm, te = PEER_TM, PEER_TE
    one = pl.Buffered(1)
    na = te // PEER_NKEYS
    hd, g8, g16 = PEER_HEADS, PEER_NKEYS // 8, PEER_NKEYS // 16
    sel_map = lambda i, j: (0, 0, 0, i)
    return pl.pallas_call(
        _peer_dense_kernel,
        grid=(t // tm, ne // te),
        in_specs=[pl.BlockSpec((tm, d), lambda i, j: (i, 0), pipeline_mode=one),
                  pl.BlockSpec((te, d), lambda i, j: (j, 0)),
                  pl.BlockSpec((1, d, te), lambda i, j: (j, 0, 0)),
                  pl.BlockSpec((hd, g8, 8, tm), sel_map, pipeline_mode=one),
                  pl.BlockSpec((hd, g8, 8, tm), sel_map, pipeline_mode=one),
                  pl.BlockSpec((hd, g16, 16, tm), sel_map, pipeline_mode=one),
                  pl.BlockSpec((hd, g16, 16, tm), sel_map, pipeline_mode=one),
                  pl.BlockSpec((tm, d), lambda i, j: (i, 0), pipeline_mode=one),
                  pl.BlockSpec((1, 1, d), lambda i, j: (i // tiles_per_row, 0, 0))],
        out_specs=pl.BlockSpec((tm, d), lambda i, j: (i, 0)),
        out_shape=jax.ShapeDtypeStruct((t, d), F32),
        scratch_shapes=[pltpu.VMEM((d, tm), F32),
                        pltpu.VMEM((te, tm), F32),
                        pltpu.VMEM((te, tm), BF16),
                        pltpu.VMEM((hd, na, 16, tm), BF16),
                        pltpu.VMEM((hd, na, 16, tm), BF16)],
        compiler_params=_cparams(("parallel", "arbitrary")),
        name="peer_dense",
    )(hh, u, vt, *sel, x, g2)


def _peer_block(x, g, shift, scale, gate2, wq, keys, u, vt):
    b, l, d = x.shape
    t = b * l
    x_rows = x if shift.shape[0] > 1 else x.reshape(1, t, d)
    q, hh = _norm_mod_matmul(x_rows, g, shift, scale, wq, BF16, 512, emit_h=True)
    sel = _peer_select(q.reshape(t, -1), keys)
    bm = gate2.shape[0]
    tiles_per_row = (l // PEER_TM) if bm > 1 else (t // PEER_TM)
    out = _peer_dense(hh.reshape(t, d), u, vt, sel, x.reshape(t, d), gate2.reshape(bm, 1, d), tiles_per_row)
    return out.reshape(b, l, d)


def _rmsnorm_kernel(x_ref, g_ref, o_ref):
    x = x_ref[...]
    o_ref[...] = x * lax.rsqrt(jnp.mean(x * x, axis=-1, keepdims=True) + EPS) * g_ref[...]


def _rmsnorm(x, g):
    t, d = x.shape
    tm = 512
    return pl.pallas_call(
        _rmsnorm_kernel,
        grid=(t // tm,),
        in_specs=[pl.BlockSpec((tm, d), lambda i: (i, 0)), pl.BlockSpec((1, d), lambda i: (0, 0))],
        out_specs=pl.BlockSpec((tm, d), lambda i: (i, 0)),
        out_shape=jax.ShapeDtypeStruct((t, d), F32),
        compiler_params=_cparams(("parallel",)),
        name="final_rmsnorm",
    )(x, g.reshape(1, d))


def _pack_w_in(w_in):
    q, k, v, z, xbc, dt, glu, gate = jnp.split(
        w_in, np.cumsum([NA_WIDTH, NA_WIDTH, NA_WIDTH, SSM_INNER, SSM_XBC, 2 * SSM_HEADS, 2 * CONV_CH]).tolist(),
        axis=-1)
    packed = jnp.concatenate([gate, glu, q, k, v, z, xbc], axis=-1).astype(BF16)
    dt_w = jnp.pad(dt, ((0, 0), (0, LANES - 2 * SSM_HEADS))).astype(BF16)
    return packed, dt_w


def _trunk_layer(xl, xc, c_rows, need_ctx, w_mod, b_mod, norm1_g, norm2_g, w_in, na_rpb, na_wo,
                 ssm_conv_w, ssm_conv_b, ssm_dt_bias, ssm_a_log, ssm_d, ssm_norm_g, ssm_wo,
                 cv_dw_w, cv_dw_b, cv_ln_g, cv_ln_b, cv_wo, cv_bo, w_out, peer_wq, peer_keys, peer_u, peer_v):
    b, s, d = xl.shape
    mod = _modulation(c_rows, w_mod, b_mod)
    sh1, sc1, g1, sh2, sc2, g2 = [mod[:b, i * d:(i + 1) * d] for i in range(6)]
    csh1, csc1, cg1, csh2, csc2, cg2 = [mod[b:b + 1, i * d:(i + 1) * d] for i in range(6)]

    w_pack, w_dt = _pack_w_in(w_in)
    dt_bias = jnp.pad(ssm_dt_bias.astype(F32).reshape(-1), (0, LANES - 2 * SSM_HEADS))

    cb, cl, _ = xc.shape
    xc_flat = xc.reshape(1, cb * cl, d)
    p_l = _norm_mod_matmul(xl, norm1_g, sh1, sc1, w_pack, BF16, 512)
    p_c = _norm_mod_matmul(xc_flat, norm1_g, csh1, csc1, w_pack, BF16, 512).reshape(cb, cl, -1)
    dt_l = _norm_mod_matmul(xl, norm1_g, sh1, sc1, w_dt, F32, LANES, softplus_bias=dt_bias)
    dt_c = _norm_mod_matmul(xc_flat, norm1_g, csh1, csc1, w_dt, F32, LANES,
                            softplus_bias=dt_bias).reshape(cb, cl, -1)

    bias_tab = _na_bias_table(na_rpb, s // GRID_W)
    y_att, y_att_c = _neighbourhood_attention(p_l, p_c, bias_tab, need_ctx)

    xbc_l = _dwconv(p_l, OFF_XBC, None, SSM_XBC, ssm_conv_w, ssm_conv_b, None, None, F32)
    xbc_c = _dwconv(p_c, OFF_XBC, None, SSM_XBC, ssm_conv_w, ssm_conv_b, None, None, F32)
    y_ssm, y_ssm_c = _ssd_scan(xbc_l, xbc_c, dt_l, dt_c, ssm_a_log, ssm_d)

    wa, ws, wc, wo = (na_wo.astype(BF16), ssm_wo.astype(BF16), cv_wo.astype(BF16), w_out.astype(BF16))
    y_cv = _dwconv(p_l, OFF_GLU_A, OFF_GLU_G, CONV_CH, cv_dw_w, cv_dw_b, cv_ln_g, cv_ln_b, BF16)
    xl = _merge(xl, g1, p_l, y_att, y_ssm, y_cv, ssm_norm_g, wa, ws, wc, cv_bo, wo)
    if need_ctx:
        y_cv_c = _dwconv(p_c, OFF_GLU_A, OFF_GLU_G, CONV_CH, cv_dw_w, cv_dw_b, cv_ln_g, cv_ln_b, BF16)
        xc = _merge(xc, cg1, p_c, y_att_c, y_ssm_c, y_cv_c, ssm_norm_g, wa, ws, wc, cv_bo, wo)

    wq = peer_wq.astype(BF16)
    keys = peer_keys.astype(BF16).reshape(2 * PEER_HEADS, PEER_NKEYS, PEER_KEY_DIM)
    u = peer_u.astype(BF16)
    vt = peer_v.astype(BF16).reshape(PEER_EXPERTS // PEER_TE, PEER_TE, d).transpose(0, 2, 1)
    xl = _peer_block(xl, norm2_g, sh2, sc2, g2, wq, keys, u, vt)
    if need_ctx:
        xc = _peer_block(xc, norm2_g, csh2, csc2, cg2, wq, keys, u, vt)
    return xl, xc


def kernel(x, c, ctx, c_ctx, w_mod, b_mod, norm1_g, norm2_g, w_in, na_rpb, na_wo, ssm_conv_w, ssm_conv_b,
           ssm_dt_bias, ssm_A_log, ssm_D, ssm_norm_g, ssm_wo, cv_dw_w, cv_dw_b, cv_ln_g, cv_ln_b, cv_wo,
           cv_bo, w_out, peer_wq, peer_keys, peer_u, peer_v, final_norm_g):
    b, s, d = x.shape
    depth = w_mod.shape[0]
    rows = -(-(b + 1) // 8) * 8
    c_rows = jnp.zeros((rows, d), F32).at[:b].set(c).at[b].set(c_ctx)
    xl, xc = x, ctx
    for l in range(depth):
        xl, xc = _trunk_layer(xl, xc, c_rows, l < depth - 1, w_mod[l], b_mod[l], norm1_g[l], norm2_g[l],
                              w_in[l], na_rpb[l], na_wo[l], ssm_conv_w[l], ssm_conv_b[l], ssm_dt_bias[l],
                              ssm_A_log[l], ssm_D[l], ssm_norm_g[l], ssm_wo[l], cv_dw_w[l], cv_dw_b[l],
                              cv_ln_g[l], cv_ln_b[l], cv_wo[l], cv_bo[l], w_out[l],
                              peer_wq[l], peer_keys[l], peer_u[l], peer_v[l])
    return _rmsnorm(xl.reshape(b * s, d), final_norm_g).reshape(b, s, d)
```

```python
import functools
import itertools
import math

import numpy as np
import jax
import jax.numpy as jnp
from jax import lax
from jax.experimental import pallas as pl
from jax.experimental.pallas import tpu as pltpu

F32 = jnp.float32
BF16 = jnp.bfloat16
EPS = 1e-6

D_MODEL = 2048
GRID_W = 64
NA_HEADS = 16
NA_HEAD_DIM = 64
NA_WIDTH = NA_HEADS * NA_HEAD_DIM
WIN_R = 8
WIN_C = 16
SSM_HEADS = 16
SSM_HEAD_DIM = 64
SSM_INNER = SSM_HEADS * SSM_HEAD_DIM
SSM_GROUPS = 2
SSM_STATE = 128
SSM_XBC = SSM_INNER + 2 * SSM_GROUPS * SSM_STATE
SSM_CONV = 5
SSM_CHUNK = 128
CONV_CH = 1024
CONV_K = 31
PEER_HEADS = 8
PEER_NKEYS = 128
PEER_KEY_DIM = 128
PEER_TOPK = 16
PEER_EXPERTS = PEER_NKEYS * PEER_NKEYS

OFF_GATE = 0
OFF_GLU_A = 6144
OFF_GLU_G = 7168
OFF_Q = 8192
OFF_K = 9216
OFF_V = 10240
OFF_Z = 11264
OFF_XBC = 12288
PACK_COLS = 13824

LANES = 128
VMEM_LIMIT = 56 * 1024 * 1024


def _cparams(sem):
    return pltpu.CompilerParams(dimension_semantics=sem, vmem_limit_bytes=VMEM_LIMIT)


def _dot(a, b):
    return jnp.dot(a, b, preferred_element_type=F32)


def _dot_nt(a, b):
    return lax.dot_general(a, b, (((1,), (1,)), ((), ())), preferred_element_type=F32)


def _dot_tn(a, b):
    return lax.dot_general(a, b, (((0,), (0,)), ((), ())), preferred_element_type=F32)


def _split3(a):
    hi = a.astype(BF16)
    r1 = a - hi.astype(F32)
    mid = r1.astype(BF16)
    lo = (r1 - mid.astype(F32)).astype(BF16)
    return hi, mid, lo


def _dot_exact_rhs01(a, m01):
    hi, mid, lo = _split3(a)
    return _dot(hi, m01) + _dot(mid, m01) + _dot(lo, m01)


def _dot_exact_lhs01(m01, a):
    hi, mid, lo = _split3(a)
    return _dot(m01, hi) + _dot(m01, mid) + _dot(m01, lo)


def _sigmoid(x):
    return 1.0 / (1.0 + jnp.exp(-x))


def _silu(x):
    return x * _sigmoid(x)


def _mod_kernel(c_ref, w_ref, b_ref, o_ref):
    a = _silu(c_ref[...]).astype(BF16)
    o_ref[...] = _dot(a, w_ref[...].astype(BF16)) + b_ref[...]


def _modulation(cs, w_mod, b_mod):
    r, d = cs.shape
    n = w_mod.shape[1]
    tn = 1024
    return pl.pallas_call(
        _mod_kernel,
        grid=(n // tn,),
        in_specs=[pl.BlockSpec((r, d), lambda j: (0, 0)),
                  pl.BlockSpec((d, tn), lambda j: (0, j)),
                  pl.BlockSpec((1, tn), lambda j: (0, j))],
        out_specs=pl.BlockSpec((r, tn), lambda j: (0, j)),
        out_shape=jax.ShapeDtypeStruct((r, n), F32),
        compiler_params=_cparams(("arbitrary",)),
        name="modulation",
    )(cs, w_mod, b_mod.reshape(1, n))


def _nmm_kernel(x_ref, g_ref, sh_ref, sc_ref, w_ref, *rest, softplus_bias, emit_h):
    if softplus_bias:
        bias_ref, rest = rest[0], rest[1:]
    if emit_h:
        o_ref, ho_ref, h_ref = rest
    else:
        o_ref, h_ref = rest

    @pl.when(pl.program_id(2) == 0)
    def _():
        x = x_ref[0]
        ms = jnp.mean(x * x, axis=-1, keepdims=True)
        y = x * lax.rsqrt(ms + EPS) * g_ref[...]
        h = (y * (1.0 + sc_ref[0]) + sh_ref[0]).astype(BF16)
        h_ref[...] = h
        if emit_h:
            ho_ref[0] = h

    acc = _dot(h_ref[...], w_ref[0])
    if softplus_bias:
        t = acc + bias_ref[...]
        acc = jnp.maximum(t, 0.0) + jnp.log1p(jnp.exp(-jnp.abs(t)))
    o_ref[0] = acc.astype(o_ref.dtype)


def _column_blocks(w, tn):
    d, n = w.shape
    return w.reshape(d, n // tn, tn).transpose(1, 0, 2)


def _norm_mod_matmul(x, g, shift, scale, w, out_dtype, tn, softplus_bias=None, emit_h=False):
    b, l, d = x.shape
    n = w.shape[1]
    w = _column_blocks(w, tn)
    tm = min(l, 1024)
    bm = shift.shape[0]
    mod_map = (lambda i, m, j: (i, 0, 0)) if bm > 1 else (lambda i, m, j: (0, 0, 0))
    in_specs = [pl.BlockSpec((1, tm, d), lambda i, m, j: (i, m, 0)),
                pl.BlockSpec((1, d), lambda i, m, j: (0, 0)),
                pl.BlockSpec((1, 1, d), mod_map),
                pl.BlockSpec((1, 1, d), mod_map),
                pl.BlockSpec((1, d, tn), lambda i, m, j: (j, 0, 0))]
    args = [x, g.reshape(1, d), shift.reshape(bm, 1, d), scale.reshape(bm, 1, d), w]
    if softplus_bias is not None:
        in_specs.append(pl.BlockSpec((1, tn), lambda i, m, j: (0, j)))
        args.append(softplus_bias.reshape(1, n))
    out_shape = [jax.ShapeDtypeStruct((b, l, n), out_dtype)]
    out_specs = [pl.BlockSpec((1, tm, tn), lambda i, m, j: (i, m, j))]
    if emit_h:
        out_shape.append(jax.ShapeDtypeStruct((b, l, d), BF16))
        out_specs.append(pl.BlockSpec((1, tm, d), lambda i, m, j: (i, m, 0)))
    res = pl.pallas_call(
        functools.partial(_nmm_kernel, softplus_bias=softplus_bias is not None, emit_h=emit_h),
        grid=(b, l // tm, n // tn),
        in_specs=in_specs,
        out_specs=out_specs,
        out_shape=out_shape,
        scratch_shapes=[pltpu.VMEM((tm, d), BF16)],
        compiler_params=_cparams(("parallel", "parallel", "arbitrary")),
        name="norm_mod_matmul",
    )(*args)
    return res if emit_h else res[0]


def _na_bias_table(rpb, rows):
    wr = min(WIN_R, rows)
    j = np.arange(GRID_W)
    cstart = np.clip(j - WIN_C // 2, 0, GRID_W - WIN_C)
    kc = np.arange(GRID_W)
    mask = (kc[None, :] >= cstart[:, None]) & (kc[None, :] < cstart[:, None] + WIN_C)
    col_off = np.clip(kc[None, :] - j[:, None], -(WIN_C - 1), WIN_C - 1) + WIN_C - 1
    dd = np.arange(wr)[:, None]
    ww = np.arange(wr)[None, :]
    row_idx = ww - dd + WIN_R - 1
    t = rpb.astype(F32)[:, row_idx][:, :, :, col_off]
    t = jnp.where(mask[None, None, None], t, -1e30)
    t = t.transpose(0, 1, 3, 2, 4).reshape(NA_HEADS, wr, GRID_W, wr * GRID_W)
    return t.reshape(NA_HEADS // 2, 2, wr, GRID_W, wr * GRID_W)


NA_ROW_GROUP = 8


def _na_kernel(q_ref, k_ref, v_ref, kc_ref, vc_ref, bias_ref, *rest, rows, wr, with_ctx):
    if with_ctx:
        qc_ref, o_ref, oc_ref = rest
    else:
        (o_ref,) = rest
    lane = lax.broadcasted_iota(jnp.int32, (1, LANES), 1)
    lo = lane < NA_HEAD_DIM
    kc = kc_ref[0]
    vc = vc_ref[0]
    scale = NA_HEAD_DIM ** -0.5
    nwin = wr * GRID_W

    def attend(problems):
        chains = []
        for q, kw, vw, bias_of in problems:
            for h in range(2):
                sel = lo if h == 0 else jnp.logical_not(lo)
                qh = jnp.where(sel, q, jnp.zeros_like(q))
                s_c = _dot_nt(qh, kc) * scale
                s_w = None if kw is None else _dot_nt(qh, kw) * scale + bias_of(h)
                chains.append((s_c, s_w, vw))
        maxes = []
        for s_c, s_w, _ in chains:
            m = jnp.max(s_c, axis=-1, keepdims=True)
            if s_w is not None:
                m = jnp.maximum(m, jnp.max(s_w, axis=-1, keepdims=True))
            maxes.append(m)
        probs = []
        for (s_c, s_w, _), m in zip(chains, maxes):
            probs.append((jnp.exp(s_c - m), None if s_w is None else jnp.exp(s_w - m)))
        outs = []
        for (_, _, vw), (p_c, p_w) in zip(chains, probs):
            den = jnp.sum(p_c, axis=-1, keepdims=True)
            o = _dot(p_c.astype(BF16), vc)
            if p_w is not None:
                den = den + jnp.sum(p_w, axis=-1, keepdims=True)
                o = o + _dot(p_w.astype(BF16), vw)
            outs.append(o / den)
        return [jnp.where(lo, outs[2 * i], outs[2 * i + 1]) for i in range(len(problems))]

    def body(g, carry):
        problems, q0s = [], []
        for rr in range(NA_ROW_GROUP):
            r = g * NA_ROW_GROUP + rr
            rs = jnp.clip(r - wr // 2, 0, rows - wr)
            d = r - rs
            q0 = pl.multiple_of(r * GRID_W, GRID_W)
            k0 = pl.multiple_of(rs * GRID_W, GRID_W)
            problems.append((q_ref[0, pl.ds(q0, GRID_W), :], k_ref[0, pl.ds(k0, nwin), :],
                             v_ref[0, pl.ds(k0, nwin), :], lambda h, d=d: bias_ref[0, h, d]))
            q0s.append(q0)
        for q0, o in zip(q0s, attend(problems)):
            o_ref[0, pl.ds(q0, GRID_W), :] = o.astype(o_ref.dtype)
        return carry

    lax.fori_loop(0, rows // NA_ROW_GROUP, body, 0)
    if with_ctx:
        oc_ref[0] = attend([(qc_ref[0], None, None, None)])[0].astype(oc_ref.dtype)


def _neighbourhood_attention(pl_lat, pl_ctx, bias_tab, with_ctx):
    b, s, _ = pl_lat.shape
    cl = pl_ctx.shape[1]
    rows = s // GRID_W
    wr = min(WIN_R, rows)
    nhp = NA_HEADS // 2
    qb, kb, vb = OFF_Q // LANES, OFF_K // LANES, OFF_V // LANES
    in_specs = [pl.BlockSpec((1, s, LANES), lambda hp, i: (i, 0, qb + hp)),
                pl.BlockSpec((1, s, LANES), lambda hp, i: (i, 0, kb + hp)),
                pl.BlockSpec((1, s, LANES), lambda hp, i: (i, 0, vb + hp)),
                pl.BlockSpec((1, cl, LANES), lambda hp, i: (i, 0, kb + hp)),
                pl.BlockSpec((1, cl, LANES), lambda hp, i: (i, 0, vb + hp)),
                pl.BlockSpec((1, 2, wr, GRID_W, wr * GRID_W), lambda hp, i: (hp, 0, 0, 0, 0))]
    args = [pl_lat, pl_lat, pl_lat, pl_ctx, pl_ctx, bias_tab]
    out_shape = [jax.ShapeDtypeStruct((b, s, NA_WIDTH), BF16)]
    out_specs = [pl.BlockSpec((1, s, LANES), lambda hp, i: (i, 0, hp))]
    if with_ctx:
        in_specs.append(pl.BlockSpec((1, cl, LANES), lambda hp, i: (i, 0, qb + hp)))
        args.append(pl_ctx)
        out_shape.append(jax.ShapeDtypeStruct((b, cl, NA_WIDTH), BF16))
        out_specs.append(pl.BlockSpec((1, cl, LANES), lambda hp, i: (i, 0, hp)))
    res = pl.pallas_call(
        functools.partial(_na_kernel, rows=rows, wr=wr, with_ctx=with_ctx),
        grid=(nhp, b),
        in_specs=in_specs,
        out_specs=out_specs,
        out_shape=out_shape,
        compiler_params=_cparams(("parallel", "parallel")),
        name="neighbourhood_attention",
    )(*args)
    return (res[0], res[1]) if with_ctx else (res[0], None)


CONV_TL = 256
CONV_HALO = 16


def _dwconv_kernel(*refs, taps, ch, glu, tl):
    if glu:
        (a_p, a_c, a_n, g_p, g_c, g_n, w_ref, b_ref, lng_ref, lnb_ref, o_ref, u_ref, y_ref) = refs
    else:
        (a_p, a_c, a_n, w_ref, b_ref, o_ref, u_ref) = refs
    i = pl.program_id(1)
    n = pl.num_programs(1)
    pad = (taps - 1) // 2
    hl = CONV_HALO

    def pre(a, g):
        a = a.astype(F32)
        if glu:
            return a * _sigmoid(g.astype(F32))
        return a

    u_ref[hl:hl + tl, :] = pre(a_c[0], g_c[0] if glu else None)
    top = pre(a_p[0, tl - hl:tl, :], g_p[0, tl - hl:tl, :] if glu else None)
    u_ref[0:hl, :] = jnp.where(i > 0, top, 0.0)
    bot = pre(a_n[0, 0:hl, :], g_n[0, 0:hl, :] if glu else None)
    u_ref[hl + tl:hl + tl + hl, :] = jnp.where(i < n - 1, bot, 0.0)

    for cc in range(ch // LANES):
        cs = slice(cc * LANES, (cc + 1) * LANES)
        acc = jnp.zeros((tl, LANES), F32) + b_ref[:, cs]
        for k in range(taps):
            off = hl - pad + k
            acc = acc + w_ref[k:k + 1, cs] * u_ref[off:off + tl, cs]
        if glu:
            y_ref[:, cs] = acc
        else:
            o_ref[0, :, cs] = _silu(acc).astype(o_ref.dtype)

    if glu:
        y = y_ref[...]
        mu = jnp.mean(y, axis=-1, keepdims=True)
        yc = y - mu
        var = jnp.mean(yc * yc, axis=-1, keepdims=True)
        z = yc * lax.rsqrt(var + EPS) * lng_ref[...] + lnb_ref[...]
        o_ref[0] = _silu(z).astype(o_ref.dtype)


def _dwconv(src, off_a, off_g, ch, w, b, ln_g, ln_b, out_dtype):
    bsz, l, _ = src.shape
    taps = w.shape[0]
    glu = off_g is not None
    tl = CONV_TL
    nt = l // tl
    ca = off_a // ch

    def spec(cb, delta):
        def imap(i, t):
            return (i, jnp.clip(t + delta, 0, nt - 1), cb)
        return pl.BlockSpec((1, tl, ch), imap)

    in_specs = [spec(ca, -1), spec(ca, 0), spec(ca, 1)]
    args = [src, src, src]
    if glu:
        cg = off_g // ch
        in_specs += [spec(cg, -1), spec(cg, 0), spec(cg, 1)]
        args += [src, src, src]
    in_specs += [pl.BlockSpec((taps, ch), lambda i, t: (0, 0)), pl.BlockSpec((1, ch), lambda i, t: (0, 0))]
    args += [w, b.reshape(1, ch)]
    scratch = [pltpu.VMEM((tl + 2 * CONV_HALO, ch), F32)]
    if glu:
        in_specs += [pl.BlockSpec((1, ch), lambda i, t: (0, 0)), pl.BlockSpec((1, ch), lambda i, t: (0, 0))]
        args += [ln_g.reshape(1, ch), ln_b.reshape(1, ch)]
        scratch.append(pltpu.VMEM((tl, ch), F32))
    return pl.pallas_call(
        functools.partial(_dwconv_kernel, taps=taps, ch=ch, glu=glu, tl=tl),
        grid=(bsz, nt),
        in_specs=in_specs,
        out_specs=pl.BlockSpec((1, tl, ch), lambda i, t: (i, t, 0)),
        out_shape=jax.ShapeDtypeStruct((bsz, l, ch), out_dtype),
        scratch_shapes=scratch,
        compiler_params=_cparams(("parallel", "parallel")),
        name="dwconv_glu" if glu else "dwconv_ssm",
    )(*args)


def _run_lockstep(coroutines):
    out = [None] * len(coroutines)
    live = list(range(len(coroutines)))
    while live:
        for i in list(live):
            try:
                next(coroutines[i])
            except StopIteration as done:
                out[i] = done.value
                live.remove(i)
    return out


def _ssd_kernel(xlf_ref, xcf_ref, dtlf_ref, dtcf_ref, dttlf_ref, dttcf_ref,
                xlb_ref, xcb_ref, dtlb_ref, dtcb_ref, dttlb_ref, dttcb_ref,
                alog_ref, alogt_ref, dskip_ref, exp_ref,
                ylf_ref, ycf_ref, ylb_ref, ycb_ref, h_ref, *, ncc):
    c = pl.program_id(1)
    q = SSM_CHUNK
    hh = SSM_HEADS
    hpg = SSM_HEADS // SSM_GROUPS
    gw = hpg * SSM_HEAD_DIM
    n = SSM_STATE
    is_ctx = c < ncc

    @pl.when(c == 0)
    def _():
        h_ref[...] = jnp.zeros_like(h_ref)

    li = lax.broadcasted_iota(jnp.int32, (q, q), 0)
    si = lax.broadcasted_iota(jnp.int32, (q, q), 1)
    lane = lax.broadcasted_iota(jnp.int32, (1, LANES), 1)
    lo = lane < SSM_HEAD_DIM
    expand = exp_ref[...]
    alog = alog_ref[...]
    alogt = alogt_ref[...]

    def direction(d, xl_ref, xc_ref, dtl_ref, dtc_ref, dttl_ref, dttc_ref):
        fwd = d == 0
        xbc = jnp.where(is_ctx, xc_ref[0], xl_ref[0]).astype(F32)
        x = xbc[:, :SSM_INNER]
        dt2 = jnp.where(is_ctx, dtc_ref[0], dtl_ref[0])
        dt_col = dt2[:, d * hh:(d + 1) * hh]
        dtt2 = jnp.where(is_ctx, dttc_ref[0], dttl_ref[0])
        dt_row = dtt2[d * hh:(d + 1) * hh, :]
        a_col = dt_col * -jnp.exp(alog[d:d + 1, :])
        a_row = dt_row * -jnp.exp(alogt[:, d:d + 1])

        tri = (si <= li) if fwd else (si >= li)
        trit = (li <= si) if fwd else (li >= si)
        tri_b = jnp.where(tri, 1.0, 0.0).astype(BF16)
        trit_b = jnp.where(trit, 1.0, 0.0).astype(BF16)
        e_col = _dot_exact_lhs01(tri_b, a_col)
        e_row = _dot_exact_rhs01(a_row, trit_b)
        last = q - 1 if fwd else 0
        e_tot = e_col[last:last + 1, :]
        yield

        w_dt = _dot_exact_rhs01(dt_col, expand)
        w_dec = _dot_exact_rhs01(jnp.exp(e_tot - e_col), expand)
        w_off = _dot_exact_rhs01(jnp.exp(e_col), expand)
        w_tot = w_off[last:last + 1, :]

        xdt = x * w_dt
        xdt_b = xdt.astype(BF16)
        xdec_b = (xdt * w_dec).astype(BF16)
        yield

        y_parts = []
        for g in range(SSM_GROUPS):
            bm = xbc[:, SSM_INNER + g * n:SSM_INNER + (g + 1) * n].astype(BF16)
            cm = xbc[:, SSM_INNER + (SSM_GROUPS + g) * n:SSM_INNER + (SSM_GROUPS + g + 1) * n].astype(BF16)
            cb = _dot_nt(cm, bm)
            hg = h_ref[d, g]
            y_off = _dot(cm, hg.astype(BF16))
            gs = slice(g * gw, (g + 1) * gw)
            h_ref[d, g] = hg * w_tot[:, gs] + _dot_tn(bm, xdec_b[:, gs])
            yield
            for pr in range(hpg // 2):
                h0 = g * hpg + 2 * pr
                cs = slice(h0 * SSM_HEAD_DIM, (h0 + 2) * SSM_HEAD_DIM)
                xp = xdt_b[:, cs]
                acc = None
                for k in range(2):
                    h = h0 + k
                    lm = jnp.where(tri, jnp.exp(e_col[:, h:h + 1] - e_row[h:h + 1, :]), 0.0)
                    mh = (cb * lm).astype(BF16)
                    sel = lo if k == 0 else jnp.logical_not(lo)
                    t = _dot(mh, jnp.where(sel, xp, jnp.zeros_like(xp)))
                    acc = t if acc is None else acc + t
                ys = slice(2 * pr * SSM_HEAD_DIM, (2 * pr + 2) * SSM_HEAD_DIM)
                y_parts.append(acc + y_off[:, ys] * w_off[:, cs])
                yield
        y = jnp.concatenate(y_parts, axis=-1)
        return y + dskip_ref[...] * x if fwd else y

    y_f, y_b = _run_lockstep([direction(0, xlf_ref, xcf_ref, dtlf_ref, dtcf_ref, dttlf_ref, dttcf_ref),
                              direction(1, xlb_ref, xcb_ref, dtlb_ref, dtcb_ref, dttlb_ref, dttcb_ref)])

    @pl.when(is_ctx)
    def _():
        ycf_ref[0] = y_f
        ycb_ref[0] = y_b

    @pl.when(jnp.logical_not(is_ctx))
    def _():
        ylf_ref[0] = y_f
        ylb_ref[0] = y_b


def _ssd_scan(xbc_l, xbc_c, dt_l, dt_c, a_log, d_skip):
    b, l, _ = xbc_l.shape
    cl = xbc_c.shape[1]
    q = SSM_CHUNK
    ncl, ncc = l // q, cl // q
    nc = ncl + ncc
    hh = SSM_HEADS
    dtt_l = jnp.swapaxes(dt_l[..., :2 * hh], 1, 2)
    dtt_c = jnp.swapaxes(dt_c[..., :2 * hh], 1, 2)
    expand = jnp.asarray(np.kron(np.eye(hh), np.ones((1, SSM_HEAD_DIM))), BF16)
    dskip = jnp.repeat(d_skip.astype(F32), SSM_HEAD_DIM).reshape(1, SSM_INNER)

    def lat_chunk(d, c):
        cc = jnp.maximum(c - ncc, 0)
        return cc if d == 0 else ncl - 1 - cc

    def ctx_chunk(d, c):
        cc = jnp.minimum(c, ncc - 1)
        return cc if d == 0 else ncc - 1 - cc

    def dir_specs(d):
        return [pl.BlockSpec((1, q, SSM_XBC), lambda i, c: (i, lat_chunk(d, c), 0)),
                pl.BlockSpec((1, q, SSM_XBC), lambda i, c: (i, ctx_chunk(d, c), 0)),
                pl.BlockSpec((1, q, LANES), lambda i, c: (i, lat_chunk(d, c), 0)),
                pl.BlockSpec((1, q, LANES), lambda i, c: (i, ctx_chunk(d, c), 0)),
                pl.BlockSpec((1, 2 * hh, q), lambda i, c: (i, 0, lat_chunk(d, c))),
                pl.BlockSpec((1, 2 * hh, q), lambda i, c: (i, 0, ctx_chunk(d, c)))]

    const = lambda i, c: (0, 0)
    in_specs = dir_specs(0) + dir_specs(1) + [
        pl.BlockSpec((2, hh), const), pl.BlockSpec((hh, 2), const),
        pl.BlockSpec((1, SSM_INNER), const), pl.BlockSpec((hh, SSM_INNER), const)]
    out_specs, out_shape = [], []
    for d in range(2):
        out_specs += [pl.BlockSpec((1, q, SSM_INNER), lambda i, c, d=d: (i, lat_chunk(d, c), 0)),
                      pl.BlockSpec((1, q, SSM_INNER), lambda i, c, d=d: (i, ctx_chunk(d, c), 0))]
        out_shape += [jax.ShapeDtypeStruct((b, l, SSM_INNER), F32), jax.ShapeDtypeStruct((b, cl, SSM_INNER), F32)]
    gw = (SSM_HEADS // SSM_GROUPS) * SSM_HEAD_DIM
    data = (xbc_l, xbc_c, dt_l, dt_c, dtt_l, dtt_c)
    yl_f, yc_f, yl_b, yc_b = pl.pallas_call(
        functools.partial(_ssd_kernel, ncc=ncc),
        grid=(b, nc),
        in_specs=in_specs,
        out_specs=out_specs,
        out_shape=out_shape,
        scratch_shapes=[pltpu.VMEM((2, SSM_GROUPS, SSM_STATE, gw), F32)],
        compiler_params=_cparams(("parallel", "arbitrary")),
        name="ssd_scan",
    )(*data, *data, a_log.astype(F32), a_log.astype(F32).T, dskip, expand)
    return (yl_f, yl_b), (yc_f, yc_b)


def _merge_kernel(x_ref, g1_ref, gate_ref, att_ref, ysf_ref, ysb_ref, z_ref, cv_ref, ng_ref,
                  wa_ref, ws_ref, wc_ref, bc_ref, wo_ref, o_ref):
    d = D_MODEL
    ys = (ysf_ref[0] + ysb_ref[0]) * _silu(z_ref[0].astype(F32))
    gsz = SSM_INNER // SSM_GROUPS
    parts = []
    for g in range(SSM_GROUPS):
        yg = ys[:, g * gsz:(g + 1) * gsz]
        parts.append(yg * lax.rsqrt(jnp.mean(yg * yg, axis=-1, keepdims=True) + EPS))
    yn = (jnp.concatenate(parts, axis=-1) * ng_ref[...]).astype(BF16)
    p_s = _dot(yn, ws_ref[...])
    p_a = _dot(att_ref[0], wa_ref[...])
    p_c = _dot(cv_ref[0], wc_ref[...]) + bc_ref[...]
    gl = gate_ref[0].astype(F32)
    m = (_sigmoid(gl[:, 0:d]) * p_a + _sigmoid(gl[:, d:2 * d]) * p_s
         + _sigmoid(gl[:, 2 * d:3 * d]) * p_c).astype(BF16)
    o_ref[0] = x_ref[0] + g1_ref[0] * _dot(m, wo_ref[...])


def _merge(x, g1, packed, y_att, y_ssm2, y_cv, norm_g, wa, ws, wc, bc, wo):
    b, l, d = x.shape
    tm = 256
    bm = g1.shape[0]
    g_map = (lambda i, t: (i, 0, 0)) if bm > 1 else (lambda i, t: (0, 0, 0))
    const = lambda i, t: (0, 0)
    one = pl.Buffered(1)
    in_specs = [pl.BlockSpec((1, tm, d), lambda i, t: (i, t, 0)),
                pl.BlockSpec((1, 1, d), g_map),
                pl.BlockSpec((1, tm, 3 * d), lambda i, t: (i, t, OFF_GATE // (3 * d))),
                pl.BlockSpec((1, tm, NA_WIDTH), lambda i, t: (i, t, 0)),
                pl.BlockSpec((1, tm, SSM_INNER), lambda i, t: (i, t, 0)),
                pl.BlockSpec((1, tm, SSM_INNER), lambda i, t: (i, t, 0)),
                pl.BlockSpec((1, tm, SSM_INNER), lambda i, t: (i, t, OFF_Z // SSM_INNER)),
                pl.BlockSpec((1, tm, CONV_CH), lambda i, t: (i, t, 0)),
                pl.BlockSpec((1, SSM_INNER), const),
                pl.BlockSpec((NA_WIDTH, d), const, pipeline_mode=one),
                pl.BlockSpec((SSM_INNER, d), const, pipeline_mode=one),
                pl.BlockSpec((CONV_CH, d), const, pipeline_mode=one),
                pl.BlockSpec((1, d), const),
                pl.BlockSpec((d, d), const, pipeline_mode=one)]
    return pl.pallas_call(
        _merge_kernel,
        grid=(b, l // tm),
        in_specs=in_specs,
        out_specs=pl.BlockSpec((1, tm, d), lambda i, t: (i, t, 0)),
        out_shape=jax.ShapeDtypeStruct((b, l, d), F32),
        compiler_params=_cparams(("parallel", "parallel")),
        name="merge",
    )(x, g1.reshape(bm, 1, d), packed, y_att, y_ssm2[0], y_ssm2[1], packed, y_cv, norm_g.reshape(1, -1),
      wa, ws, wc, bc.reshape(1, d), wo)


PEER_TQ = 256
NEG_INF = float("-inf")


def _sort_network(n):
    pairs = []

    def merge(lo, hi, r):
        step = r * 2
        if step < hi - lo:
            merge(lo, hi, step)
            merge(lo + r, hi, step)
            pairs.extend((i, i + r) for i in range(lo + r, hi - r, step))
        else:
            pairs.append((lo, lo + r))

    def sort(lo, hi):
        if hi - lo >= 1:
            mid = lo + (hi - lo) // 2
            sort(lo, mid)
            sort(mid + 1, hi)
            merge(lo, hi, 1)

    sort(0, n - 1)
    return pairs


def _exchange(v, i, j):
    a, b = v[i], v[j]
    v[i], v[j] = jnp.maximum(a, b), jnp.minimum(a, b)


def _top_k_sorted_many(problems):
    k = len(problems[0])
    vs = [list(v) for v in problems]
    for i, j in _sort_network(k):
        for v in vs:
            _exchange(v, i, j)
    for shift in (4, 2, 1):
        others = [[pltpu.roll(x, shift, axis=0) for x in v] for v in vs]
        vs = [[jnp.maximum(v[i], o[k - 1 - i]) for i in range(k)]
              for v, o in zip(vs, others)]
        dist = k // 2
        while dist >= 1:
            for i in range(k):
                if i & dist == 0:
                    for v in vs:
                        _exchange(v, i, i + dist)
            dist //= 2
    return vs


def _top_k_sorted(v):
    return _top_k_sorted_many([v])[0]


def _peer_select_kernel(q_ref, keys_ref, n1_ref, e1_ref, r2_ref, e2_ref, s_ref, top_ref):
    tq = PEER_TQ
    k = PEER_TOPK
    sub = lax.broadcasted_iota(jnp.int32, (8, tq), 0)

    def per_head_sets(h, carry):
        scores = []
        for i in (2 * h, 2 * h + 1):
            c0 = pl.multiple_of(i * PEER_KEY_DIM, PEER_KEY_DIM)
            s = _dot_nt(keys_ref[i], q_ref[:, pl.ds(c0, PEER_KEY_DIM)])
            s_ref[i] = s
            scores.append([s[8 * r:8 * r + 8, :] for r in range(PEER_NKEYS // 8)])
        tops = _top_k_sorted_many(scores)
        for n, i in enumerate((2 * h, 2 * h + 1)):
            for r in range(k):
                top_ref[i, r] = tops[n][r]
        return carry

    lax.fori_loop(0, PEER_HEADS, per_head_sets, 0)

    def spread(rows):
        out = rows[7]
        for s in range(6, -1, -1):
            out = jnp.where(sub == s, rows[s], out)
        return out

    def per_head(h, carry):
        t1 = [top_ref[2 * h, r] for r in range(k)]
        t2 = [top_ref[2 * h + 1, r] for r in range(k)]
        p2a, p2b, p1b = spread(t2[:8]), spread(t2[8:]), spread(t1[8:])
        cand = [t1[0] + p2a, t1[0] + p2b] + [t1[i] + p2a for i in range(1, 8)] + [p1b + t2[0]]
        cand += [jnp.full((8, tq), NEG_INF, F32)] * (k - len(cand))
        best = _top_k_sorted(cand)
        z = jnp.ones((8, tq), F32)
        for r in range(1, k):
            z = z + jnp.exp(best[r] - best[0])
        thr, rz = best[k - 1], 1.0 / z
        one, zero = jnp.ones((8, tq), F32), jnp.zeros((8, tq), F32)
        for g2 in range(PEER_NKEYS // 16):
            ranks, e2s = [], []
            for g in (2 * g2, 2 * g2 + 1):
                s1 = s_ref[2 * h, 8 * g:8 * g + 8, :]
                s2 = s_ref[2 * h + 1, 8 * g:8 * g + 8, :]
                n, r = zero, zero
                for jj in range(k):
                    n = n + jnp.where(s1 + t2[jj] >= thr, one, zero)
                    r = r + jnp.where(t2[jj] > s2, one, zero)
                n1_ref[h, g] = n
                e1_ref[h, g] = jnp.exp(s1 - t1[0]) * rz
                ranks.append(r)
                e2s.append(jnp.exp(s2 - t2[0]))
            r2_ref[h, g2] = jnp.concatenate(ranks, axis=0).astype(BF16)
            e2_ref[h, g2] = jnp.concatenate(e2s, axis=0).astype(BF16)
        return carry

    lax.fori_loop(0, PEER_HEADS, per_head, 0)


def _peer_select(q, keys):
    t = q.shape[0]
    tq = PEER_TQ
    nset = 2 * PEER_HEADS
    hh, g8, g16 = PEER_HEADS, PEER_NKEYS // 8, PEER_NKEYS // 16
    return pl.pallas_call(
        _peer_select_kernel,
        grid=(t // tq,),
        in_specs=[pl.BlockSpec((tq, q.shape[1]), lambda i: (i, 0)),
                  pl.BlockSpec((nset, PEER_NKEYS, PEER_KEY_DIM), lambda i: (0, 0, 0))],
        out_specs=[pl.BlockSpec((hh, g8, 8, tq), lambda i: (0, 0, 0, i)),
                   pl.BlockSpec((hh, g8, 8, tq), lambda i: (0, 0, 0, i)),
                   pl.BlockSpec((hh, g16, 16, tq), lambda i: (0, 0, 0, i)),
                   pl.BlockSpec((hh, g16, 16, tq), lambda i: (0, 0, 0, i))],
        out_shape=[jax.ShapeDtypeStruct((hh, g8, 8, t), F32),
                   jax.ShapeDtypeStruct((hh, g8, 8, t), F32),
                   jax.ShapeDtypeStruct((hh, g16, 16, t), BF16),
                   jax.ShapeDtypeStruct((hh, g16, 16, t), BF16)],
        scratch_shapes=[pltpu.VMEM((nset, PEER_NKEYS, tq), F32),
                        pltpu.VMEM((nset, PEER_TOPK, 8, tq), F32)],
        compiler_params=_cparams(("parallel",)),
        name="peer_select",
    )(q, keys)


PEER_TM = 512
PEER_TE = 1024
PEER_KEY_BLOCK = 32
PEER_ROW_BLOCK = 4


def _gelu(x):
    return 0.5 * x * (1.0 + lax.erf(x * np.float32(math.sqrt(0.5))))


def _peer_dense_kernel(h_ref, u_ref, vt_ref, n1_ref, e1_ref, r2_ref, e2_ref, x_ref, g2_ref, o_ref,
                       acc_ref, at_ref, wa_ref, bcn_ref, bce_ref):
    j = pl.program_id(1)
    nsteps = pl.num_programs(1)
    tm = PEER_TM
    te = PEER_TE
    nk = PEER_NKEYS

    @pl.when(j == 0)
    def _():
        acc_ref[...] = jnp.zeros_like(acc_ref)

    na = te // nk
    kb = PEER_KEY_BLOCK
    kg = kb // 16
    nkq = nk // kb

    at_ref[...] = _dot_nt(u_ref[...], h_ref[...])

    for h, al in itertools.product(range(PEER_HEADS), range(na)):
        bcn_ref[h, al] = jnp.broadcast_to(n1_ref[h, j, al:al + 1, :], (16, tm)).astype(BF16)
        bce_ref[h, al] = jnp.broadcast_to(e1_ref[h, j, al:al + 1, :], (16, tm)).astype(BF16)

    def gate_tile(it, carry):
        lc = it // nkq
        kq = it % nkq
        ls = pl.ds(pl.multiple_of(lc * LANES, LANES), LANES)
        for ab in range(na // PEER_ROW_BLOCK):
            als = range(ab * PEER_ROW_BLOCK, (ab + 1) * PEER_ROW_BLOCK)
            w = {(al, t): jnp.zeros((16, LANES), BF16) for al in als for t in range(kg)}
            for h in range(PEER_HEADS):
                r2 = [r2_ref[h, kq * kg + t, :, ls] for t in range(kg)]
                e2 = [e2_ref[h, kq * kg + t, :, ls] for t in range(kg)]
                for al in als:
                    n1 = bcn_ref[h, al, :, ls]
                    e1 = bce_ref[h, al, :, ls]
                    for t in range(kg):
                        w[al, t] = w[al, t] + jnp.where(r2[t] < n1, e2[t], jnp.zeros_like(e2[t])) * e1
            for al, t in itertools.product(als, range(kg)):
                rs = pl.ds(pl.multiple_of(al * nk + kq * kb + 16 * t, 16), 16)
                wa_ref[rs, ls] = (w[al, t].astype(F32) * _gelu(at_ref[rs, ls])).astype(BF16)
        return carry

    lax.fori_loop(0, (tm // LANES) * nkq, gate_tile, 0)

    acc_ref[...] += _dot(vt_ref[0], wa_ref[...])

    @pl.when(j == nsteps - 1)
    def _():
        o_ref[...] = x_ref[...] + g2_ref[0] * acc_ref[...].T


def _peer_dense(hh, u, vt, sel, x, g2, tiles_per_row):
    t, d = hh.shape
    ne = u.shape[0]
    tm, te = PEER_TM, PEER_TE
    one = pl.Buffered(1)
    na = te // PEER_NKEYS
    hd, g8, g16 = PEER_HEADS, PEER_NKEYS // 8, PEER_NKEYS // 16
    sel_map = lambda i, j: (0, 0, 0, i)
    return pl.pallas_call(
        _peer_dense_kernel,
        grid=(t // tm, ne // te),
        in_specs=[pl.BlockSpec((tm, d), lambda i, j: (i, 0), pipeline_mode=one),
                  pl.BlockSpec((te, d), lambda i, j: (j, 0)),
                  pl.BlockSpec((1, d, te), lambda i, j: (j, 0, 0)),
                  pl.BlockSpec((hd, g8, 8, tm), sel_map, pipeline_mode=one),
                  pl.BlockSpec((hd, g8, 8, tm), sel_map, pipeline_mode=one),
                  pl.BlockSpec((hd, g16, 16, tm), sel_map, pipeline_mode=one),
                  pl.BlockSpec((hd, g16, 16, tm), sel_map, pipeline_mode=one),
                  pl.BlockSpec((tm, d), lambda i, j: (i, 0), pipeline_mode=one),
                  pl.BlockSpec((1, 1, d), lambda i, j: (i // tiles_per_row, 0, 0))],
        out_specs=pl.BlockSpec((tm, d), lambda i, j: (i, 0)),
        out_shape=jax.ShapeDtypeStruct((t, d), F32),
        scratch_shapes=[pltpu.VMEM((d, tm), F32),
                        pltpu.VMEM((te, tm), F32),
                        pltpu.VMEM((te, tm), BF16),
                        pltpu.VMEM((hd, na, 16, tm), BF16),
                        pltpu.VMEM((hd, na, 16, tm), BF16)],
        compiler_params=_cparams(("parallel", "arbitrary")),
        name="peer_dense",
    )(hh, u, vt, *sel, x, g2)


def _peer_block(x, g, shift, scale, gate2, wq, keys, u, vt):
    b, l, d = x.shape
    t = b * l
    x_rows = x if shift.shape[0] > 1 else x.reshape(1, t, d)
    q, hh = _norm_mod_matmul(x_rows, g, shift, scale, wq, BF16, 512, emit_h=True)
    sel = _peer_select(q.reshape(t, -1), keys)
    bm = gate2.shape[0]
    tiles_per_row = (l // PEER_TM) if bm > 1 else (t // PEER_TM)
    out = _peer_dense(hh.reshape(t, d), u, vt, sel, x.reshape(t, d), gate2.reshape(bm, 1, d), tiles_per_row)
    return out.reshape(b, l, d)


def _rmsnorm_kernel(x_ref, g_ref, o_ref):
    x = x_ref[...]
    o_ref[...] = x * lax.rsqrt(jnp.mean(x * x, axis=-1, keepdims=True) + EPS) * g_ref[...]


def _rmsnorm(x, g):
    t, d = x.shape
    tm = 512
    return pl.pallas_call(
        _rmsnorm_kernel,
        grid=(t // tm,),
        in_specs=[pl.BlockSpec((tm, d), lambda i: (i, 0)), pl.BlockSpec((1, d), lambda i: (0, 0))],
        out_specs=pl.BlockSpec((tm, d), lambda i: (i, 0)),
        out_shape=jax.ShapeDtypeStruct((t, d), F32),
        compiler_params=_cparams(("parallel",)),
        name="final_rmsnorm",
    )(x, g.reshape(1, d))


def _pack_w_in(w_in):
    q, k, v, z, xbc, dt, glu, gate = jnp.split(
        w_in, np.cumsum([NA_WIDTH, NA_WIDTH, NA_WIDTH, SSM_INNER, SSM_XBC, 2 * SSM_HEADS, 2 * CONV_CH]).tolist(),
        axis=-1)
    packed = jnp.concatenate([gate, glu, q, k, v, z, xbc], axis=-1).astype(BF16)
    dt_w = jnp.pad(dt, ((0, 0), (0, LANES - 2 * SSM_HEADS))).astype(BF16)
    return packed, dt_w


def _trunk_layer(xl, xc, c_rows, need_ctx, w_mod, b_mod, norm1_g, norm2_g, w_in, na_rpb, na_wo,
                 ssm_conv_w, ssm_conv_b, ssm_dt_bias, ssm_a_log, ssm_d, ssm_norm_g, ssm_wo,
                 cv_dw_w, cv_dw_b, cv_ln_g, cv_ln_b, cv_wo, cv_bo, w_out, peer_wq, peer_keys, peer_u, peer_v):
    b, s, d = xl.shape
    mod = _modulation(c_rows, w_mod, b_mod)
    sh1, sc1, g1, sh2, sc2, g2 = [mod[:b, i * d:(i + 1) * d] for i in range(6)]
    csh1, csc1, cg1, csh2, csc2, cg2 = [mod[b:b + 1, i * d:(i + 1) * d] for i in range(6)]

    w_pack, w_dt = _pack_w_in(w_in)
    dt_bias = jnp.pad(ssm_dt_bias.astype(F32).reshape(-1), (0, LANES - 2 * SSM_HEADS))

    cb, cl, _ = xc.shape
    xc_flat = xc.reshape(1, cb * cl, d)
    p_l = _norm_mod_matmul(xl, norm1_g, sh1, sc1, w_pack, BF16, 512)
    p_c = _norm_mod_matmul(xc_flat, norm1_g, csh1, csc1, w_pack, BF16, 512).reshape(cb, cl, -1)
    dt_l = _norm_mod_matmul(xl, norm1_g, sh1, sc1, w_dt, F32, LANES, softplus_bias=dt_bias)
    dt_c = _norm_mod_matmul(xc_flat, norm1_g, csh1, csc1, w_dt, F32, LANES,
                            softplus_bias=dt_bias).reshape(cb, cl, -1)

    bias_tab = _na_bias_table(na_rpb, s // GRID_W)
    y_att, y_att_c = _neighbourhood_attention(p_l, p_c, bias_tab, need_ctx)

    xbc_l = _dwconv(p_l, OFF_XBC, None, SSM_XBC, ssm_conv_w, ssm_conv_b, None, None, F32)
    xbc_c = _dwconv(p_c, OFF_XBC, None, SSM_XBC, ssm_conv_w, ssm_conv_b, None, None, F32)
    y_ssm, y_ssm_c = _ssd_scan(xbc_l, xbc_c, dt_l, dt_c, ssm_a_log, ssm_d)

    wa, ws, wc, wo = (na_wo.astype(BF16), ssm_wo.astype(BF16), cv_wo.astype(BF16), w_out.astype(BF16))
    y_cv = _dwconv(p_l, OFF_GLU_A, OFF_GLU_G, CONV_CH, cv_dw_w, cv_dw_b, cv_ln_g, cv_ln_b, BF16)
    xl = _merge(xl, g1, p_l, y_att, y_ssm, y_cv, ssm_norm_g, wa, ws, wc, cv_bo, wo)
    if need_ctx:
        y_cv_c = _dwconv(p_c, OFF_GLU_A, OFF_GLU_G, CONV_CH, cv_dw_w, cv_dw_b, cv_ln_g, cv_ln_b, BF16)
        xc = _merge(xc, cg1, p_c, y_att_c, y_ssm_c, y_cv_c, ssm_norm_g, wa, ws, wc, cv_bo, wo)

    wq = peer_wq.astype(BF16)
    keys = peer_keys.astype(BF16).reshape(2 * PEER_HEADS, PEER_NKEYS, PEER_KEY_DIM)
    u = peer_u.astype(BF16)
    vt = peer_v.astype(BF16).reshape(PEER_EXPERTS // PEER_TE, PEER_TE, d).transpose(0, 2, 1)
    xl = _peer_block(xl, norm2_g, sh2, sc2, g2, wq, keys, u, vt)
    if need_ctx:
        xc = _peer_block(xc, norm2_g, csh2, csc2, cg2, wq, keys, u, vt)
    return xl, xc


def kernel(x, c, ctx, c_ctx, w_mod, b_mod, norm1_g, norm2_g, w_in, na_rpb, na_wo, ssm_conv_w, ssm_conv_b,
           ssm_dt_bias, ssm_A_log, ssm_D, ssm_norm_g, ssm_wo, cv_dw_w, cv_dw_b, cv_ln_g, cv_ln_b, cv_wo,
           cv_bo, w_out, peer_wq, peer_keys, peer_u, peer_v, final_norm_g):
    b, s, d = x.shape
    depth = w_mod.shape[0]
    rows = -(-(b + 1) // 8) * 8
    c_rows = jnp.zeros((rows, d), F32).at[:b].set(c).at[b].set(c_ctx)
    xl, xc = x, ctx
    for l in range(depth):
        xl, xc = _trunk_layer(xl, xc, c_rows, l < depth - 1, w_mod[l], b_mod[l], norm1_g[l], norm2_g[l],
                              w_in[l], na_rpb[l], na_wo[l], ssm_conv_w[l], ssm_conv_b[l], ssm_dt_bias[l],
                              ssm_A_log[l], ssm_D[l], ssm_norm_g[l], ssm_wo[l], cv_dw_w[l], cv_dw_b[l],
                              cv_ln_g[l], cv_ln_b[l], cv_wo[l], cv_bo[l], w_out[l],
                              peer_wq[l], peer_keys[l], peer_u[l], peer_v[l])
    return _rmsnorm(xl.reshape(b * s, d), final_norm_g).reshape(b, s, d)
```

```python
import functools
import itertools
import math

import numpy as np
import jax
import jax.numpy as jnp
from jax import lax
from jax.experimental import pallas as pl
from jax.experimental.pallas import tpu as pltpu

F32 = jnp.float32
BF16 = jnp.bfloat16
EPS = 1e-6

D_MODEL = 2048
GRID_W = 64
NA_HEADS = 16
NA_HEAD_DIM = 64
NA_WIDTH = NA_HEADS * NA_HEAD_DIM
WIN_R = 8
WIN_C = 16
SSM_HEADS = 16
SSM_HEAD_DIM = 64
SSM_INNER = SSM_HEADS * SSM_HEAD_DIM
SSM_GROUPS = 2
SSM_STATE = 128
SSM_XBC = SSM_INNER + 2 * SSM_GROUPS * SSM_STATE
SSM_CONV = 5
SSM_CHUNK = 128
CONV_CH = 1024
CONV_K = 31
PEER_HEADS = 8
PEER_NKEYS = 128
PEER_KEY_DIM = 128
PEER_TOPK = 16
PEER_EXPERTS = PEER_NKEYS * PEER_NKEYS

OFF_GATE = 0
OFF_GLU_A = 6144
OFF_GLU_G = 7168
OFF_Q = 8192
OFF_K = 9216
OFF_V = 10240
OFF_Z = 11264
OFF_XBC = 12288
PACK_COLS = 13824

LANES = 128
VMEM_LIMIT = 56 * 1024 * 1024


def _cparams(sem):
    return pltpu.CompilerParams(dimension_semantics=sem, vmem_limit_bytes=VMEM_LIMIT)


def _dot(a, b):
    return jnp.dot(a, b, preferred_element_type=F32)


def _dot_nt(a, b):
    return lax.dot_general(a, b, (((1,), (1,)), ((), ())), preferred_element_type=F32)


def _dot_tn(a, b):
    return lax.dot_general(a, b, (((0,), (0,)), ((), ())), preferred_element_type=F32)


def _split3(a):
    hi = a.astype(BF16)
    r1 = a - hi.astype(F32)
    mid = r1.astype(BF16)
    lo = (r1 - mid.astype(F32)).astype(BF16)
    return hi, mid, lo


def _dot_exact_rhs01(a, m01):
    hi, mid, lo = _split3(a)
    return _dot(hi, m01) + _dot(mid, m01) + _dot(lo, m01)


def _dot_exact_lhs01(m01, a):
    hi, mid, lo = _split3(a)
    return _dot(m01, hi) + _dot(m01, mid) + _dot(m01, lo)


def _sigmoid(x):
    return 1.0 / (1.0 + jnp.exp(-x))


def _silu(x):
    return x * _sigmoid(x)


def _mod_kernel(c_ref, w_ref, b_ref, o_ref):
    a = _silu(c_ref[...]).astype(BF16)
    o_ref[...] = _dot(a, w_ref[...].astype(BF16)) + b_ref[...]


def _modulation(cs, w_mod, b_mod):
    r, d = cs.shape
    n = w_mod.shape[1]
    tn = 1024
    return pl.pallas_call(
        _mod_kernel,
        grid=(n // tn,),
        in_specs=[pl.BlockSpec((r, d), lambda j: (0, 0)),
                  pl.BlockSpec((d, tn), lambda j: (0, j)),
                  pl.BlockSpec((1, tn), lambda j: (0, j))],
        out_specs=pl.BlockSpec((r, tn), lambda j: (0, j)),
        out_shape=jax.ShapeDtypeStruct((r, n), F32),
        compiler_params=_cparams(("arbitrary",)),
        name="modulation",
    )(cs, w_mod, b_mod.reshape(1, n))


def _nmm_kernel(x_ref, g_ref, sh_ref, sc_ref, w_ref, *rest, softplus_bias, emit_h):
    if softplus_bias:
        bias_ref, rest = rest[0], rest[1:]
    if emit_h:
        o_ref, ho_ref, h_ref = rest
    else:
        o_ref, h_ref = rest

    @pl.when(pl.program_id(2) == 0)
    def _():
        x = x_ref[0]
        ms = jnp.mean(x * x, axis=-1, keepdims=True)
        y = x * lax.rsqrt(ms + EPS) * g_ref[...]
        h = (y * (1.0 + sc_ref[0]) + sh_ref[0]).astype(BF16)
        h_ref[...] = h
        if emit_h:
            ho_ref[0] = h

    acc = _dot(h_ref[...], w_ref[0])
    if softplus_bias:
        t = acc + bias_ref[...]
        acc = jnp.maximum(t, 0.0) + jnp.log1p(jnp.exp(-jnp.abs(t)))
    o_ref[0] = acc.astype(o_ref.dtype)


def _column_blocks(w, tn):
    d, n = w.shape
    return w.reshape(d, n // tn, tn).transpose(1, 0, 2)


def _norm_mod_matmul(x, g, shift, scale, w, out_dtype, tn, softplus_bias=None, emit_h=False):
    b, l, d = x.shape
    n = w.shape[1]
    w = _column_blocks(w, tn)
    tm = min(l, 1024)
    bm = shift.shape[0]
    mod_map = (lambda i, m, j: (i, 0, 0)) if bm > 1 else (lambda i, m, j: (0, 0, 0))
    in_specs = [pl.BlockSpec((1, tm, d), lambda i, m, j: (i, m, 0)),
                pl.BlockSpec((1, d), lambda i, m, j: (0, 0)),
                pl.BlockSpec((1, 1, d), mod_map),
                pl.BlockSpec((1, 1, d), mod_map),
                pl.BlockSpec((1, d, tn), lambda i, m, j: (j, 0, 0))]
    args = [x, g.reshape(1, d), shift.reshape(bm, 1, d), scale.reshape(bm, 1, d), w]
    if softplus_bias is not None:
        in_specs.append(pl.BlockSpec((1, tn), lambda i, m, j: (0, j)))
        args.append(softplus_bias.reshape(1, n))
    out_shape = [jax.ShapeDtypeStruct((b, l, n), out_dtype)]
    out_specs = [pl.BlockSpec((1, tm, tn), lambda i, m, j: (i, m, j))]
    if emit_h:
        out_shape.append(jax.ShapeDtypeStruct((b, l, d), BF16))
        out_specs.append(pl.BlockSpec((1, tm, d), lambda i, m, j: (i, m, 0)))
    res = pl.pallas_call(
        functools.partial(_nmm_kernel, softplus_bias=softplus_bias is not None, emit_h=emit_h),
        grid=(b, l // tm, n // tn),
        in_specs=in_specs,
        out_specs=out_specs,
        out_shape=out_shape,
        scratch_shapes=[pltpu.VMEM((tm, d), BF16)],
        compiler_params=_cparams(("parallel", "parallel", "arbitrary")),
        name="norm_mod_matmul",
    )(*args)
    return res if emit_h else res[0]


def _na_bias_table(rpb, rows):
    wr = min(WIN_R, rows)
    j = np.arange(GRID_W)
    cstart = np.clip(j - WIN_C // 2, 0, GRID_W - WIN_C)
    kc = np.arange(GRID_W)
    mask = (kc[None, :] >= cstart[:, None]) & (kc[None, :] < cstart[:, None] + WIN_C)
    col_off = np.clip(kc[None, :] - j[:, None], -(WIN_C - 1), WIN_C - 1) + WIN_C - 1
    dd = np.arange(wr)[:, None]
    ww = np.arange(wr)[None, :]
    row_idx = ww - dd + WIN_R - 1
    t = rpb.astype(F32)[:, row_idx][:, :, :, col_off]
    t = jnp.where(mask[None, None, None], t, -1e30)
    t = t.transpose(0, 1, 3, 2, 4).reshape(NA_HEADS, wr, GRID_W, wr * GRID_W)
    return t.reshape(NA_HEADS // 2, 2, wr, GRID_W, wr * GRID_W)


NA_ROW_GROUP = 8


def _na_kernel(q_ref, k_ref, v_ref, kc_ref, vc_ref, bias_ref, *rest, rows, wr, with_ctx):
    if with_ctx:
        qc_ref, o_ref, oc_ref = rest
    else:
        (o_ref,) = rest
    lane = lax.broadcasted_iota(jnp.int32, (1, LANES), 1)
    lo = lane < NA_HEAD_DIM
    kc = kc_ref[0]
    vc = vc_ref[0]
    scale = NA_HEAD_DIM ** -0.5
    nwin = wr * GRID_W

    def attend(problems):
        chains = []
        for q, kw, vw, bias_of in problems:
            for h in range(2):
                sel = lo if h == 0 else jnp.logical_not(lo)
                qh = jnp.where(sel, q, jnp.zeros_like(q))
                s_c = _dot_nt(qh, kc) * scale
                s_w = None if kw is None else _dot_nt(qh, kw) * scale + bias_of(h)
                chains.append((s_c, s_w, vw))
        maxes = []
        for s_c, s_w, _ in chains:
            m = jnp.max(s_c, axis=-1, keepdims=True)
            if s_w is not None:
                m = jnp.maximum(m, jnp.max(s_w, axis=-1, keepdims=True))
            maxes.append(m)
        probs = []
        for (s_c, s_w, _), m in zip(chains, maxes):
            probs.append((jnp.exp(s_c - m), None if s_w is None else jnp.exp(s_w - m)))
        outs = []
        for (_, _, vw), (p_c, p_w) in zip(chains, probs):
            den = jnp.sum(p_c, axis=-1, keepdims=True)
            o = _dot(p_c.astype(BF16), vc)
            if p_w is not None:
                den = den + jnp.sum(p_w, axis=-1, keepdims=True)
                o = o + _dot(p_w.astype(BF16), vw)
            outs.append(o / den)
        return [jnp.where(lo, outs[2 * i], outs[2 * i + 1]) for i in range(len(problems))]

    def body(g, carry):
        problems, q0s = [], []
        for rr in range(NA_ROW_GROUP):
            r = g * NA_ROW_GROUP + rr
            rs = jnp.clip(r - wr // 2, 0, rows - wr)
            d = r - rs
            q0 = pl.multiple_of(r * GRID_W, GRID_W)
            k0 = pl.multiple_of(rs * GRID_W, GRID_W)
            problems.append((q_ref[0, pl.ds(q0, GRID_W), :], k_ref[0, pl.ds(k0, nwin), :],
                             v_ref[0, pl.ds(k0, nwin), :], lambda h, d=d: bias_ref[0, h, d]))
            q0s.append(q0)
        for q0, o in zip(q0s, attend(problems)):
            o_ref[0, pl.ds(q0, GRID_W), :] = o.astype(o_ref.dtype)
        return carry

    lax.fori_loop(0, rows // NA_ROW_GROUP, body, 0)
    if with_ctx:
        oc_ref[0] = attend([(qc_ref[0], None, None, None)])[0].astype(oc_ref.dtype)


def _neighbourhood_attention(pl_lat, pl_ctx, bias_tab, with_ctx):
    b, s, _ = pl_lat.shape
    cl = pl_ctx.shape[1]
    rows = s // GRID_W
    wr = min(WIN_R, rows)
    nhp = NA_HEADS // 2
    qb, kb, vb = OFF_Q // LANES, OFF_K // LANES, OFF_V // LANES
    in_specs = [pl.BlockSpec((1, s, LANES), lambda hp, i: (i, 0, qb + hp)),
                pl.BlockSpec((1, s, LANES), lambda hp, i: (i, 0, kb + hp)),
                pl.BlockSpec((1, s, LANES), lambda hp, i: (i, 0, vb + hp)),
                pl.BlockSpec((1, cl, LANES), lambda hp, i: (i, 0, kb + hp)),
                pl.BlockSpec((1, cl, LANES), lambda hp, i: (i, 0, vb + hp)),
                pl.BlockSpec((1, 2, wr, GRID_W, wr * GRID_W), lambda hp, i: (hp, 0, 0, 0, 0))]
    args = [pl_lat, pl_lat, pl_lat, pl_ctx, pl_ctx, bias_tab]
    out_shape = [jax.ShapeDtypeStruct((b, s, NA_WIDTH), BF16)]
    out_specs = [pl.BlockSpec((1, s, LANES), lambda hp, i: (i, 0, hp))]
    if with_ctx:
        in_specs.append(pl.BlockSpec((1, cl, LANES), lambda hp, i: (i, 0, qb + hp)))
        args.append(pl_ctx)
        out_shape.append(jax.ShapeDtypeStruct((b, cl, NA_WIDTH), BF16))
        out_specs.append(pl.BlockSpec((1, cl, LANES), lambda hp, i: (i, 0, hp)))
    res = pl.pallas_call(
        functools.partial(_na_kernel, rows=rows, wr=wr, with_ctx=with_ctx),
        grid=(nhp, b),
        in_specs=in_specs,
        out_specs=out_specs,
        out_shape=out_shape,
        compiler_params=_cparams(("parallel", "parallel")),
        name="neighbourhood_attention",
    )(*args)
    return (res[0], res[1]) if with_ctx else (res[0], None)


CONV_TL = 256
CONV_HALO = 16


def _dwconv_kernel(*refs, taps, ch, glu, tl):
    if glu:
        (a_p, a_c, a_n, g_p, g_c, g_n, w_ref, b_ref, lng_ref, lnb_ref, o_ref, u_ref, y_ref) = refs
    else:
        (a_p, a_c, a_n, w_ref, b_ref, o_ref, u_ref) = refs
    i = pl.program_id(1)
    n = pl.num_programs(1)
    pad = (taps - 1) // 2
    hl = CONV_HALO

    def pre(a, g):
        a = a.astype(F32)
        if glu:
            return a * _sigmoid(g.astype(F32))
        return a

    u_ref[hl:hl + tl, :] = pre(a_c[0], g_c[0] if glu else None)
    top = pre(a_p[0, tl - hl:tl, :], g_p[0, tl - hl:tl, :] if glu else None)
    u_ref[0:hl, :] = jnp.where(i > 0, top, 0.0)
    bot = pre(a_n[0, 0:hl, :], g_n[0, 0:hl, :] if glu else None)
    u_ref[hl + tl:hl + tl + hl, :] = jnp.where(i < n - 1, bot, 0.0)

    for cc in range(ch // LANES):
        cs = slice(cc * LANES, (cc + 1) * LANES)
        acc = jnp.zeros((tl, LANES), F32) + b_ref[:, cs]
        for k in range(taps):
            off = hl - pad + k
            acc = acc + w_ref[k:k + 1, cs] * u_ref[off:off + tl, cs]
        if glu:
            y_ref[:, cs] = acc
        else:
            o_ref[0, :, cs] = _silu(acc).astype(o_ref.dtype)

    if glu:
        y = y_ref[...]
        mu = jnp.mean(y, axis=-1, keepdims=True)
        yc = y - mu
        var = jnp.mean(yc * yc, axis=-1, keepdims=True)
        z = yc * lax.rsqrt(var + EPS) * lng_ref[...] + lnb_ref[...]
        o_ref[0] = _silu(z).astype(o_ref.dtype)


def _dwconv(src, off_a, off_g, ch, w, b, ln_g, ln_b, out_dtype):
    bsz, l, _ = src.shape
    taps = w.shape[0]
    glu = off_g is not None
    tl = CONV_TL
    nt = l // tl
    ca = off_a // ch

    def spec(cb, delta):
        def imap(i, t):
            return (i, jnp.clip(t + delta, 0, nt - 1), cb)
        return pl.BlockSpec((1, tl, ch), imap)

    in_specs = [spec(ca, -1), spec(ca, 0), spec(ca, 1)]
    args = [src, src, src]
    if glu:
        cg = off_g // ch
        in_specs += [spec(cg, -1), spec(cg, 0), spec(cg, 1)]
        args += [src, src, src]
    in_specs += [pl.BlockSpec((taps, ch), lambda i, t: (0, 0)), pl.BlockSpec((1, ch), lambda i, t: (0, 0))]
    args += [w, b.reshape(1, ch)]
    scratch = [pltpu.VMEM((tl + 2 * CONV_HALO, ch), F32)]
    if glu:
        in_specs += [pl.BlockSpec((1, ch), lambda i, t: (0, 0)), pl.BlockSpec((1, ch), lambda i, t: (0, 0))]
        args += [ln_g.reshape(1, ch), ln_b.reshape(1, ch)]
        scratch.append(pltpu.VMEM((tl, ch), F32))
    return pl.pallas_call(
        functools.partial(_dwconv_kernel, taps=taps, ch=ch, glu=glu, tl=tl),
        grid=(bsz, nt),
        in_specs=in_specs,
        out_specs=pl.BlockSpec((1, tl, ch), lambda i, t: (i, t, 0)),
        out_shape=jax.ShapeDtypeStruct((bsz, l, ch), out_dtype),
        scratch_shapes=scratch,
        compiler_params=_cparams(("parallel", "parallel")),
        name="dwconv_glu" if glu else "dwconv_ssm",
    )(*args)


def _run_lockstep(coroutines):
    out = [None] * len(coroutines)
    live = list(range(len(coroutines)))
    while live:
        for i in list(live):
            try:
                next(coroutines[i])
            except StopIteration as done:
                out[i] = done.value
                live.remove(i)
    return out


def _ssd_kernel(xlf_ref, xcf_ref, dtlf_ref, dtcf_ref, dttlf_ref, dttcf_ref,
                xlb_ref, xcb_ref, dtlb_ref, dtcb_ref, dttlb_ref, dttcb_ref,
                alog_ref, alogt_ref, dskip_ref, exp_ref,
                ylf_ref, ycf_ref, ylb_ref, ycb_ref, h_ref, *, ncc):
    c = pl.program_id(1)
    q = SSM_CHUNK
    hh = SSM_HEADS
    hpg = SSM_HEADS // SSM_GROUPS
    gw = hpg * SSM_HEAD_DIM
    n = SSM_STATE
    is_ctx = c < ncc

    @pl.when(c == 0)
    def _():
        h_ref[...] = jnp.zeros_like(h_ref)

    li = lax.broadcasted_iota(jnp.int32, (q, q), 0)
    si = lax.broadcasted_iota(jnp.int32, (q, q), 1)
    lane = lax.broadcasted_iota(jnp.int32, (1, LANES), 1)
    lo = lane < SSM_HEAD_DIM
    expand = exp_ref[...]
    alog = alog_ref[...]
    alogt = alogt_ref[...]

    def direction(d, xl_ref, xc_ref, dtl_ref, dtc_ref, dttl_ref, dttc_ref):
        fwd = d == 0
        xbc = jnp.where(is_ctx, xc_ref[0], xl_ref[0]).astype(F32)
        x = xbc[:, :SSM_INNER]
        dt2 = jnp.where(is_ctx, dtc_ref[0], dtl_ref[0])
        dt_col = dt2[:, d * hh:(d + 1) * hh]
        dtt2 = jnp.where(is_ctx, dttc_ref[0], dttl_ref[0])
        dt_row = dtt2[d * hh:(d + 1) * hh, :]
        a_col = dt_col * -jnp.exp(alog[d:d + 1, :])
        a_row = dt_row * -jnp.exp(alogt[:, d:d + 1])

        tri = (si <= li) if fwd else (si >= li)
        trit = (li <= si) if fwd else (li >= si)
        tri_b = jnp.where(tri, 1.0, 0.0).astype(BF16)
        trit_b = jnp.where(trit, 1.0, 0.0).astype(BF16)
        e_col = _dot_exact_lhs01(tri_b, a_col)
        e_row = _dot_exact_rhs01(a_row, trit_b)
        last = q - 1 if fwd else 0
        e_tot = e_col[last:last + 1, :]
        yield

        w_dt = _dot_exact_rhs01(dt_col, expand)
        w_dec = _dot_exact_rhs01(jnp.exp(e_tot - e_col), expand)
        w_off = _dot_exact_rhs01(jnp.exp(e_col), expand)
        w_tot = w_off[last:last + 1, :]

        xdt = x * w_dt
        xdt_b = xdt.astype(BF16)
        xdec_b = (xdt * w_dec).astype(BF16)
        yield

        y_parts = []
        for g in range(SSM_GROUPS):
            bm = xbc[:, SSM_INNER + g * n:SSM_INNER + (g + 1) * n].astype(BF16)
            cm = xbc[:, SSM_INNER + (SSM_GROUPS + g) * n:SSM_INNER + (SSM_GROUPS + g + 1) * n].astype(BF16)
            cb = _dot_nt(cm, bm)
            hg = h_ref[d, g]
            y_off = _dot(cm, hg.astype(BF16))
            gs = slice(g * gw, (g + 1) * gw)
            h_ref[d, g] = hg * w_tot[:, gs] + _dot_tn(bm, xdec_b[:, gs])
            yield
            for pr in range(hpg // 2):
                h0 = g * hpg + 2 * pr
                cs = slice(h0 * SSM_HEAD_DIM, (h0 + 2) * SSM_HEAD_DIM)
                xp = xdt_b[:, cs]
                acc = None
                for k in range(2):
                    h = h0 + k
                    lm = jnp.where(tri, jnp.exp(e_col[:, h:h + 1] - e_row[h:h + 1, :]), 0.0)
                    mh = (cb * lm).astype(BF16)
                    sel = lo if k == 0 else jnp.logical_not(lo)
                    t = _dot(mh, jnp.where(sel, xp, jnp.zeros_like(xp)))
                    acc = t if acc is None else acc + t
                ys = slice(2 * pr * SSM_HEAD_DIM, (2 * pr + 2) * SSM_HEAD_DIM)
                y_parts.append(acc + y_off[:, ys] * w_off[:, cs])
                yield
        y = jnp.concatenate(y_parts, axis=-1)
        return y + dskip_ref[...] * x if fwd else y

    y_f, y_b = _run_lockstep([direction(0, xlf_ref, xcf_ref, dtlf_ref, dtcf_ref, dttlf_ref, dttcf_ref),
                              direction(1, xlb_ref, xcb_ref, dtlb_ref, dtcb_ref, dttlb_ref, dttcb_ref)])

    @pl.when(is_ctx)
    def _():
        ycf_ref[0] = y_f
        ycb_ref[0] = y_b

    @pl.when(jnp.logical_not(is_ctx))
    def _():
        ylf_ref[0] = y_f
        ylb_ref[0] = y_b


def _ssd_scan(xbc_l, xbc_c, dt_l, dt_c, a_log, d_skip):
    b, l, _ = xbc_l.shape
    cl = xbc_c.shape[1]
    q = SSM_CHUNK
    ncl, ncc = l // q, cl // q
    nc = ncl + ncc
    hh = SSM_HEADS
    dtt_l = jnp.swapaxes(dt_l[..., :2 * hh], 1, 2)
    dtt_c = jnp.swapaxes(dt_c[..., :2 * hh], 1, 2)
    expand = jnp.asarray(np.kron(np.eye(hh), np.ones((1, SSM_HEAD_DIM))), BF16)
    dskip = jnp.repeat(d_skip.astype(F32), SSM_HEAD_DIM).reshape(1, SSM_INNER)

    def lat_chunk(d, c):
        cc = jnp.maximum(c - ncc, 0)
        return cc if d == 0 else ncl - 1 - cc

    def ctx_chunk(d, c):
        cc = jnp.minimum(c, ncc - 1)
        return cc if d == 0 else ncc - 1 - cc

    def dir_specs(d):
        return [pl.BlockSpec((1, q, SSM_XBC), lambda i, c: (i, lat_chunk(d, c), 0)),
                pl.BlockSpec((1, q, SSM_XBC), lambda i, c: (i, ctx_chunk(d, c), 0)),
                pl.BlockSpec((1, q, LANES), lambda i, c: (i, lat_chunk(d, c), 0)),
                pl.BlockSpec((1, q, LANES), lambda i, c: (i, ctx_chunk(d, c), 0)),
                pl.BlockSpec((1, 2 * hh, q), lambda i, c: (i, 0, lat_chunk(d, c))),
                pl.BlockSpec((1, 2 * hh, q), lambda i, c: (i, 0, ctx_chunk(d, c)))]

    const = lambda i, c: (0, 0)
    in_specs = dir_specs(0) + dir_specs(1) + [
        pl.BlockSpec((2, hh), const), pl.BlockSpec((hh, 2), const),
        pl.BlockSpec((1, SSM_INNER), const), pl.BlockSpec((hh, SSM_INNER), const)]
    out_specs, out_shape = [], []
    for d in range(2):
        out_specs += [pl.BlockSpec((1, q, SSM_INNER), lambda i, c, d=d: (i, lat_chunk(d, c), 0)),
                      pl.BlockSpec((1, q, SSM_INNER), lambda i, c, d=d: (i, ctx_chunk(d, c), 0))]
        out_shape += [jax.ShapeDtypeStruct((b, l, SSM_INNER), F32), jax.ShapeDtypeStruct((b, cl, SSM_INNER), F32)]
    gw = (SSM_HEADS // SSM_GROUPS) * SSM_HEAD_DIM
    data = (xbc_l, xbc_c, dt_l, dt_c, dtt_l, dtt_c)
    yl_f, yc_f, yl_b, yc_b = pl.pallas_call(
        functools.partial(_ssd_kernel, ncc=ncc),
        grid=(b, nc),
        in_specs=in_specs,
        out_specs=out_specs,
        out_shape=out_shape,
        scratch_shapes=[pltpu.VMEM((2, SSM_GROUPS, SSM_STATE, gw), F32)],
        compiler_params=_cparams(("parallel", "arbitrary")),
        name="ssd_scan",
    )(*data, *data, a_log.astype(F32), a_log.astype(F32).T, dskip, expand)
    return (yl_f, yl_b), (yc_f, yc_b)


def _merge_kernel(x_ref, g1_ref, gate_ref, att_ref, ysf_ref, ysb_ref, z_ref, cv_ref, ng_ref,
                  wa_ref, ws_ref, wc_ref, bc_ref, wo_ref, o_ref):
    d = D_MODEL
    ys = (ysf_ref[0] + ysb_ref[0]) * _silu(z_ref[0].astype(F32))
    gsz = SSM_INNER // SSM_GROUPS
    parts = []
    for g in range(SSM_GROUPS):
        yg = ys[:, g * gsz:(g + 1) * gsz]
        parts.append(yg * lax.rsqrt(jnp.mean(yg * yg, axis=-1, keepdims=True) + EPS))
    yn = (jnp.concatenate(parts, axis=-1) * ng_ref[...]).astype(BF16)
    p_s = _dot(yn, ws_ref[...])
    p_a = _dot(att_ref[0], wa_ref[...])
    p_c = _dot(cv_ref[0], wc_ref[...]) + bc_ref[...]
    gl = gate_ref[0].astype(F32)
    m = (_sigmoid(gl[:, 0:d]) * p_a + _sigmoid(gl[:, d:2 * d]) * p_s
         + _sigmoid(gl[:, 2 * d:3 * d]) * p_c).astype(BF16)
    o_ref[0] = x_ref[0] + g1_ref[0] * _dot(m, wo_ref[...])


def _merge(x, g1, packed, y_att, y_ssm2, y_cv, norm_g, wa, ws, wc, bc, wo):
    b, l, d = x.shape
    tm = 256
    bm = g1.shape[0]
    g_map = (lambda i, t: (i, 0, 0)) if bm > 1 else (lambda i, t: (0, 0, 0))
    const = lambda i, t: (0, 0)
    one = pl.Buffered(1)
    in_specs = [pl.BlockSpec((1, tm, d), lambda i, t: (i, t, 0)),
                pl.BlockSpec((1, 1, d), g_map),
                pl.BlockSpec((1, tm, 3 * d), lambda i, t: (i, t, OFF_GATE // (3 * d))),
                pl.BlockSpec((1, tm, NA_WIDTH), lambda i, t: (i, t, 0)),
                pl.BlockSpec((1, tm, SSM_INNER), lambda i, t: (i, t, 0)),
                pl.BlockSpec((1, tm, SSM_INNER), lambda i, t: (i, t, 0)),
                pl.BlockSpec((1, tm, SSM_INNER), lambda i, t: (i, t, OFF_Z // SSM_INNER)),
                pl.BlockSpec((1, tm, CONV_CH), lambda i, t: (i, t, 0)),
                pl.BlockSpec((1, SSM_INNER), const),
                pl.BlockSpec((NA_WIDTH, d), const, pipeline_mode=one),
                pl.BlockSpec((SSM_INNER, d), const, pipeline_mode=one),
                pl.BlockSpec((CONV_CH, d), const, pipeline_mode=one),
                pl.BlockSpec((1, d), const),
                pl.BlockSpec((d, d), const, pipeline_mode=one)]
    return pl.pallas_call(
        _merge_kernel,
        grid=(b, l // tm),
        in_specs=in_specs,
        out_specs=pl.BlockSpec((1, tm, d), lambda i, t: (i, t, 0)),
        out_shape=jax.ShapeDtypeStruct((b, l, d), F32),
        compiler_params=_cparams(("parallel", "parallel")),
        name="merge",
    )(x, g1.reshape(bm, 1, d), packed, y_att, y_ssm2[0], y_ssm2[1], packed, y_cv, norm_g.reshape(1, -1),
      wa, ws, wc, bc.reshape(1, d), wo)


PEER_TQ = 256
NEG_INF = float("-inf")


def _sort_network(n):
    pairs = []

    def merge(lo, hi, r):
        step = r * 2
        if step < hi - lo:
            merge(lo, hi, step)
            merge(lo + r, hi, step)
            pairs.extend((i, i + r) for i in range(lo + r, hi - r, step))
        else:
            pairs.append((lo, lo + r))

    def sort(lo, hi):
        if hi - lo >= 1:
            mid = lo + (hi - lo) // 2
            sort(lo, mid)
            sort(mid + 1, hi)
            merge(lo, hi, 1)

    sort(0, n - 1)
    return pairs


def _exchange(v, i, j):
    a, b = v[i], v[j]
    v[i], v[j] = jnp.maximum(a, b), jnp.minimum(a, b)


def _top_k_sorted_many(problems):
    k = len(problems[0])
    vs = [list(v) for v in problems]
    for i, j in _sort_network(k):
        for v in vs:
            _exchange(v, i, j)
    for shift in (4, 2, 1):
        others = [[pltpu.roll(x, shift, axis=0) for x in v] for v in vs]
        vs = [[jnp.maximum(v[i], o[k - 1 - i]) for i in range(k)]
              for v, o in zip(vs, others)]
        dist = k // 2
        while dist >= 1:
            for i in range(k):
                if i & dist == 0:
                    for v in vs:
                        _exchange(v, i, i + dist)
            dist //= 2
    return vs


def _top_k_sorted(v):
    return _top_k_sorted_many([v])[0]


def _peer_select_kernel(q_ref, keys_ref, n1_ref, e1_ref, r2_ref, e2_ref, s_ref, top_ref):
    tq = PEER_TQ
    k = PEER_TOPK
    sub = lax.broadcasted_iota(jnp.int32, (8, tq), 0)

    def per_head_sets(h, carry):
        scores = []
        for i in (2 * h, 2 * h + 1):
            c0 = pl.multiple_of(i * PEER_KEY_DIM, PEER_KEY_DIM)
            s = _dot_nt(keys_ref[i], q_ref[:, pl.ds(c0, PEER_KEY_DIM)])
            s_ref[i] = s
            scores.append([s[8 * r:8 * r + 8, :] for r in range(PEER_NKEYS // 8)])
        tops = _top_k_sorted_many(scores)
        for n, i in enumerate((2 * h, 2 * h + 1)):
            for r in range(k):
                top_ref[i, r] = tops[n][r]
        return carry

    lax.fori_loop(0, PEER_HEADS, per_head_sets, 0)

    def spread(rows):
        out = rows[7]
        for s in range(6, -1, -1):
            out = jnp.where(sub == s, rows[s], out)
        return out

    def per_head(h, carry):
        t1 = [top_ref[2 * h, r] for r in range(k)]
        t2 = [top_ref[2 * h + 1, r] for r in range(k)]
        p2a, p2b, p1b = spread(t2[:8]), spread(t2[8:]), spread(t1[8:])
        cand = [t1[0] + p2a, t1[0] + p2b] + [t1[i] + p2a for i in range(1, 8)] + [p1b + t2[0]]
        cand += [jnp.full((8, tq), NEG_INF, F32)] * (k - len(cand))
        best = _top_k_sorted(cand)
        z = jnp.ones((8, tq), F32)
        for r in range(1, k):
            z = z + jnp.exp(best[r] - best[0])
        thr, rz = best[k - 1], 1.0 / z
        one, zero = jnp.ones((8, tq), F32), jnp.zeros((8, tq), F32)
        for g2 in range(PEER_NKEYS // 16):
            ranks, e2s = [], []
            for g in (2 * g2, 2 * g2 + 1):
                s1 = s_ref[2 * h, 8 * g:8 * g + 8, :]
                s2 = s_ref[2 * h + 1, 8 * g:8 * g + 8, :]
                n, r = zero, zero
                for jj in range(k):
                    n = n + jnp.where(s1 + t2[jj] >= thr, one, zero)
                    r = r + jnp.where(t2[jj] > s2, one, zero)
                n1_ref[h, g] = n
                e1_ref[h, g] = jnp.exp(s1 - t1[0]) * rz
                ranks.append(r)
                e2s.append(jnp.exp(s2 - t2[0]))
            r2_ref[h, g2] = jnp.concatenate(ranks, axis=0).astype(BF16)
            e2_ref[h, g2] = jnp.concatenate(e2s, axis=0).astype(BF16)
        return carry

    lax.fori_loop(0, PEER_HEADS, per_head, 0)


def _peer_select(q, keys):
    t = q.shape[0]
    tq = PEER_TQ
    nset = 2 * PEER_HEADS
    hh, g8, g16 = PEER_HEADS, PEER_NKEYS // 8, PEER_NKEYS // 16
    return pl.pallas_call(
        _peer_select_kernel,
        grid=(t // tq,),
        in_specs=[pl.BlockSpec((tq, q.shape[1]), lambda i: (i, 0)),
                  pl.BlockSpec((nset, PEER_NKEYS, PEER_KEY_DIM), lambda i: (0, 0, 0))],
        out_specs=[pl.BlockSpec((hh, g8, 8, tq), lambda i: (0, 0, 0, i)),
                   pl.BlockSpec((hh, g8, 8, tq), lambda i: (0, 0, 0, i)),
                   pl.BlockSpec((hh, g16, 16, tq), lambda i: (0, 0, 0, i)),
                   pl.BlockSpec((hh, g16, 16, tq), lambda i: (0, 0, 0, i))],
        out_shape=[jax.ShapeDtypeStruct((hh, g8, 8, t), F32),
                   jax.ShapeDtypeStruct((hh, g8, 8, t), F32),
                   jax.ShapeDtypeStruct((hh, g16, 16, t), BF16),
                   jax.ShapeDtypeStruct((hh, g16, 16, t), BF16)],
        scratch_shapes=[pltpu.VMEM((nset, PEER_NKEYS, tq), F32),
                        pltpu.VMEM((nset, PEER_TOPK, 8, tq), F32)],
        compiler_params=_cparams(("parallel",)),
        name="peer_select",
    )(q, keys)


PEER_TM = 512
PEER_TE = 1024
PEER_KEY_BLOCK = 32
PEER_ROW_BLOCK = 4


def _gelu(x):
    return 0.5 * x * (1.0 + lax.erf(x * np.float32(math.sqrt(0.5))))


def _peer_dense_kernel(h_ref, u_ref, vt_ref, n1_ref, e1_ref, r2_ref, e2_ref, x_ref, g2_ref, o_ref,
                       acc_ref, at_ref, wa_ref, bcn_ref, bce_ref):
    j = pl.program_id(1)
    nsteps = pl.num_programs(1)
    tm = PEER_TM
    te = PEER_TE
    nk = PEER_NKEYS

    @pl.when(j == 0)
    def _():
        acc_ref[...] = jnp.zeros_like(acc_ref)

    na = te // nk
    kb = PEER_KEY_BLOCK
    kg = kb // 16
    nkq = nk // kb

    at_ref[...] = _dot_nt(u_ref[...], h_ref[...])

    for h, al in itertools.product(range(PEER_HEADS), range(na)):
        bcn_ref[h, al] = jnp.broadcast_to(n1_ref[h, j, al:al + 1, :], (16, tm)).astype(BF16)
        bce_ref[h, al] = jnp.broadcast_to(e1_ref[h, j, al:al + 1, :], (16, tm)).astype(BF16)

    def gate_tile(it, carry):
        lc = it // nkq
        kq = it % nkq
        ls = pl.ds(pl.multiple_of(lc * LANES, LANES), LANES)
        for ab in range(na // PEER_ROW_BLOCK):
            als = range(ab * PEER_ROW_BLOCK, (ab + 1) * PEER_ROW_BLOCK)
            w = {(al, t): jnp.zeros((16, LANES), BF16) for al in als for t in range(kg)}
            for h in range(PEER_HEADS):
                r2 = [r2_ref[h, kq * kg + t, :, ls] for t in range(kg)]
                e2 = [e2_ref[h, kq * kg + t, :, ls] for t in range(kg)]
                for al in als:
                    n1 = bcn_ref[h, al, :, ls]
                    e1 = bce_ref[h, al, :, ls]
                    for t in range(kg):
                        w[al, t] = w[al, t] + jnp.where(r2[t] < n1, e2[t], jnp.zeros_like(e2[t])) * e1
            for al in als:
                rs = pl.ds(pl.multiple_of(al * nk + kq * kb, kb), kb)
                wt = jnp.concatenate([w[al, t] for t in range(kg)], axis=0)
                wa_ref[rs, ls] = (wt.astype(F32) * _gelu(at_ref[rs, ls])).astype(BF16)
        return carry

    lax.fori_loop(0, (tm // LANES) * nkq, gate_tile, 0)

    acc_ref[...] += _dot(vt_ref[0], wa_ref[...])

    @pl.when(j == nsteps - 1)
    def _():
        o_ref[...] = x_ref[...] + g2_ref[0] * acc_ref[...].T


def _peer_dense(hh, u, vt, sel, x, g2, tiles_per_row):
    t, d = hh.shape
    ne = u.shape[0]
    tm, te = PEER_TM, PEER_TE
    one = pl.Buffered(1)
    na = te // PEER_NKEYS
    hd, g8, g16 = PEER_HEADS, PEER_NKEYS // 8, PEER_NKEYS // 16
    sel_map = lambda i, j: (0, 0, 0, i)
    return pl.pallas_call(
        _peer_dense_kernel,
        grid=(t // tm, ne // te),
        in_specs=[pl.BlockSpec((tm, d), lambda i, j: (i, 0), pipeline_mode=one),
                  pl.BlockSpec((te, d), lambda i, j: (j, 0)),
                  pl.BlockSpec((1, d, te), lambda i, j: (j, 0, 0)),
                  pl.BlockSpec((hd, g8, 8, tm), sel_map, pipeline_mode=one),
                  pl.BlockSpec((hd, g8, 8, tm), sel_map, pipeline_mode=one),
                  pl.BlockSpec((hd, g16, 16, tm), sel_map, pipeline_mode=one),
                  pl.BlockSpec((hd, g16, 16, tm), sel_map, pipeline_mode=one),
                  pl.BlockSpec((tm, d), lambda i, j: (i, 0), pipeline_mode=one),
                  pl.BlockSpec((1, 1, d), lambda i, j: (i // tiles_per_row, 0, 0))],
        out_specs=pl.BlockSpec((tm, d), lambda i, j: (i, 0)),
        out_shape=jax.ShapeDtypeStruct((t, d), F32),
        scratch_shapes=[pltpu.VMEM((d, tm), F32),
                        pltpu.VMEM((te, tm), F32),
                        pltpu.VMEM((te, tm), BF16),
                        pltpu.VMEM((hd, na, 16, tm), BF16),
                        pltpu.VMEM((hd, na, 16, tm), BF16)],
        compiler_params=_cparams(("parallel", "arbitrary")),
        name="peer_dense",
    )(hh, u, vt, *sel, x, g2)


def _peer_block(x, g, shift, scale, gate2, wq, keys, u, vt):
    b, l, d = x.shape
    t = b * l
    x_rows = x if shift.shape[0] > 1 else x.reshape(1, t, d)
    q, hh = _norm_mod_matmul(x_rows, g, shift, scale, wq, BF16, 512, emit_h=True)
    sel = _peer_select(q.reshape(t, -1), keys)
    bm = gate2.shape[0]
    tiles_per_row = (l // PEER_TM) if bm > 1 else (t // PEER_TM)
    out = _peer_dense(hh.reshape(t, d), u, vt, sel, x.reshape(t, d), gate2.reshape(bm, 1, d), tiles_per_row)
    return out.reshape(b, l, d)


def _rmsnorm_kernel(x_ref, g_ref, o_ref):
    x = x_ref[...]
    o_ref[...] = x * lax.rsqrt(jnp.mean(x * x, axis=-1, keepdims=True) + EPS) * g_ref[...]


def _rmsnorm(x, g):
    t, d = x.shape
    tm = 512
    return pl.pallas_call(
        _rmsnorm_kernel,
        grid=(t // tm,),
        in_specs=[pl.BlockSpec((tm, d), lambda i: (i, 0)), pl.BlockSpec((1, d), lambda i: (0, 0))],
        out_specs=pl.BlockSpec((tm, d), lambda i: (i, 0)),
        out_shape=jax.ShapeDtypeStruct((t, d), F32),
        compiler_params=_cparams(("parallel",)),
        name="final_rmsnorm",
    )(x, g.reshape(1, d))


def _pack_w_in(w_in):
    q, k, v, z, xbc, dt, glu, gate = jnp.split(
        w_in, np.cumsum([NA_WIDTH, NA_WIDTH, NA_WIDTH, SSM_INNER, SSM_XBC, 2 * SSM_HEADS, 2 * CONV_CH]).tolist(),
        axis=-1)
    packed = jnp.concatenate([gate, glu, q, k, v, z, xbc], axis=-1).astype(BF16)
    dt_w = jnp.pad(dt, ((0, 0), (0, LANES - 2 * SSM_HEADS))).astype(BF16)
    return packed, dt_w


def _trunk_layer(xl, xc, c_rows, need_ctx, w_mod, b_mod, norm1_g, norm2_g, w_in, na_rpb, na_wo,
                 ssm_conv_w, ssm_conv_b, ssm_dt_bias, ssm_a_log, ssm_d, ssm_norm_g, ssm_wo,
                 cv_dw_w, cv_dw_b, cv_ln_g, cv_ln_b, cv_wo, cv_bo, w_out, peer_wq, peer_keys, peer_u, peer_v):
    b, s, d = xl.shape
    mod = _modulation(c_rows, w_mod, b_mod)
    sh1, sc1, g1, sh2, sc2, g2 = [mod[:b, i * d:(i + 1) * d] for i in range(6)]
    csh1, csc1, cg1, csh2, csc2, cg2 = [mod[b:b + 1, i * d:(i + 1) * d] for i in range(6)]

    w_pack, w_dt = _pack_w_in(w_in)
    dt_bias = jnp.pad(ssm_dt_bias.astype(F32).reshape(-1), (0, LANES - 2 * SSM_HEADS))

    cb, cl, _ = xc.shape
    xc_flat = xc.reshape(1, cb * cl, d)
    p_l = _norm_mod_matmul(xl, norm1_g, sh1, sc1, w_pack, BF16, 512)
    p_c = _norm_mod_matmul(xc_flat, norm1_g, csh1, csc1, w_pack, BF16, 512).reshape(cb, cl, -1)
    dt_l = _norm_mod_matmul(xl, norm1_g, sh1, sc1, w_dt, F32, LANES, softplus_bias=dt_bias)
    dt_c = _norm_mod_matmul(xc_flat, norm1_g, csh1, csc1, w_dt, F32, LANES,
                            softplus_bias=dt_bias).reshape(cb, cl, -1)

    bias_tab = _na_bias_table(na_rpb, s // GRID_W)
    y_att, y_att_c = _neighbourhood_attention(p_l, p_c, bias_tab, need_ctx)

    xbc_l = _dwconv(p_l, OFF_XBC, None, SSM_XBC, ssm_conv_w, ssm_conv_b, None, None, F32)
    xbc_c = _dwconv(p_c, OFF_XBC, None, SSM_XBC, ssm_conv_w, ssm_conv_b, None, None, F32)
    y_ssm, y_ssm_c = _ssd_scan(xbc_l, xbc_c, dt_l, dt_c, ssm_a_log, ssm_d)

    wa, ws, wc, wo = (na_wo.astype(BF16), ssm_wo.astype(BF16), cv_wo.astype(BF16), w_out.astype(BF16))
    y_cv = _dwconv(p_l, OFF_GLU_A, OFF_GLU_G, CONV_CH, cv_dw_w, cv_dw_b, cv_ln_g, cv_ln_b, BF16)
    xl = _merge(xl, g1, p_l, y_att, y_ssm, y_cv, ssm_norm_g, wa, ws, wc, cv_bo, wo)
    if need_ctx:
        y_cv_c = _dwconv(p_c, OFF_GLU_A, OFF_GLU_G, CONV_CH, cv_dw_w, cv_dw_b, cv_ln_g, cv_ln_b, BF16)
        xc = _merge(xc, cg1, p_c, y_att_c, y_ssm_c, y_cv_c, ssm_norm_g, wa, ws, wc, cv_bo, wo)

    wq = peer_wq.astype(BF16)
    keys = peer_keys.astype(BF16).reshape(2 * PEER_HEADS, PEER_NKEYS, PEER_KEY_DIM)
    u = peer_u.astype(BF16)
    vt = peer_v.astype(BF16).reshape(PEER_EXPERTS // PEER_TE, PEER_TE, d).transpose(0, 2, 1)
    xl = _peer_block(xl, norm2_g, sh2, sc2, g2, wq, keys, u, vt)
    if need_ctx:
        xc = _peer_block(xc, norm2_g, csh2, csc2, cg2, wq, keys, u, vt)
    return xl, xc


def kernel(x, c, ctx, c_ctx, w_mod, b_mod, norm1_g, norm2_g, w_in, na_rpb, na_wo, ssm_conv_w, ssm_conv_b,
           ssm_dt_bias, ssm_A_log, ssm_D, ssm_norm_g, ssm_wo, cv_dw_w, cv_dw_b, cv_ln_g, cv_ln_b, cv_wo,
           cv_bo, w_out, peer_wq, peer_keys, peer_u, peer_v, final_norm_g):
    b, s, d = x.shape
    depth = w_mod.shape[0]
    rows = -(-(b + 1) // 8) * 8
    c_rows = jnp.zeros((rows, d), F32).at[:b].set(c).at[b].set(c_ctx)
    xl, xc = x, ctx
    for l in range(depth):
        xl, xc = _trunk_layer(xl, xc, c_rows, l < depth - 1, w_mod[l], b_mod[l], norm1_g[l], norm2_g[l],
                              w_in[l], na_rpb[l], na_wo[l], ssm_conv_w[l], ssm_conv_b[l], ssm_dt_bias[l],
                              ssm_A_log[l], ssm_D[l], ssm_norm_g[l], ssm_wo[l], cv_dw_w[l], cv_dw_b[l],
                              cv_ln_g[l], cv_ln_b[l], cv_wo[l], cv_bo[l], w_out[l],
                              peer_wq[l], peer_keys[l], peer_u[l], peer_v[l])
    return _rmsnorm(xl.reshape(b * s, d), final_norm_g).reshape(b, s, d)
```

```python
import functools
import itertools
import math

import numpy as np
import jax
import jax.numpy as jnp
from jax import lax
from jax.experimental import pallas as pl
from jax.experimental.pallas import tpu as pltpu

F32 = jnp.float32
BF16 = jnp.bfloat16
EPS = 1e-6

D_MODEL = 2048
GRID_W = 64
NA_HEADS = 16
NA_HEAD_DIM = 64
NA_WIDTH = NA_HEADS * NA_HEAD_DIM
WIN_R = 8
WIN_C = 16
SSM_HEADS = 16
SSM_HEAD_DIM = 64
SSM_INNER = SSM_HEADS * SSM_HEAD_DIM
SSM_GROUPS = 2
SSM_STATE = 128
SSM_XBC = SSM_INNER + 2 * SSM_GROUPS * SSM_STATE
SSM_CONV = 5
SSM_CHUNK = 128
CONV_CH = 1024
CONV_K = 31
PEER_HEADS = 8
PEER_NKEYS = 128
PEER_KEY_DIM = 128
PEER_TOPK = 16
PEER_EXPERTS = PEER_NKEYS * PEER_NKEYS

OFF_GATE = 0
OFF_GLU_A = 6144
OFF_GLU_G = 7168
OFF_Q = 8192
OFF_K = 9216
OFF_V = 10240
OFF_Z = 11264
OFF_XBC = 12288
PACK_COLS = 13824

LANES = 128
VMEM_LIMIT = 56 * 1024 * 1024


def _cparams(sem):
    return pltpu.CompilerParams(dimension_semantics=sem, vmem_limit_bytes=VMEM_LIMIT)


def _dot(a, b):
    return jnp.dot(a, b, preferred_element_type=F32)


def _dot_nt(a, b):
    return lax.dot_general(a, b, (((1,), (1,)), ((), ())), preferred_element_type=F32)


def _dot_tn(a, b):
    return lax.dot_general(a, b, (((0,), (0,)), ((), ())), preferred_element_type=F32)


def _split3(a):
    hi = a.astype(BF16)
    r1 = a - hi.astype(F32)
    mid = r1.astype(BF16)
    lo = (r1 - mid.astype(F32)).astype(BF16)
    return hi, mid, lo


def _dot_exact_rhs01(a, m01):
    hi, mid, lo = _split3(a)
    return _dot(hi, m01) + _dot(mid, m01) + _dot(lo, m01)


def _dot_exact_lhs01(m01, a):
    hi, mid, lo = _split3(a)
    return _dot(m01, hi) + _dot(m01, mid) + _dot(m01, lo)


def _sigmoid(x):
    return 1.0 / (1.0 + jnp.exp(-x))


def _silu(x):
    return x * _sigmoid(x)


def _mod_kernel(c_ref, w_ref, b_ref, o_ref):
    a = _silu(c_ref[...]).astype(BF16)
    o_ref[...] = _dot(a, w_ref[...].astype(BF16)) + b_ref[...]


def _modulation(cs, w_mod, b_mod):
    r, d = cs.shape
    n = w_mod.shape[1]
    tn = 1024
    return pl.pallas_call(
        _mod_kernel,
        grid=(n // tn,),
        in_specs=[pl.BlockSpec((r, d), lambda j: (0, 0)),
                  pl.BlockSpec((d, tn), lambda j: (0, j)),
                  pl.BlockSpec((1, tn), lambda j: (0, j))],
        out_specs=pl.BlockSpec((r, tn), lambda j: (0, j)),
        out_shape=jax.ShapeDtypeStruct((r, n), F32),
        compiler_params=_cparams(("arbitrary",)),
        name="modulation",
    )(cs, w_mod, b_mod.reshape(1, n))


def _nmm_kernel(x_ref, g_ref, sh_ref, sc_ref, w_ref, *rest, softplus_bias, emit_h):
    if softplus_bias:
        bias_ref, rest = rest[0], rest[1:]
    if emit_h:
        o_ref, ho_ref, h_ref = rest
    else:
        o_ref, h_ref = rest

    @pl.when(pl.program_id(2) == 0)
    def _():
        x = x_ref[0]
        ms = jnp.mean(x * x, axis=-1, keepdims=True)
        y = x * lax.rsqrt(ms + EPS) * g_ref[...]
        h = (y * (1.0 + sc_ref[0]) + sh_ref[0]).astype(BF16)
        h_ref[...] = h
        if emit_h:
            ho_ref[0] = h

    acc = _dot(h_ref[...], w_ref[0])
    if softplus_bias:
        t = acc + bias_ref[...]
        acc = jnp.maximum(t, 0.0) + jnp.log1p(jnp.exp(-jnp.abs(t)))
    o_ref[0] = acc.astype(o_ref.dtype)


def _column_blocks(w, tn):
    d, n = w.shape
    return w.reshape(d, n // tn, tn).transpose(1, 0, 2)


def _norm_mod_matmul(x, g, shift, scale, w, out_dtype, tn, softplus_bias=None, emit_h=False):
    b, l, d = x.shape
    n = w.shape[1]
    w = _column_blocks(w, tn)
    tm = min(l, 1024)
    bm = shift.shape[0]
    mod_map = (lambda i, m, j: (i, 0, 0)) if bm > 1 else (lambda i, m, j: (0, 0, 0))
    in_specs = [pl.BlockSpec((1, tm, d), lambda i, m, j: (i, m, 0)),
                pl.BlockSpec((1, d), lambda i, m, j: (0, 0)),
                pl.BlockSpec((1, 1, d), mod_map),
                pl.BlockSpec((1, 1, d), mod_map),
                pl.BlockSpec((1, d, tn), lambda i, m, j: (j, 0, 0))]
    args = [x, g.reshape(1, d), shift.reshape(bm, 1, d), scale.reshape(bm, 1, d), w]
    if softplus_bias is not None:
        in_specs.append(pl.BlockSpec((1, tn), lambda i, m, j: (0, j)))
        args.append(softplus_bias.reshape(1, n))
    out_shape = [jax.ShapeDtypeStruct((b, l, n), out_dtype)]
    out_specs = [pl.BlockSpec((1, tm, tn), lambda i, m, j: (i, m, j))]
    if emit_h:
        out_shape.append(jax.ShapeDtypeStruct((b, l, d), BF16))
        out_specs.append(pl.BlockSpec((1, tm, d), lambda i, m, j: (i, m, 0)))
    res = pl.pallas_call(
        functools.partial(_nmm_kernel, softplus_bias=softplus_bias is not None, emit_h=emit_h),
        grid=(b, l // tm, n // tn),
        in_specs=in_specs,
        out_specs=out_specs,
        out_shape=out_shape,
        scratch_shapes=[pltpu.VMEM((tm, d), BF16)],
        compiler_params=_cparams(("parallel", "parallel", "arbitrary")),
        name="norm_mod_matmul",
    )(*args)
    return res if emit_h else res[0]


def _na_bias_table(rpb, rows):
    wr = min(WIN_R, rows)
    j = np.arange(GRID_W)
    cstart = np.clip(j - WIN_C // 2, 0, GRID_W - WIN_C)
    kc = np.arange(GRID_W)
    mask = (kc[None, :] >= cstart[:, None]) & (kc[None, :] < cstart[:, None] + WIN_C)
    col_off = np.clip(kc[None, :] - j[:, None], -(WIN_C - 1), WIN_C - 1) + WIN_C - 1
    dd = np.arange(wr)[:, None]
    ww = np.arange(wr)[None, :]
    row_idx = ww - dd + WIN_R - 1
    t = rpb.astype(F32)[:, row_idx][:, :, :, col_off]
    t = jnp.where(mask[None, None, None], t, -1e30)
    t = t.transpose(0, 1, 3, 2, 4).reshape(NA_HEADS, wr, GRID_W, wr * GRID_W)
    return t.reshape(NA_HEADS // 2, 2, wr, GRID_W, wr * GRID_W)


NA_ROW_GROUP = 8


def _na_kernel(q_ref, k_ref, v_ref, kc_ref, vc_ref, bias_ref, *rest, rows, wr, with_ctx):
    if with_ctx:
        qc_ref, o_ref, oc_ref = rest
    else:
        (o_ref,) = rest
    lane = lax.broadcasted_iota(jnp.int32, (1, LANES), 1)
    lo = lane < NA_HEAD_DIM
    kc = kc_ref[0]
    vc = vc_ref[0]
    scale = NA_HEAD_DIM ** -0.5
    nwin = wr * GRID_W

    def attend(problems):
        chains = []
        for q, kw, vw, bias_of in problems:
            for h in range(2):
                sel = lo if h == 0 else jnp.logical_not(lo)
                qh = jnp.where(sel, q, jnp.zeros_like(q))
                s_c = _dot_nt(qh, kc) * scale
                s_w = None if kw is None else _dot_nt(qh, kw) * scale + bias_of(h)
                chains.append((s_c, s_w, vw))
        maxes = []
        for s_c, s_w, _ in chains:
            m = jnp.max(s_c, axis=-1, keepdims=True)
            if s_w is not None:
                m = jnp.maximum(m, jnp.max(s_w, axis=-1, keepdims=True))
            maxes.append(m)
        probs = []
        for (s_c, s_w, _), m in zip(chains, maxes):
            probs.append((jnp.exp(s_c - m), None if s_w is None else jnp.exp(s_w - m)))
        outs = []
        for (_, _, vw), (p_c, p_w) in zip(chains, probs):
            den = jnp.sum(p_c, axis=-1, keepdims=True)
            o = _dot(p_c.astype(BF16), vc)
            if p_w is not None:
                den = den + jnp.sum(p_w, axis=-1, keepdims=True)
                o = o + _dot(p_w.astype(BF16), vw)
            outs.append(o / den)
        return [jnp.where(lo, outs[2 * i], outs[2 * i + 1]) for i in range(len(problems))]

    def body(g, carry):
        problems, q0s = [], []
        for rr in range(NA_ROW_GROUP):
            r = g * NA_ROW_GROUP + rr
            rs = jnp.clip(r - wr // 2, 0, rows - wr)
            d = r - rs
            q0 = pl.multiple_of(r * GRID_W, GRID_W)
            k0 = pl.multiple_of(rs * GRID_W, GRID_W)
            problems.append((q_ref[0, pl.ds(q0, GRID_W), :], k_ref[0, pl.ds(k0, nwin), :],
                             v_ref[0, pl.ds(k0, nwin), :], lambda h, d=d: bias_ref[0, h, d]))
            q0s.append(q0)
        for q0, o in zip(q0s, attend(problems)):
            o_ref[0, pl.ds(q0, GRID_W), :] = o.astype(o_ref.dtype)
        return carry

    lax.fori_loop(0, rows // NA_ROW_GROUP, body, 0)
    if with_ctx:
        oc_ref[0] = attend([(qc_ref[0], None, None, None)])[0].astype(oc_ref.dtype)


def _neighbourhood_attention(pl_lat, pl_ctx, bias_tab, with_ctx):
    b, s, _ = pl_lat.shape
    cl = pl_ctx.shape[1]
    rows = s // GRID_W
    wr = min(WIN_R, rows)
    nhp = NA_HEADS // 2
    qb, kb, vb = OFF_Q // LANES, OFF_K // LANES, OFF_V // LANES
    in_specs = [pl.BlockSpec((1, s, LANES), lambda hp, i: (i, 0, qb + hp)),
                pl.BlockSpec((1, s, LANES), lambda hp, i: (i, 0, kb + hp)),
                pl.BlockSpec((1, s, LANES), lambda hp, i: (i, 0, vb + hp)),
                pl.BlockSpec((1, cl, LANES), lambda hp, i: (i, 0, kb + hp)),
                pl.BlockSpec((1, cl, LANES), lambda hp, i: (i, 0, vb + hp)),
                pl.BlockSpec((1, 2, wr, GRID_W, wr * GRID_W), lambda hp, i: (hp, 0, 0, 0, 0))]
    args = [pl_lat, pl_lat, pl_lat, pl_ctx, pl_ctx, bias_tab]
    out_shape = [jax.ShapeDtypeStruct((b, s, NA_WIDTH), BF16)]
    out_specs = [pl.BlockSpec((1, s, LANES), lambda hp, i: (i, 0, hp))]
    if with_ctx:
        in_specs.append(pl.BlockSpec((1, cl, LANES), lambda hp, i: (i, 0, qb + hp)))
        args.append(pl_ctx)
        out_shape.append(jax.ShapeDtypeStruct((b, cl, NA_WIDTH), BF16))
        out_specs.append(pl.BlockSpec((1, cl, LANES), lambda hp, i: (i, 0, hp)))
    res = pl.pallas_call(
        functools.partial(_na_kernel, rows=rows, wr=wr, with_ctx=with_ctx),
        grid=(nhp, b),
        in_specs=in_specs,
        out_specs=out_specs,
        out_shape=out_shape,
        compiler_params=_cparams(("parallel", "parallel")),
        name="neighbourhood_attention",
    )(*args)
    return (res[0], res[1]) if with_ctx else (res[0], None)


CONV_TL = 256
CONV_HALO = 16


def _dwconv_kernel(*refs, taps, ch, glu, tl):
    if glu:
        (a_p, a_c, a_n, g_p, g_c, g_n, w_ref, b_ref, lng_ref, lnb_ref, o_ref, u_ref, y_ref) = refs
    else:
        (a_p, a_c, a_n, w_ref, b_ref, o_ref, u_ref) = refs
    i = pl.program_id(1)
    n = pl.num_programs(1)
    pad = (taps - 1) // 2
    hl = CONV_HALO

    def pre(a, g):
        a = a.astype(F32)
        if glu:
            return a * _sigmoid(g.astype(F32))
        return a

    u_ref[hl:hl + tl, :] = pre(a_c[0], g_c[0] if glu else None)
    top = pre(a_p[0, tl - hl:tl, :], g_p[0, tl - hl:tl, :] if glu else None)
    u_ref[0:hl, :] = jnp.where(i > 0, top, 0.0)
    bot = pre(a_n[0, 0:hl, :], g_n[0, 0:hl, :] if glu else None)
    u_ref[hl + tl:hl + tl + hl, :] = jnp.where(i < n - 1, bot, 0.0)

    for cc in range(ch // LANES):
        cs = slice(cc * LANES, (cc + 1) * LANES)
        acc = jnp.zeros((tl, LANES), F32) + b_ref[:, cs]
        for k in range(taps):
            off = hl - pad + k
            acc = acc + w_ref[k:k + 1, cs] * u_ref[off:off + tl, cs]
        if glu:
            y_ref[:, cs] = acc
        else:
            o_ref[0, :, cs] = _silu(acc).astype(o_ref.dtype)

    if glu:
        y = y_ref[...]
        mu = jnp.mean(y, axis=-1, keepdims=True)
        yc = y - mu
        var = jnp.mean(yc * yc, axis=-1, keepdims=True)
        z = yc * lax.rsqrt(var + EPS) * lng_ref[...] + lnb_ref[...]
        o_ref[0] = _silu(z).astype(o_ref.dtype)


def _dwconv(src, off_a, off_g, ch, w, b, ln_g, ln_b, out_dtype):
    bsz, l, _ = src.shape
    taps = w.shape[0]
    glu = off_g is not None
    tl = CONV_TL
    nt = l // tl
    ca = off_a // ch

    def spec(cb, delta):
        def imap(i, t):
            return (i, jnp.clip(t + delta, 0, nt - 1), cb)
        return pl.BlockSpec((1, tl, ch), imap)

    in_specs = [spec(ca, -1), spec(ca, 0), spec(ca, 1)]
    args = [src, src, src]
    if glu:
        cg = off_g // ch
        in_specs += [spec(cg, -1), spec(cg, 0), spec(cg, 1)]
        args += [src, src, src]
    in_specs += [pl.BlockSpec((taps, ch), lambda i, t: (0, 0)), pl.BlockSpec((1, ch), lambda i, t: (0, 0))]
    args += [w, b.reshape(1, ch)]
    scratch = [pltpu.VMEM((tl + 2 * CONV_HALO, ch), F32)]
    if glu:
        in_specs += [pl.BlockSpec((1, ch), lambda i, t: (0, 0)), pl.BlockSpec((1, ch), lambda i, t: (0, 0))]
        args += [ln_g.reshape(1, ch), ln_b.reshape(1, ch)]
        scratch.append(pltpu.VMEM((tl, ch), F32))
    return pl.pallas_call(
        functools.partial(_dwconv_kernel, taps=taps, ch=ch, glu=glu, tl=tl),
        grid=(bsz, nt),
        in_specs=in_specs,
        out_specs=pl.BlockSpec((1, tl, ch), lambda i, t: (i, t, 0)),
        out_shape=jax.ShapeDtypeStruct((bsz, l, ch), out_dtype),
        scratch_shapes=scratch,
        compiler_params=_cparams(("parallel", "parallel")),
        name="dwconv_glu" if glu else "dwconv_ssm",
    )(*args)


def _run_lockstep(coroutines):
    out = [None] * len(coroutines)
    live = list(range(len(coroutines)))
    while live:
        for i in list(live):
            try:
                next(coroutines[i])
            except StopIteration as done:
                out[i] = done.value
                live.remove(i)
    return out


def _ssd_kernel(xlf_ref, xcf_ref, dtlf_ref, dtcf_ref, dttlf_ref, dttcf_ref,
                xlb_ref, xcb_ref, dtlb_ref, dtcb_ref, dttlb_ref, dttcb_ref,
                alog_ref, alogt_ref, dskip_ref, exp_ref,
                ylf_ref, ycf_ref, ylb_ref, ycb_ref, h_ref, *, ncc):
    c = pl.program_id(1)
    q = SSM_CHUNK
    hh = SSM_HEADS
    hpg = SSM_HEADS // SSM_GROUPS
    gw = hpg * SSM_HEAD_DIM
    n = SSM_STATE
    is_ctx = c < ncc

    @pl.when(c == 0)
    def _():
        h_ref[...] = jnp.zeros_like(h_ref)

    li = lax.broadcasted_iota(jnp.int32, (q, q), 0)
    si = lax.broadcasted_iota(jnp.int32, (q, q), 1)
    lane = lax.broadcasted_iota(jnp.int32, (1, LANES), 1)
    lo = lane < SSM_HEAD_DIM
    expand = exp_ref[...]
    alog = alog_ref[...]
    alogt = alogt_ref[...]

    def direction(d, xl_ref, xc_ref, dtl_ref, dtc_ref, dttl_ref, dttc_ref):
        fwd = d == 0
        xbc = jnp.where(is_ctx, xc_ref[0], xl_ref[0]).astype(F32)
        x = xbc[:, :SSM_INNER]
        dt2 = jnp.where(is_ctx, dtc_ref[0], dtl_ref[0])
        dt_col = dt2[:, d * hh:(d + 1) * hh]
        dtt2 = jnp.where(is_ctx, dttc_ref[0], dttl_ref[0])
        dt_row = dtt2[d * hh:(d + 1) * hh, :]
        a_col = dt_col * -jnp.exp(alog[d:d + 1, :])
        a_row = dt_row * -jnp.exp(alogt[:, d:d + 1])

        tri = (si <= li) if fwd else (si >= li)
        trit = (li <= si) if fwd else (li >= si)
        tri_b = jnp.where(tri, 1.0, 0.0).astype(BF16)
        trit_b = jnp.where(trit, 1.0, 0.0).astype(BF16)
        e_col = _dot_exact_lhs01(tri_b, a_col)
        e_row = _dot_exact_rhs01(a_row, trit_b)
        last = q - 1 if fwd else 0
        e_tot = e_col[last:last + 1, :]
        yield

        w_dt = _dot_exact_rhs01(dt_col, expand)
        w_dec = _dot_exact_rhs01(jnp.exp(e_tot - e_col), expand)
        w_off = _dot_exact_rhs01(jnp.exp(e_col), expand)
        w_tot = w_off[last:last + 1, :]

        xdt = x * w_dt
        xdt_b = xdt.astype(BF16)
        xdec_b = (xdt * w_dec).astype(BF16)
        yield

        y_parts = []
        for g in range(SSM_GROUPS):
            bm = xbc[:, SSM_INNER + g * n:SSM_INNER + (g + 1) * n].astype(BF16)
            cm = xbc[:, SSM_INNER + (SSM_GROUPS + g) * n:SSM_INNER + (SSM_GROUPS + g + 1) * n].astype(BF16)
            cb = _dot_nt(cm, bm)
            hg = h_ref[d, g]
            y_off = _dot(cm, hg.astype(BF16))
            gs = slice(g * gw, (g + 1) * gw)
            h_ref[d, g] = hg * w_tot[:, gs] + _dot_tn(bm, xdec_b[:, gs])
            yield
            for pr in range(hpg // 2):
                h0 = g * hpg + 2 * pr
                cs = slice(h0 * SSM_HEAD_DIM, (h0 + 2) * SSM_HEAD_DIM)
                xp = xdt_b[:, cs]
                acc = None
                for k in range(2):
                    h = h0 + k
                    lm = jnp.where(tri, jnp.exp(e_col[:, h:h + 1] - e_row[h:h + 1, :]), 0.0)
                    mh = (cb * lm).astype(BF16)
                    sel = lo if k == 0 else jnp.logical_not(lo)
                    t = _dot(mh, jnp.where(sel, xp, jnp.zeros_like(xp)))
                    acc = t if acc is None else acc + t
                ys = slice(2 * pr * SSM_HEAD_DIM, (2 * pr + 2) * SSM_HEAD_DIM)
                y_parts.append(acc + y_off[:, ys] * w_off[:, cs])
                yield
        y = jnp.concatenate(y_parts, axis=-1)
        return y + dskip_ref[...] * x if fwd else y

    y_f, y_b = _run_lockstep([direction(0, xlf_ref, xcf_ref, dtlf_ref, dtcf_ref, dttlf_ref, dttcf_ref),
                              direction(1, xlb_ref, xcb_ref, dtlb_ref, dtcb_ref, dttlb_ref, dttcb_ref)])

    @pl.when(is_ctx)
    def _():
        ycf_ref[0] = y_f
        ycb_ref[0] = y_b

    @pl.when(jnp.logical_not(is_ctx))
    def _():
        ylf_ref[0] = y_f
        ylb_ref[0] = y_b


def _ssd_scan(xbc_l, xbc_c, dt_l, dt_c, a_log, d_skip):
    b, l, _ = xbc_l.shape
    cl = xbc_c.shape[1]
    q = SSM_CHUNK
    ncl, ncc = l // q, cl // q
    nc = ncl + ncc
    hh = SSM_HEADS
    dtt_l = jnp.swapaxes(dt_l[..., :2 * hh], 1, 2)
    dtt_c = jnp.swapaxes(dt_c[..., :2 * hh], 1, 2)
    expand = jnp.asarray(np.kron(np.eye(hh), np.ones((1, SSM_HEAD_DIM))), BF16)
    dskip = jnp.repeat(d_skip.astype(F32), SSM_HEAD_DIM).reshape(1, SSM_INNER)

    def lat_chunk(d, c):
        cc = jnp.maximum(c - ncc, 0)
        return cc if d == 0 else ncl - 1 - cc

    def ctx_chunk(d, c):
        cc = jnp.minimum(c, ncc - 1)
        return cc if d == 0 else ncc - 1 - cc

    def dir_specs(d):
        return [pl.BlockSpec((1, q, SSM_XBC), lambda i, c: (i, lat_chunk(d, c), 0)),
                pl.BlockSpec((1, q, SSM_XBC), lambda i, c: (i, ctx_chunk(d, c), 0)),
                pl.BlockSpec((1, q, LANES), lambda i, c: (i, lat_chunk(d, c), 0)),
                pl.BlockSpec((1, q, LANES), lambda i, c: (i, ctx_chunk(d, c), 0)),
                pl.BlockSpec((1, 2 * hh, q), lambda i, c: (i, 0, lat_chunk(d, c))),
                pl.BlockSpec((1, 2 * hh, q), lambda i, c: (i, 0, ctx_chunk(d, c)))]

    const = lambda i, c: (0, 0)
    in_specs = dir_specs(0) + dir_specs(1) + [
        pl.BlockSpec((2, hh), const), pl.BlockSpec((hh, 2), const),
        pl.BlockSpec((1, SSM_INNER), const), pl.BlockSpec((hh, SSM_INNER), const)]
    out_specs, out_shape = [], []
    for d in range(2):
        out_specs += [pl.BlockSpec((1, q, SSM_INNER), lambda i, c, d=d: (i, lat_chunk(d, c), 0)),
                      pl.BlockSpec((1, q, SSM_INNER), lambda i, c, d=d: (i, ctx_chunk(d, c), 0))]
        out_shape += [jax.ShapeDtypeStruct((b, l, SSM_INNER), F32), jax.ShapeDtypeStruct((b, cl, SSM_INNER), F32)]
    gw = (SSM_HEADS // SSM_GROUPS) * SSM_HEAD_DIM
    data = (xbc_l, xbc_c, dt_l, dt_c, dtt_l, dtt_c)
    yl_f, yc_f, yl_b, yc_b = pl.pallas_call(
        functools.partial(_ssd_kernel, ncc=ncc),
        grid=(b, nc),
        in_specs=in_specs,
        out_specs=out_specs,
        out_shape=out_shape,
        scratch_shapes=[pltpu.VMEM((2, SSM_GROUPS, SSM_STATE, gw), F32)],
        compiler_params=_cparams(("parallel", "arbitrary")),
        name="ssd_scan",
    )(*data, *data, a_log.astype(F32), a_log.astype(F32).T, dskip, expand)
    return (yl_f, yl_b), (yc_f, yc_b)


def _merge_kernel(x_ref, g1_ref, gate_ref, att_ref, ysf_ref, ysb_ref, z_ref, cv_ref, ng_ref,
                  wa_ref, ws_ref, wc_ref, bc_ref, wo_ref, o_ref):
    d = D_MODEL
    ys = (ysf_ref[0] + ysb_ref[0]) * _silu(z_ref[0].astype(F32))
    gsz = SSM_INNER // SSM_GROUPS
    parts = []
    for g in range(SSM_GROUPS):
        yg = ys[:, g * gsz:(g + 1) * gsz]
        parts.append(yg * lax.rsqrt(jnp.mean(yg * yg, axis=-1, keepdims=True) + EPS))
    yn = (jnp.concatenate(parts, axis=-1) * ng_ref[...]).astype(BF16)
    p_s = _dot(yn, ws_ref[...])
    p_a = _dot(att_ref[0], wa_ref[...])
    p_c = _dot(cv_ref[0], wc_ref[...]) + bc_ref[...]
    gl = gate_ref[0].astype(F32)
    m = (_sigmoid(gl[:, 0:d]) * p_a + _sigmoid(gl[:, d:2 * d]) * p_s
         + _sigmoid(gl[:, 2 * d:3 * d]) * p_c).astype(BF16)
    o_ref[0] = x_ref[0] + g1_ref[0] * _dot(m, wo_ref[...])


def _merge(x, g1, packed, y_att, y_ssm2, y_cv, norm_g, wa, ws, wc, bc, wo):
    b, l, d = x.shape
    tm = 256
    bm = g1.shape[0]
    g_map = (lambda i, t: (i, 0, 0)) if bm > 1 else (lambda i, t: (0, 0, 0))
    const = lambda i, t: (0, 0)
    one = pl.Buffered(1)
    in_specs = [pl.BlockSpec((1, tm, d), lambda i, t: (i, t, 0)),
                pl.BlockSpec((1, 1, d), g_map),
                pl.BlockSpec((1, tm, 3 * d), lambda i, t: (i, t, OFF_GATE // (3 * d))),
                pl.BlockSpec((1, tm, NA_WIDTH), lambda i, t: (i, t, 0)),
                pl.BlockSpec((1, tm, SSM_INNER), lambda i, t: (i, t, 0)),
                pl.BlockSpec((1, tm, SSM_INNER), lambda i, t: (i, t, 0)),
                pl.BlockSpec((1, tm, SSM_INNER), lambda i, t: (i, t, OFF_Z // SSM_INNER)),
                pl.BlockSpec((1, tm, CONV_CH), lambda i, t: (i, t, 0)),
                pl.BlockSpec((1, SSM_INNER), const),
                pl.BlockSpec((NA_WIDTH, d), const, pipeline_mode=one),
                pl.BlockSpec((SSM_INNER, d), const, pipeline_mode=one),
                pl.BlockSpec((CONV_CH, d), const, pipeline_mode=one),
                pl.BlockSpec((1, d), const),
                pl.BlockSpec((d, d), const, pipeline_mode=one)]
    return pl.pallas_call(
        _merge_kernel,
        grid=(b, l // tm),
        in_specs=in_specs,
        out_specs=pl.BlockSpec((1, tm, d), lambda i, t: (i, t, 0)),
        out_shape=jax.ShapeDtypeStruct((b, l, d), F32),
        compiler_params=_cparams(("parallel", "parallel")),
        name="merge",
    )(x, g1.reshape(bm, 1, d), packed, y_att, y_ssm2[0], y_ssm2[1], packed, y_cv, norm_g.reshape(1, -1),
      wa, ws, wc, bc.reshape(1, d), wo)


PEER_TQ = 256
NEG_INF = float("-inf")


def _sort_network(n):
    pairs = []

    def merge(lo, hi, r):
        step = r * 2
        if step < hi - lo:
            merge(lo, hi, step)
            merge(lo + r, hi, step)
            pairs.extend((i, i + r) for i in range(lo + r, hi - r, step))
        else:
            pairs.append((lo, lo + r))

    def sort(lo, hi):
        if hi - lo >= 1:
            mid = lo + (hi - lo) // 2
            sort(lo, mid)
            sort(mid + 1, hi)
            merge(lo, hi, 1)

    sort(0, n - 1)
    return pairs


def _exchange(v, i, j):
    a, b = v[i], v[j]
    v[i], v[j] = jnp.maximum(a, b), jnp.minimum(a, b)


def _top_k_sorted_many(problems):
    k = len(problems[0])
    vs = [list(v) for v in problems]
    for i, j in _sort_network(k):
        for v in vs:
            _exchange(v, i, j)
    for shift in (4, 2, 1):
        others = [[pltpu.roll(x, shift, axis=0) for x in v] for v in vs]
        vs = [[jnp.maximum(v[i], o[k - 1 - i]) for i in range(k)]
              for v, o in zip(vs, others)]
        dist = k // 2
        while dist >= 1:
            for i in range(k):
                if i & dist == 0:
                    for v in vs:
                        _exchange(v, i, i + dist)
            dist //= 2
    return vs


def _top_k_sorted(v):
    return _top_k_sorted_many([v])[0]


def _peer_select_kernel(q_ref, keys_ref, n1_ref, e1_ref, r2_ref, e2_ref, s_ref, top_ref):
    tq = PEER_TQ
    k = PEER_TOPK
    sub = lax.broadcasted_iota(jnp.int32, (8, tq), 0)

    def per_head_sets(h, carry):
        scores = []
        for i in (2 * h, 2 * h + 1):
            c0 = pl.multiple_of(i * PEER_KEY_DIM, PEER_KEY_DIM)
            s = _dot_nt(keys_ref[i], q_ref[:, pl.ds(c0, PEER_KEY_DIM)])
            s_ref[i] = s
            scores.append([s[8 * r:8 * r + 8, :] for r in range(PEER_NKEYS // 8)])
        tops = _top_k_sorted_many(scores)
        for n, i in enumerate((2 * h, 2 * h + 1)):
            for r in range(k):
                top_ref[i, r] = tops[n][r]
        return carry

    lax.fori_loop(0, PEER_HEADS, per_head_sets, 0)

    def spread(rows):
        out = rows[7]
        for s in range(6, -1, -1):
            out = jnp.where(sub == s, rows[s], out)
        return out

    def per_head(h, carry):
        t1 = [top_ref[2 * h, r] for r in range(k)]
        t2 = [top_ref[2 * h + 1, r] for r in range(k)]
        p2a, p2b, p1b = spread(t2[:8]), spread(t2[8:]), spread(t1[8:])
        cand = [t1[0] + p2a, t1[0] + p2b] + [t1[i] + p2a for i in range(1, 8)] + [p1b + t2[0]]
        cand += [jnp.full((8, tq), NEG_INF, F32)] * (k - len(cand))
        best = _top_k_sorted(cand)
        z = jnp.ones((8, tq), F32)
        for r in range(1, k):
            z = z + jnp.exp(best[r] - best[0])
        thr, rz = best[k - 1], 1.0 / z
        one, zero = jnp.ones((8, tq), F32), jnp.zeros((8, tq), F32)
        for g2 in range(PEER_NKEYS // 16):
            ranks, e2s = [], []
            for g in (2 * g2, 2 * g2 + 1):
                s1 = s_ref[2 * h, 8 * g:8 * g + 8, :]
                s2 = s_ref[2 * h + 1, 8 * g:8 * g + 8, :]
                n, r = zero, zero
                for jj in range(k):
                    n = n + jnp.where(s1 + t2[jj] >= thr, one, zero)
                    r = r + jnp.where(t2[jj] > s2, one, zero)
                n1_ref[h, g] = n
                e1_ref[h, g] = jnp.exp(s1 - t1[0]) * rz
                ranks.append(r)
                e2s.append(jnp.exp(s2 - t2[0]))
            r2_ref[h, g2] = jnp.concatenate(ranks, axis=0).astype(BF16)
            e2_ref[h, g2] = jnp.concatenate(e2s, axis=0).astype(BF16)
        return carry

    lax.fori_loop(0, PEER_HEADS, per_head, 0)


def _peer_select(q, keys):
    t = q.shape[0]
    tq = PEER_TQ
    nset = 2 * PEER_HEADS
    hh, g8, g16 = PEER_HEADS, PEER_NKEYS // 8, PEER_NKEYS // 16
    return pl.pallas_call(
        _peer_select_kernel,
        grid=(t // tq,),
        in_specs=[pl.BlockSpec((tq, q.shape[1]), lambda i: (i, 0)),
                  pl.BlockSpec((nset, PEER_NKEYS, PEER_KEY_DIM), lambda i: (0, 0, 0))],
        out_specs=[pl.BlockSpec((hh, g8, 8, tq), lambda i: (0, 0, 0, i)),
                   pl.BlockSpec((hh, g8, 8, tq), lambda i: (0, 0, 0, i)),
                   pl.BlockSpec((hh, g16, 16, tq), lambda i: (0, 0, 0, i)),
                   pl.BlockSpec((hh, g16, 16, tq), lambda i: (0, 0, 0, i))],
        out_shape=[jax.ShapeDtypeStruct((hh, g8, 8, t), F32),
                   jax.ShapeDtypeStruct((hh, g8, 8, t), F32),
                   jax.ShapeDtypeStruct((hh, g16, 16, t), BF16),
                   jax.ShapeDtypeStruct((hh, g16, 16, t), BF16)],
        scratch_shapes=[pltpu.VMEM((nset, PEER_NKEYS, tq), F32),
                        pltpu.VMEM((nset, PEER_TOPK, 8, tq), F32)],
        compiler_params=_cparams(("parallel",)),
        name="peer_select",
    )(q, keys)


PEER_TM = 512
PEER_TE = 1024
PEER_KEY_BLOCK = 32
PEER_ROW_BLOCK = 2


def _gelu(x):
    return 0.5 * x * (1.0 + lax.erf(x * np.float32(math.sqrt(0.5))))


def _peer_dense_kernel(h_ref, u_ref, vt_ref, n1_ref, e1_ref, r2_ref, e2_ref, x_ref, g2_ref, o_ref,
                       acc_ref, at_ref, wa_ref, bcn_ref, bce_ref):
    j = pl.program_id(1)
    nsteps = pl.num_programs(1)
    tm = PEER_TM
    te = PEER_TE
    nk = PEER_NKEYS

    @pl.when(j == 0)
    def _():
        acc_ref[...] = jnp.zeros_like(acc_ref)

    na = te // nk
    kb = PEER_KEY_BLOCK
    kg = kb // 16
    nkq = nk // kb

    at_ref[...] = _dot_nt(u_ref[...], h_ref[...])

    for h, al in itertools.product(range(PEER_HEADS), range(na)):
        bcn_ref[h, al] = jnp.broadcast_to(n1_ref[h, j, al:al + 1, :], (16, tm)).astype(BF16)
        bce_ref[h, al] = jnp.broadcast_to(e1_ref[h, j, al:al + 1, :], (16, tm)).astype(BF16)

    def gate_tile(it, carry):
        lc = it // (na // PEER_ROW_BLOCK)
        ab = it % (na // PEER_ROW_BLOCK)
        ls = pl.ds(pl.multiple_of(lc * LANES, LANES), LANES)
        a0 = ab * PEER_ROW_BLOCK
        nt = nkq * kg
        w = {(r, t): jnp.zeros((16, LANES), BF16) for r in range(PEER_ROW_BLOCK) for t in range(nt)}
        for h in range(PEER_HEADS):
            r2 = [r2_ref[h, t, :, ls] for t in range(nt)]
            e2 = [e2_ref[h, t, :, ls] for t in range(nt)]
            for r in range(PEER_ROW_BLOCK):
                n1 = bcn_ref[h, a0 + r, :, ls]
                e1 = bce_ref[h, a0 + r, :, ls]
                for t in range(nt):
                    w[r, t] = w[r, t] + jnp.where(r2[t] < n1, e2[t], jnp.zeros_like(e2[t])) * e1
        for r, kq in itertools.product(range(PEER_ROW_BLOCK), range(nkq)):
            rs = pl.ds(pl.multiple_of((a0 + r) * nk + kq * kb, kb), kb)
            wt = jnp.concatenate([w[r, kq * kg + t] for t in range(kg)], axis=0)
            wa_ref[rs, ls] = (wt.astype(F32) * _gelu(at_ref[rs, ls])).astype(BF16)
        return carry

    lax.fori_loop(0, (tm // LANES) * (na // PEER_ROW_BLOCK), gate_tile, 0)

    acc_ref[...] += _dot(vt_ref[0], wa_ref[...])

    @pl.when(j == nsteps - 1)
    def _():
        o_ref[...] = x_ref[...] + g2_ref[0] * acc_ref[...].T


def _peer_dense(hh, u, vt, sel, x, g2, tiles_per_row):
    t, d = hh.shape
    ne = u.shape[0]
    tm, te = PEER_TM, PEER_TE
    one = pl.Buffered(1)
    na = te // PEER_NKEYS
    hd, g8, g16 = PEER_HEADS, PEER_NKEYS // 8, PEER_NKEYS // 16
    sel_map = lambda i, j: (0, 0, 0, i)
    return pl.pallas_call(
        _peer_dense_kernel,
        grid=(t // tm, ne // te),
        in_specs=[pl.BlockSpec((tm, d), lambda i, j: (i, 0), pipeline_mode=one),
                  pl.BlockSpec((te, d), lambda i, j: (j, 0)),
                  pl.BlockSpec((1, d, te), lambda i, j: (j, 0, 0)),
                  pl.BlockSpec((hd, g8, 8, tm), sel_map, pipeline_mode=one),
                  pl.BlockSpec((hd, g8, 8, tm), sel_map, pipeline_mode=one),
                  pl.BlockSpec((hd, g16, 16, tm), sel_map, pipeline_mode=one),
                  pl.BlockSpec((hd, g16, 16, tm), sel_map, pipeline_mode=one),
                  pl.BlockSpec((tm, d), lambda i, j: (i, 0), pipeline_mode=one),
                  pl.BlockSpec((1, 1, d), lambda i, j: (i // tiles_per_row, 0, 0))],
        out_specs=pl.BlockSpec((tm, d), lambda i, j: (i, 0)),
        out_shape=jax.ShapeDtypeStruct((t, d), F32),
        scratch_shapes=[pltpu.VMEM((d, tm), F32),
                        pltpu.VMEM((te, tm), F32),
                        pltpu.VMEM((te, tm), BF16),
                        pltpu.VMEM((hd, na, 16, tm), BF16),
                        pltpu.VMEM((hd, na, 16, tm), BF16)],
        compiler_params=_cparams(("parallel", "arbitrary")),
        name="peer_dense",
    )(hh, u, vt, *sel, x, g2)


def _peer_block(x, g, shift, scale, gate2, wq, keys, u, vt):
    b, l, d = x.shape
    t = b * l
    x_rows = x if shift.shape[0] > 1 else x.reshape(1, t, d)
    q, hh = _norm_mod_matmul(x_rows, g, shift, scale, wq, BF16, 512, emit_h=True)
    sel = _peer_select(q.reshape(t, -1), keys)
    bm = gate2.shape[0]
    tiles_per_row = (l // PEER_TM) if bm > 1 else (t // PEER_TM)
    out = _peer_dense(hh.reshape(t, d), u, vt, sel, x.reshape(t, d), gate2.reshape(bm, 1, d), tiles_per_row)
    return out.reshape(b, l, d)


def _rmsnorm_kernel(x_ref, g_ref, o_ref):
    x = x_ref[...]
    o_ref[...] = x * lax.rsqrt(jnp.mean(x * x, axis=-1, keepdims=True) + EPS) * g_ref[...]


def _rmsnorm(x, g):
    t, d = x.shape
    tm = 512
    return pl.pallas_call(
        _rmsnorm_kernel,
        grid=(t // tm,),
        in_specs=[pl.BlockSpec((tm, d), lambda i: (i, 0)), pl.BlockSpec((1, d), lambda i: (0, 0))],
        out_specs=pl.BlockSpec((tm, d), lambda i: (i, 0)),
        out_shape=jax.ShapeDtypeStruct((t, d), F32),
        compiler_params=_cparams(("parallel",)),
        name="final_rmsnorm",
    )(x, g.reshape(1, d))


def _pack_w_in(w_in):
    q, k, v, z, xbc, dt, glu, gate = jnp.split(
        w_in, np.cumsum([NA_WIDTH, NA_WIDTH, NA_WIDTH, SSM_INNER, SSM_XBC, 2 * SSM_HEADS, 2 * CONV_CH]).tolist(),
        axis=-1)
    packed = jnp.concatenate([gate, glu, q, k, v, z, xbc], axis=-1).astype(BF16)
    dt_w = jnp.pad(dt, ((0, 0), (0, LANES - 2 * SSM_HEADS))).astype(BF16)
    return packed, dt_w


def _trunk_layer(xl, xc, c_rows, need_ctx, w_mod, b_mod, norm1_g, norm2_g, w_in, na_rpb, na_wo,
                 ssm_conv_w, ssm_conv_b, ssm_dt_bias, ssm_a_log, ssm_d, ssm_norm_g, ssm_wo,
                 cv_dw_w, cv_dw_b, cv_ln_g, cv_ln_b, cv_wo, cv_bo, w_out, peer_wq, peer_keys, peer_u, peer_v):
    b, s, d = xl.shape
    mod = _modulation(c_rows, w_mod, b_mod)
    sh1, sc1, g1, sh2, sc2, g2 = [mod[:b, i * d:(i + 1) * d] for i in range(6)]
    csh1, csc1, cg1, csh2, csc2, cg2 = [mod[b:b + 1, i * d:(i + 1) * d] for i in range(6)]

    w_pack, w_dt = _pack_w_in(w_in)
    dt_bias = jnp.pad(ssm_dt_bias.astype(F32).reshape(-1), (0, LANES - 2 * SSM_HEADS))

    cb, cl, _ = xc.shape
    xc_flat = xc.reshape(1, cb * cl, d)
    p_l = _norm_mod_matmul(xl, norm1_g, sh1, sc1, w_pack, BF16, 512)
    p_c = _norm_mod_matmul(xc_flat, norm1_g, csh1, csc1, w_pack, BF16, 512).reshape(cb, cl, -1)
    dt_l = _norm_mod_matmul(xl, norm1_g, sh1, sc1, w_dt, F32, LANES, softplus_bias=dt_bias)
    dt_c = _norm_mod_matmul(xc_flat, norm1_g, csh1, csc1, w_dt, F32, LANES,
                            softplus_bias=dt_bias).reshape(cb, cl, -1)

    bias_tab = _na_bias_table(na_rpb, s // GRID_W)
    y_att, y_att_c = _neighbourhood_attention(p_l, p_c, bias_tab, need_ctx)

    xbc_l = _dwconv(p_l, OFF_XBC, None, SSM_XBC, ssm_conv_w, ssm_conv_b, None, None, F32)
    xbc_c = _dwconv(p_c, OFF_XBC, None, SSM_XBC, ssm_conv_w, ssm_conv_b, None, None, F32)
    y_ssm, y_ssm_c = _ssd_scan(xbc_l, xbc_c, dt_l, dt_c, ssm_a_log, ssm_d)

    wa, ws, wc, wo = (na_wo.astype(BF16), ssm_wo.astype(BF16), cv_wo.astype(BF16), w_out.astype(BF16))
    y_cv = _dwconv(p_l, OFF_GLU_A, OFF_GLU_G, CONV_CH, cv_dw_w, cv_dw_b, cv_ln_g, cv_ln_b, BF16)
    xl = _merge(xl, g1, p_l, y_att, y_ssm, y_cv, ssm_norm_g, wa, ws, wc, cv_bo, wo)
    if need_ctx:
        y_cv_c = _dwconv(p_c, OFF_GLU_A, OFF_GLU_G, CONV_CH, cv_dw_w, cv_dw_b, cv_ln_g, cv_ln_b, BF16)
        xc = _merge(xc, cg1, p_c, y_att_c, y_ssm_c, y_cv_c, ssm_norm_g, wa, ws, wc, cv_bo, wo)

    wq = peer_wq.astype(BF16)
    keys = peer_keys.astype(BF16).reshape(2 * PEER_HEADS, PEER_NKEYS, PEER_KEY_DIM)
    u = peer_u.astype(BF16)
    vt = peer_v.astype(BF16).reshape(PEER_EXPERTS // PEER_TE, PEER_TE, d).transpose(0, 2, 1)
    xl = _peer_block(xl, norm2_g, sh2, sc2, g2, wq, keys, u, vt)
    if need_ctx:
        xc = _peer_block(xc, norm2_g, csh2, csc2, cg2, wq, keys, u, vt)
    return xl, xc


def kernel(x, c, ctx, c_ctx, w_mod, b_mod, norm1_g, norm2_g, w_in, na_rpb, na_wo, ssm_conv_w, ssm_conv_b,
           ssm_dt_bias, ssm_A_log, ssm_D, ssm_norm_g, ssm_wo, cv_dw_w, cv_dw_b, cv_ln_g, cv_ln_b, cv_wo,
           cv_bo, w_out, peer_wq, peer_keys, peer_u, peer_v, final_norm_g):
    b, s, d = x.shape
    depth = w_mod.shape[0]
    rows = -(-(b + 1) // 8) * 8
    c_rows = jnp.zeros((rows, d), F32).at[:b].set(c).at[b].set(c_ctx)
    xl, xc = x, ctx
    for l in range(depth):
        xl, xc = _trunk_layer(xl, xc, c_rows, l < depth - 1, w_mod[l], b_mod[l], norm1_g[l], norm2_g[l],
                              w_in[l], na_rpb[l], na_wo[l], ssm_conv_w[l], ssm_conv_b[l], ssm_dt_bias[l],
                              ssm_A_log[l], ssm_D[l], ssm_norm_g[l], ssm_wo[l], cv_dw_w[l], cv_dw_b[l],
                              cv_ln_g[l], cv_ln_b[l], cv_wo[l], cv_bo[l], w_out[l],
                              peer_wq[l], peer_keys[l], peer_u[l], peer_v[l])
    return _rmsnorm(xl.reshape(b * s, d), final_norm_g).reshape(b, s, d)
```

```python
import functools
import itertools
import math

import numpy as np
import jax
import jax.numpy as jnp
from jax import lax
from jax.experimental import pallas as pl
from jax.experimental.pallas import tpu as pltpu

F32 = jnp.float32
BF16 = jnp.bfloat16
EPS = 1e-6

D_MODEL = 2048
GRID_W = 64
NA_HEADS = 16
NA_HEAD_DIM = 64
NA_WIDTH = NA_HEADS * NA_HEAD_DIM
WIN_R = 8
WIN_C = 16
SSM_HEADS = 16
SSM_HEAD_DIM = 64
SSM_INNER = SSM_HEADS * SSM_HEAD_DIM
SSM_GROUPS = 2
SSM_STATE = 128
SSM_XBC = SSM_INNER + 2 * SSM_GROUPS * SSM_STATE
SSM_CONV = 5
SSM_CHUNK = 128
CONV_CH = 1024
CONV_K = 31
PEER_HEADS = 8
PEER_NKEYS = 128
PEER_KEY_DIM = 128
PEER_TOPK = 16
PEER_EXPERTS = PEER_NKEYS * PEER_NKEYS

OFF_GATE = 0
OFF_GLU_A = 6144
OFF_GLU_G = 7168
OFF_Q = 8192
OFF_K = 9216
OFF_V = 10240
OFF_Z = 11264
OFF_XBC = 12288
PACK_COLS = 13824

LANES = 128
VMEM_LIMIT = 56 * 1024 * 1024


def _cparams(sem):
    return pltpu.CompilerParams(dimension_semantics=sem, vmem_limit_bytes=VMEM_LIMIT)


def _dot(a, b):
    return jnp.dot(a, b, preferred_element_type=F32)


def _dot_nt(a, b):
    return lax.dot_general(a, b, (((1,), (1,)), ((), ())), preferred_element_type=F32)


def _dot_tn(a, b):
    return lax.dot_general(a, b, (((0,), (0,)), ((), ())), preferred_element_type=F32)


def _split3(a):
    hi = a.astype(BF16)
    r1 = a - hi.astype(F32)
    mid = r1.astype(BF16)
    lo = (r1 - mid.astype(F32)).astype(BF16)
    return hi, mid, lo


def _dot_exact_rhs01(a, m01):
    hi, mid, lo = _split3(a)
    return _dot(hi, m01) + _dot(mid, m01) + _dot(lo, m01)


def _dot_exact_lhs01(m01, a):
    hi, mid, lo = _split3(a)
    return _dot(m01, hi) + _dot(m01, mid) + _dot(m01, lo)


def _sigmoid(x):
    return 1.0 / (1.0 + jnp.exp(-x))


def _silu(x):
    return x * _sigmoid(x)


def _mod_kernel(c_ref, w_ref, b_ref, o_ref):
    a = _silu(c_ref[...]).astype(BF16)
    o_ref[...] = _dot(a, w_ref[...].astype(BF16)) + b_ref[...]


def _modulation(cs, w_mod, b_mod):
    r, d = cs.shape
    n = w_mod.shape[1]
    tn = 1024
    return pl.pallas_call(
        _mod_kernel,
        grid=(n // tn,),
        in_specs=[pl.BlockSpec((r, d), lambda j: (0, 0)),
                  pl.BlockSpec((d, tn), lambda j: (0, j)),
                  pl.BlockSpec((1, tn), lambda j: (0, j))],
        out_specs=pl.BlockSpec((r, tn), lambda j: (0, j)),
        out_shape=jax.ShapeDtypeStruct((r, n), F32),
        compiler_params=_cparams(("arbitrary",)),
        name="modulation",
    )(cs, w_mod, b_mod.reshape(1, n))


def _nmm_kernel(x_ref, g_ref, sh_ref, sc_ref, w_ref, *rest, softplus_bias, emit_h):
    if softplus_bias:
        bias_ref, rest = rest[0], rest[1:]
    if emit_h:
        o_ref, ho_ref, h_ref = rest
    else:
        o_ref, h_ref = rest

    @pl.when(pl.program_id(2) == 0)
    def _():
        x = x_ref[0]
        ms = jnp.mean(x * x, axis=-1, keepdims=True)
        y = x * lax.rsqrt(ms + EPS) * g_ref[...]
        h = (y * (1.0 + sc_ref[0]) + sh_ref[0]).astype(BF16)
        h_ref[...] = h
        if emit_h:
            ho_ref[0] = h

    acc = _dot(h_ref[...], w_ref[0])
    if softplus_bias:
        t = acc + bias_ref[...]
        acc = jnp.maximum(t, 0.0) + jnp.log1p(jnp.exp(-jnp.abs(t)))
    o_ref[0] = acc.astype(o_ref.dtype)


def _column_blocks(w, tn):
    d, n = w.shape
    return w.reshape(d, n // tn, tn).transpose(1, 0, 2)


def _norm_mod_matmul(x, g, shift, scale, w, out_dtype, tn, softplus_bias=None, emit_h=False):
    b, l, d = x.shape
    n = w.shape[1]
    w = _column_blocks(w, tn)
    tm = min(l, 1024)
    bm = shift.shape[0]
    mod_map = (lambda i, m, j: (i, 0, 0)) if bm > 1 else (lambda i, m, j: (0, 0, 0))
    in_specs = [pl.BlockSpec((1, tm, d), lambda i, m, j: (i, m, 0)),
                pl.BlockSpec((1, d), lambda i, m, j: (0, 0)),
                pl.BlockSpec((1, 1, d), mod_map),
                pl.BlockSpec((1, 1, d), mod_map),
                pl.BlockSpec((1, d, tn), lambda i, m, j: (j, 0, 0))]
    args = [x, g.reshape(1, d), shift.reshape(bm, 1, d), scale.reshape(bm, 1, d), w]
    if softplus_bias is not None:
        in_specs.append(pl.BlockSpec((1, tn), lambda i, m, j: (0, j)))
        args.append(softplus_bias.reshape(1, n))
    out_shape = [jax.ShapeDtypeStruct((b, l, n), out_dtype)]
    out_specs = [pl.BlockSpec((1, tm, tn), lambda i, m, j: (i, m, j))]
    if emit_h:
        out_shape.append(jax.ShapeDtypeStruct((b, l, d), BF16))
        out_specs.append(pl.BlockSpec((1, tm, d), lambda i, m, j: (i, m, 0)))
    res = pl.pallas_call(
        functools.partial(_nmm_kernel, softplus_bias=softplus_bias is not None, emit_h=emit_h),
        grid=(b, l // tm, n // tn),
        in_specs=in_specs,
        out_specs=out_specs,
        out_shape=out_shape,
        scratch_shapes=[pltpu.VMEM((tm, d), BF16)],
        compiler_params=_cparams(("parallel", "parallel", "arbitrary")),
        name="norm_mod_matmul",
    )(*args)
    return res if emit_h else res[0]


def _na_bias_table(rpb, rows):
    wr = min(WIN_R, rows)
    j = np.arange(GRID_W)
    cstart = np.clip(j - WIN_C // 2, 0, GRID_W - WIN_C)
    kc = np.arange(GRID_W)
    mask = (kc[None, :] >= cstart[:, None]) & (kc[None, :] < cstart[:, None] + WIN_C)
    col_off = np.clip(kc[None, :] - j[:, None], -(WIN_C - 1), WIN_C - 1) + WIN_C - 1
    dd = np.arange(wr)[:, None]
    ww = np.arange(wr)[None, :]
    row_idx = ww - dd + WIN_R - 1
    t = rpb.astype(F32)[:, row_idx][:, :, :, col_off]
    t = jnp.where(mask[None, None, None], t, -1e30)
    t = t.transpose(0, 1, 3, 2, 4).reshape(NA_HEADS, wr, GRID_W, wr * GRID_W)
    return t.reshape(NA_HEADS // 2, 2, wr, GRID_W, wr * GRID_W)


NA_ROW_GROUP = 8


def _na_kernel(q_ref, k_ref, v_ref, kc_ref, vc_ref, bias_ref, *rest, rows, wr, with_ctx):
    if with_ctx:
        qc_ref, o_ref, oc_ref = rest
    else:
        (o_ref,) = rest
    lane = lax.broadcasted_iota(jnp.int32, (1, LANES), 1)
    lo = lane < NA_HEAD_DIM
    kc = kc_ref[0]
    vc = vc_ref[0]
    scale = NA_HEAD_DIM ** -0.5
    nwin = wr * GRID_W

    def attend(problems):
        chains = []
        for q, kw, vw, bias_of in problems:
            for h in range(2):
                sel = lo if h == 0 else jnp.logical_not(lo)
                qh = jnp.where(sel, q, jnp.zeros_like(q))
                s_c = _dot_nt(qh, kc) * scale
                s_w = None if kw is None else _dot_nt(qh, kw) * scale + bias_of(h)
                chains.append((s_c, s_w, vw))
        maxes = []
        for s_c, s_w, _ in chains:
            m = jnp.max(s_c, axis=-1, keepdims=True)
            if s_w is not None:
                m = jnp.maximum(m, jnp.max(s_w, axis=-1, keepdims=True))
            maxes.append(m)
        probs = []
        for (s_c, s_w, _), m in zip(chains, maxes):
            probs.append((jnp.exp(s_c - m), None if s_w is None else jnp.exp(s_w - m)))
        outs = []
        for (_, _, vw), (p_c, p_w) in zip(chains, probs):
            den = jnp.sum(p_c, axis=-1, keepdims=True)
            o = _dot(p_c.astype(BF16), vc)
            if p_w is not None:
                den = den + jnp.sum(p_w, axis=-1, keepdims=True)
                o = o + _dot(p_w.astype(BF16), vw)
            outs.append(o / den)
        return [jnp.where(lo, outs[2 * i], outs[2 * i + 1]) for i in range(len(problems))]

    def body(g, carry):
        problems, q0s = [], []
        for rr in range(NA_ROW_GROUP):
            r = g * NA_ROW_GROUP + rr
            rs = jnp.clip(r - wr // 2, 0, rows - wr)
            d = r - rs
            q0 = pl.multiple_of(r * GRID_W, GRID_W)
            k0 = pl.multiple_of(rs * GRID_W, GRID_W)
            problems.append((q_ref[0, pl.ds(q0, GRID_W), :], k_ref[0, pl.ds(k0, nwin), :],
                             v_ref[0, pl.ds(k0, nwin), :], lambda h, d=d: bias_ref[0, h, d]))
            q0s.append(q0)
        for q0, o in zip(q0s, attend(problems)):
            o_ref[0, pl.ds(q0, GRID_W), :] = o.astype(o_ref.dtype)
        return carry

    lax.fori_loop(0, rows // NA_ROW_GROUP, body, 0)
    if with_ctx:
        oc_ref[0] = attend([(qc_ref[0], None, None, None)])[0].astype(oc_ref.dtype)


def _neighbourhood_attention(pl_lat, pl_ctx, bias_tab, with_ctx):
    b, s, _ = pl_lat.shape
    cl = pl_ctx.shape[1]
    rows = s // GRID_W
    wr = min(WIN_R, rows)
    nhp = NA_HEADS // 2
    qb, kb, vb = OFF_Q // LANES, OFF_K // LANES, OFF_V // LANES
    in_specs = [pl.BlockSpec((1, s, LANES), lambda hp, i: (i, 0, qb + hp)),
                pl.BlockSpec((1, s, LANES), lambda hp, i: (i, 0, kb + hp)),
                pl.BlockSpec((1, s, LANES), lambda hp, i: (i, 0, vb + hp)),
                pl.BlockSpec((1, cl, LANES), lambda hp, i: (i, 0, kb + hp)),
                pl.BlockSpec((1, cl, LANES), lambda hp, i: (i, 0, vb + hp)),
                pl.BlockSpec((1, 2, wr, GRID_W, wr * GRID_W), lambda hp, i: (hp, 0, 0, 0, 0))]
    args = [pl_lat, pl_lat, pl_lat, pl_ctx, pl_ctx, bias_tab]
    out_shape = [jax.ShapeDtypeStruct((b, s, NA_WIDTH), BF16)]
    out_specs = [pl.BlockSpec((1, s, LANES), lambda hp, i: (i, 0, hp))]
    if with_ctx:
        in_specs.append(pl.BlockSpec((1, cl, LANES), lambda hp, i: (i, 0, qb + hp)))
        args.append(pl_ctx)
        out_shape.append(jax.ShapeDtypeStruct((b, cl, NA_WIDTH), BF16))
        out_specs.append(pl.BlockSpec((1, cl, LANES), lambda hp, i: (i, 0, hp)))
    res = pl.pallas_call(
        functools.partial(_na_kernel, rows=rows, wr=wr, with_ctx=with_ctx),
        grid=(nhp, b),
        in_specs=in_specs,
        out_specs=out_specs,
        out_shape=out_shape,
        compiler_params=_cparams(("parallel", "parallel")),
        name="neighbourhood_attention",
    )(*args)
    return (res[0], res[1]) if with_ctx else (res[0], None)


CONV_TL = 256
CONV_HALO = 16


def _dwconv_kernel(*refs, taps, ch, glu, tl):
    if glu:
        (a_p, a_c, a_n, g_p, g_c, g_n, w_ref, b_ref, lng_ref, lnb_ref, o_ref, u_ref, y_ref, ur_ref) = refs
    else:
        (a_p, a_c, a_n, w_ref, b_ref, o_ref, u_ref) = refs
    i = pl.program_id(1)
    n = pl.num_programs(1)
    pad = (taps - 1) // 2
    hl = CONV_HALO

    def pre(a, g):
        a = a.astype(F32)
        if glu:
            return a * _sigmoid(g.astype(F32))
        return a

    u_ref[hl:hl + tl, :] = pre(a_c[0], g_c[0] if glu else None)
    top = pre(a_p[0, tl - hl:tl, :], g_p[0, tl - hl:tl, :] if glu else None)
    u_ref[0:hl, :] = jnp.where(i > 0, top, 0.0)
    bot = pre(a_n[0, 0:hl, :], g_n[0, 0:hl, :] if glu else None)
    u_ref[hl + tl:hl + tl + hl, :] = jnp.where(i < n - 1, bot, 0.0)

    if glu:
        span = tl + 2 * hl - 8
        for r in range(8):
            ur_ref[r] = u_ref[r:r + span, :]

    for cc in range(ch // LANES):
        cs = slice(cc * LANES, (cc + 1) * LANES)
        acc = jnp.zeros((tl, LANES), F32) + b_ref[:, cs]
        for k in range(taps):
            off = hl - pad + k
            if glu:
                rows = ur_ref[off % 8, 8 * (off // 8):8 * (off // 8) + tl, cs]
            else:
                rows = u_ref[off:off + tl, cs]
            acc = acc + w_ref[k:k + 1, cs] * rows
        if glu:
            y_ref[:, cs] = acc
        else:
            o_ref[0, :, cs] = _silu(acc).astype(o_ref.dtype)

    if glu:
        y = y_ref[...]
        mu = jnp.mean(y, axis=-1, keepdims=True)
        yc = y - mu
        var = jnp.mean(yc * yc, axis=-1, keepdims=True)
        z = yc * lax.rsqrt(var + EPS) * lng_ref[...] + lnb_ref[...]
        o_ref[0] = _silu(z).astype(o_ref.dtype)


def _dwconv(src, off_a, off_g, ch, w, b, ln_g, ln_b, out_dtype):
    bsz, l, _ = src.shape
    taps = w.shape[0]
    glu = off_g is not None
    tl = CONV_TL
    nt = l // tl
    ca = off_a // ch

    def spec(cb, delta):
        def imap(i, t):
            return (i, jnp.clip(t + delta, 0, nt - 1), cb)
        return pl.BlockSpec((1, tl, ch), imap)

    in_specs = [spec(ca, -1), spec(ca, 0), spec(ca, 1)]
    args = [src, src, src]
    if glu:
        cg = off_g // ch
        in_specs += [spec(cg, -1), spec(cg, 0), spec(cg, 1)]
        args += [src, src, src]
    in_specs += [pl.BlockSpec((taps, ch), lambda i, t: (0, 0)), pl.BlockSpec((1, ch), lambda i, t: (0, 0))]
    args += [w, b.reshape(1, ch)]
    scratch = [pltpu.VMEM((tl + 2 * CONV_HALO, ch), F32)]
    if glu:
        in_specs += [pl.BlockSpec((1, ch), lambda i, t: (0, 0)), pl.BlockSpec((1, ch), lambda i, t: (0, 0))]
        args += [ln_g.reshape(1, ch), ln_b.reshape(1, ch)]
        scratch.append(pltpu.VMEM((tl, ch), F32))
        scratch.append(pltpu.VMEM((8, tl + 2 * CONV_HALO - 8, ch), F32))
    return pl.pallas_call(
        functools.partial(_dwconv_kernel, taps=taps, ch=ch, glu=glu, tl=tl),
        grid=(bsz, nt),
        in_specs=in_specs,
        out_specs=pl.BlockSpec((1, tl, ch), lambda i, t: (i, t, 0)),
        out_shape=jax.ShapeDtypeStruct((bsz, l, ch), out_dtype),
        scratch_shapes=scratch,
        compiler_params=_cparams(("parallel", "parallel")),
        name="dwconv_glu" if glu else "dwconv_ssm",
    )(*args)


def _run_lockstep(coroutines):
    out = [None] * len(coroutines)
    live = list(range(len(coroutines)))
    while live:
        for i in list(live):
            try:
                next(coroutines[i])
            except StopIteration as done:
                out[i] = done.value
                live.remove(i)
    return out


def _ssd_kernel(xlf_ref, xcf_ref, dtlf_ref, dtcf_ref, dttlf_ref, dttcf_ref,
                xlb_ref, xcb_ref, dtlb_ref, dtcb_ref, dttlb_ref, dttcb_ref,
                alog_ref, alogt_ref, dskip_ref, exp_ref,
                ylf_ref, ycf_ref, ylb_ref, ycb_ref, h_ref, *, ncc):
    c = pl.program_id(1)
    q = SSM_CHUNK
    hh = SSM_HEADS
    hpg = SSM_HEADS // SSM_GROUPS
    gw = hpg * SSM_HEAD_DIM
    n = SSM_STATE
    is_ctx = c < ncc

    @pl.when(c == 0)
    def _():
        h_ref[...] = jnp.zeros_like(h_ref)

    li = lax.broadcasted_iota(jnp.int32, (q, q), 0)
    si = lax.broadcasted_iota(jnp.int32, (q, q), 1)
    lane = lax.broadcasted_iota(jnp.int32, (1, LANES), 1)
    lo = lane < SSM_HEAD_DIM
    expand = exp_ref[...]
    alog = alog_ref[...]
    alogt = alogt_ref[...]

    def direction(d, xl_ref, xc_ref, dtl_ref, dtc_ref, dttl_ref, dttc_ref):
        fwd = d == 0
        xbc = jnp.where(is_ctx, xc_ref[0], xl_ref[0]).astype(F32)
        x = xbc[:, :SSM_INNER]
        dt2 = jnp.where(is_ctx, dtc_ref[0], dtl_ref[0])
        dt_col = dt2[:, d * hh:(d + 1) * hh]
        dtt2 = jnp.where(is_ctx, dttc_ref[0], dttl_ref[0])
        dt_row = dtt2[d * hh:(d + 1) * hh, :]
        a_col = dt_col * -jnp.exp(alog[d:d + 1, :])
        a_row = dt_row * -jnp.exp(alogt[:, d:d + 1])

        tri = (si <= li) if fwd else (si >= li)
        trit = (li <= si) if fwd else (li >= si)
        tri_b = jnp.where(tri, 1.0, 0.0).astype(BF16)
        trit_b = jnp.where(trit, 1.0, 0.0).astype(BF16)
        e_col = _dot_exact_lhs01(tri_b, a_col)
        e_row = _dot_exact_rhs01(a_row, trit_b)
        last = q - 1 if fwd else 0
        e_tot = e_col[last:last + 1, :]
        yield

        w_dt = _dot_exact_rhs01(dt_col, expand)
        w_dec = _dot_exact_rhs01(jnp.exp(e_tot - e_col), expand)
        w_off = _dot_exact_rhs01(jnp.exp(e_col), expand)
        w_tot = w_off[last:last + 1, :]

        xdt = x * w_dt
        xdt_b = xdt.astype(BF16)
        xdec_b = (xdt * w_dec).astype(BF16)
        yield

        y_parts = []
        for g in range(SSM_GROUPS):
            bm = xbc[:, SSM_INNER + g * n:SSM_INNER + (g + 1) * n].astype(BF16)
            cm = xbc[:, SSM_INNER + (SSM_GROUPS + g) * n:SSM_INNER + (SSM_GROUPS + g + 1) * n].astype(BF16)
            cb = _dot_nt(cm, bm)
            hg = h_ref[d, g]
            y_off = _dot(cm, hg.astype(BF16))
            gs = slice(g * gw, (g + 1) * gw)
            h_ref[d, g] = hg * w_tot[:, gs] + _dot_tn(bm, xdec_b[:, gs])
            yield
            for pr in range(hpg // 2):
                h0 = g * hpg + 2 * pr
                cs = slice(h0 * SSM_HEAD_DIM, (h0 + 2) * SSM_HEAD_DIM)
                xp = xdt_b[:, cs]
                acc = None
                for k in range(2):
                    h = h0 + k
                    lm = jnp.where(tri, jnp.exp(e_col[:, h:h + 1] - e_row[h:h + 1, :]), 0.0)
                    mh = (cb * lm).astype(BF16)
                    sel = lo if k == 0 else jnp.logical_not(lo)
                    t = _dot(mh, jnp.where(sel, xp, jnp.zeros_like(xp)))
                    acc = t if acc is None else acc + t
                ys = slice(2 * pr * SSM_HEAD_DIM, (2 * pr + 2) * SSM_HEAD_DIM)
                y_parts.append(acc + y_off[:, ys] * w_off[:, cs])
                yield
        y = jnp.concatenate(y_parts, axis=-1)
        return y + dskip_ref[...] * x if fwd else y

    y_f, y_b = _run_lockstep([direction(0, xlf_ref, xcf_ref, dtlf_ref, dtcf_ref, dttlf_ref, dttcf_ref),
                              direction(1, xlb_ref, xcb_ref, dtlb_ref, dtcb_ref, dttlb_ref, dttcb_ref)])

    @pl.when(is_ctx)
    def _():
        ycf_ref[0] = y_f
        ycb_ref[0] = y_b

    @pl.when(jnp.logical_not(is_ctx))
    def _():
        ylf_ref[0] = y_f
        ylb_ref[0] = y_b


def _ssd_scan(xbc_l, xbc_c, dt_l, dt_c, a_log, d_skip):
    b, l, _ = xbc_l.shape
    cl = xbc_c.shape[1]
    q = SSM_CHUNK
    ncl, ncc = l // q, cl // q
    nc = ncl + ncc
    hh = SSM_HEADS
    dtt_l = jnp.swapaxes(dt_l[..., :2 * hh], 1, 2)
    dtt_c = jnp.swapaxes(dt_c[..., :2 * hh], 1, 2)
    expand = jnp.asarray(np.kron(np.eye(hh), np.ones((1, SSM_HEAD_DIM))), BF16)
    dskip = jnp.repeat(d_skip.astype(F32), SSM_HEAD_DIM).reshape(1, SSM_INNER)

    def lat_chunk(d, c):
        cc = jnp.maximum(c - ncc, 0)
        return cc if d == 0 else ncl - 1 - cc

    def ctx_chunk(d, c):
        cc = jnp.minimum(c, ncc - 1)
        return cc if d == 0 else ncc - 1 - cc

    def dir_specs(d):
        return [pl.BlockSpec((1, q, SSM_XBC), lambda i, c: (i, lat_chunk(d, c), 0)),
                pl.BlockSpec((1, q, SSM_XBC), lambda i, c: (i, ctx_chunk(d, c), 0)),
                pl.BlockSpec((1, q, LANES), lambda i, c: (i, lat_chunk(d, c), 0)),
                pl.BlockSpec((1, q, LANES), lambda i, c: (i, ctx_chunk(d, c), 0)),
                pl.BlockSpec((1, 2 * hh, q), lambda i, c: (i, 0, lat_chunk(d, c))),
                pl.BlockSpec((1, 2 * hh, q), lambda i, c: (i, 0, ctx_chunk(d, c)))]

    const = lambda i, c: (0, 0)
    in_specs = dir_specs(0) + dir_specs(1) + [
        pl.BlockSpec((2, hh), const), pl.BlockSpec((hh, 2), const),
        pl.BlockSpec((1, SSM_INNER), const), pl.BlockSpec((hh, SSM_INNER), const)]
    out_specs, out_shape = [], []
    for d in range(2):
        out_specs += [pl.BlockSpec((1, q, SSM_INNER), lambda i, c, d=d: (i, lat_chunk(d, c), 0)),
                      pl.BlockSpec((1, q, SSM_INNER), lambda i, c, d=d: (i, ctx_chunk(d, c), 0))]
        out_shape += [jax.ShapeDtypeStruct((b, l, SSM_INNER), F32), jax.ShapeDtypeStruct((b, cl, SSM_INNER), F32)]
    gw = (SSM_HEADS // SSM_GROUPS) * SSM_HEAD_DIM
    data = (xbc_l, xbc_c, dt_l, dt_c, dtt_l, dtt_c)
    yl_f, yc_f, yl_b, yc_b = pl.pallas_call(
        functools.partial(_ssd_kernel, ncc=ncc),
        grid=(b, nc),
        in_specs=in_specs,
        out_specs=out_specs,
        out_shape=out_shape,
        scratch_shapes=[pltpu.VMEM((2, SSM_GROUPS, SSM_STATE, gw), F32)],
        compiler_params=_cparams(("parallel", "arbitrary")),
        name="ssd_scan",
    )(*data, *data, a_log.astype(F32), a_log.astype(F32).T, dskip, expand)
    return (yl_f, yl_b), (yc_f, yc_b)


def _merge_kernel(x_ref, g1_ref, gate_ref, att_ref, ysf_ref, ysb_ref, z_ref, cv_ref, ng_ref,
                  wa_ref, ws_ref, wc_ref, bc_ref, wo_ref, o_ref):
    d = D_MODEL
    ys = (ysf_ref[0] + ysb_ref[0]) * _silu(z_ref[0].astype(F32))
    gsz = SSM_INNER // SSM_GROUPS
    parts = []
    for g in range(SSM_GROUPS):
        yg = ys[:, g * gsz:(g + 1) * gsz]
        parts.append(yg * lax.rsqrt(jnp.mean(yg * yg, axis=-1, keepdims=True) + EPS))
    yn = (jnp.concatenate(parts, axis=-1) * ng_ref[...]).astype(BF16)
    p_s = _dot(yn, ws_ref[...])
    p_a = _dot(att_ref[0], wa_ref[...])
    p_c = _dot(cv_ref[0], wc_ref[...]) + bc_ref[...]
    gl = gate_ref[0].astype(F32)
    m = (_sigmoid(gl[:, 0:d]) * p_a + _sigmoid(gl[:, d:2 * d]) * p_s
         + _sigmoid(gl[:, 2 * d:3 * d]) * p_c).astype(BF16)
    o_ref[0] = x_ref[0] + g1_ref[0] * _dot(m, wo_ref[...])


def _merge(x, g1, packed, y_att, y_ssm2, y_cv, norm_g, wa, ws, wc, bc, wo):
    b, l, d = x.shape
    tm = 256
    bm = g1.shape[0]
    g_map = (lambda i, t: (i, 0, 0)) if bm > 1 else (lambda i, t: (0, 0, 0))
    const = lambda i, t: (0, 0)
    one = pl.Buffered(1)
    in_specs = [pl.BlockSpec((1, tm, d), lambda i, t: (i, t, 0)),
                pl.BlockSpec((1, 1, d), g_map),
                pl.BlockSpec((1, tm, 3 * d), lambda i, t: (i, t, OFF_GATE // (3 * d))),
                pl.BlockSpec((1, tm, NA_WIDTH), lambda i, t: (i, t, 0)),
                pl.BlockSpec((1, tm, SSM_INNER), lambda i, t: (i, t, 0)),
                pl.BlockSpec((1, tm, SSM_INNER), lambda i, t: (i, t, 0)),
                pl.BlockSpec((1, tm, SSM_INNER), lambda i, t: (i, t, OFF_Z // SSM_INNER)),
                pl.BlockSpec((1, tm, CONV_CH), lambda i, t: (i, t, 0)),
                pl.BlockSpec((1, SSM_INNER), const),
                pl.BlockSpec((NA_WIDTH, d), const, pipeline_mode=one),
                pl.BlockSpec((SSM_INNER, d), const, pipeline_mode=one),
                pl.BlockSpec((CONV_CH, d), const, pipeline_mode=one),
                pl.BlockSpec((1, d), const),
                pl.BlockSpec((d, d), const, pipeline_mode=one)]
    return pl.pallas_call(
        _merge_kernel,
        grid=(b, l // tm),
        in_specs=in_specs,
        out_specs=pl.BlockSpec((1, tm, d), lambda i, t: (i, t, 0)),
        out_shape=jax.ShapeDtypeStruct((b, l, d), F32),
        compiler_params=_cparams(("parallel", "parallel")),
        name="merge",
    )(x, g1.reshape(bm, 1, d), packed, y_att, y_ssm2[0], y_ssm2[1], packed, y_cv, norm_g.reshape(1, -1),
      wa, ws, wc, bc.reshape(1, d), wo)


PEER_TQ = 256
NEG_INF = float("-inf")


def _sort_network(n):
    pairs = []

    def merge(lo, hi, r):
        step = r * 2
        if step < hi - lo:
            merge(lo, hi, step)
            merge(lo + r, hi, step)
            pairs.extend((i, i + r) for i in range(lo + r, hi - r, step))
        else:
            pairs.append((lo, lo + r))

    def sort(lo, hi):
        if hi - lo >= 1:
            mid = lo + (hi - lo) // 2
            sort(lo, mid)
            sort(mid + 1, hi)
            merge(lo, hi, 1)

    sort(0, n - 1)
    return pairs


def _exchange(v, i, j):
    a, b = v[i], v[j]
    v[i], v[j] = jnp.maximum(a, b), jnp.minimum(a, b)


def _top_k_sorted_many(problems):
    k = len(problems[0])
    vs = [list(v) for v in problems]
    for i, j in _sort_network(k):
        for v in vs:
            _exchange(v, i, j)
    for shift in (4, 2, 1):
        others = [[pltpu.roll(x, shift, axis=0) for x in v] for v in vs]
        vs = [[jnp.maximum(v[i], o[k - 1 - i]) for i in range(k)]
              for v, o in zip(vs, others)]
        dist = k // 2
        while dist >= 1:
            for i in range(k):
                if i & dist == 0:
                    for v in vs:
                        _exchange(v, i, i + dist)
            dist //= 2
    return vs


def _top_k_sorted(v):
    return _top_k_sorted_many([v])[0]


def _peer_select_kernel(q_ref, keys_ref, n1_ref, e1_ref, r2_ref, e2_ref, s_ref, top_ref):
    tq = PEER_TQ
    k = PEER_TOPK
    sub = lax.broadcasted_iota(jnp.int32, (8, tq), 0)

    def per_head_sets(h, carry):
        scores = []
        for i in (2 * h, 2 * h + 1):
            c0 = pl.multiple_of(i * PEER_KEY_DIM, PEER_KEY_DIM)
            s = _dot_nt(keys_ref[i], q_ref[:, pl.ds(c0, PEER_KEY_DIM)])
            s_ref[i] = s
            scores.append([s[8 * r:8 * r + 8, :] for r in range(PEER_NKEYS // 8)])
        tops = _top_k_sorted_many(scores)
        for n, i in enumerate((2 * h, 2 * h + 1)):
            for r in range(k):
                top_ref[i, r] = tops[n][r]
        return carry

    lax.fori_loop(0, PEER_HEADS, per_head_sets, 0)

    def spread(rows):
        out = rows[7]
        for s in range(6, -1, -1):
            out = jnp.where(sub == s, rows[s], out)
        return out

    def per_head(h, carry):
        t1 = [top_ref[2 * h, r] for r in range(k)]
        t2 = [top_ref[2 * h + 1, r] for r in range(k)]
        p2a, p2b, p1b = spread(t2[:8]), spread(t2[8:]), spread(t1[8:])
        cand = [t1[0] + p2a, t1[0] + p2b] + [t1[i] + p2a for i in range(1, 8)] + [p1b + t2[0]]
        cand += [jnp.full((8, tq), NEG_INF, F32)] * (k - len(cand))
        best = _top_k_sorted(cand)
        z = jnp.ones((8, tq), F32)
        for r in range(1, k):
            z = z + jnp.exp(best[r] - best[0])
        thr, rz = best[k - 1], 1.0 / z
        one, zero = jnp.ones((8, tq), F32), jnp.zeros((8, tq), F32)
        for g2 in range(PEER_NKEYS // 16):
            ranks, e2s = [], []
            for g in (2 * g2, 2 * g2 + 1):
                s1 = s_ref[2 * h, 8 * g:8 * g + 8, :]
                s2 = s_ref[2 * h + 1, 8 * g:8 * g + 8, :]
                n, r = zero, zero
                for jj in range(k):
                    n = n + jnp.where(s1 + t2[jj] >= thr, one, zero)
                    r = r + jnp.where(t2[jj] > s2, one, zero)
                n1_ref[h, g] = n
                e1_ref[h, g] = jnp.exp(s1 - t1[0]) * rz
                ranks.append(r)
                e2s.append(jnp.exp(s2 - t2[0]))
            r2_ref[h, g2] = jnp.concatenate(ranks, axis=0).astype(BF16)
            e2_ref[h, g2] = jnp.concatenate(e2s, axis=0).astype(BF16)
        return carry

    lax.fori_loop(0, PEER_HEADS, per_head, 0)


def _peer_select(q, keys):
    t = q.shape[0]
    tq = PEER_TQ
    nset = 2 * PEER_HEADS
    hh, g8, g16 = PEER_HEADS, PEER_NKEYS // 8, PEER_NKEYS // 16
    return pl.pallas_call(
        _peer_select_kernel,
        grid=(t // tq,),
        in_specs=[pl.BlockSpec((tq, q.shape[1]), lambda i: (i, 0)),
                  pl.BlockSpec((nset, PEER_NKEYS, PEER_KEY_DIM), lambda i: (0, 0, 0))],
        out_specs=[pl.BlockSpec((hh, g8, 8, tq), lambda i: (0, 0, 0, i)),
                   pl.BlockSpec((hh, g8, 8, tq), lambda i: (0, 0, 0, i)),
                   pl.BlockSpec((hh, g16, 16, tq), lambda i: (0, 0, 0, i)),
                   pl.BlockSpec((hh, g16, 16, tq), lambda i: (0, 0, 0, i))],
        out_shape=[jax.ShapeDtypeStruct((hh, g8, 8, t), F32),
                   jax.ShapeDtypeStruct((hh, g8, 8, t), F32),
                   jax.ShapeDtypeStruct((hh, g16, 16, t), BF16),
                   jax.ShapeDtypeStruct((hh, g16, 16, t), BF16)],
        scratch_shapes=[pltpu.VMEM((nset, PEER_NKEYS, tq), F32),
                        pltpu.VMEM((nset, PEER_TOPK, 8, tq), F32)],
        compiler_params=_cparams(("parallel",)),
        name="peer_select",
    )(q, keys)


PEER_TM = 512
PEER_TE = 1024
PEER_KEY_BLOCK = 32
PEER_ROW_BLOCK = 2


def _gelu(x):
    return 0.5 * x * (1.0 + lax.erf(x * np.float32(math.sqrt(0.5))))


def _peer_dense_kernel(h_ref, u_ref, vt_ref, n1_ref, e1_ref, r2_ref, e2_ref, x_ref, g2_ref, o_ref,
                       acc_ref, at_ref, wa_ref, bcn_ref, bce_ref):
    j = pl.program_id(1)
    nsteps = pl.num_programs(1)
    tm = PEER_TM
    te = PEER_TE
    nk = PEER_NKEYS

    @pl.when(j == 0)
    def _():
        acc_ref[...] = jnp.zeros_like(acc_ref)

    na = te // nk
    kb = PEER_KEY_BLOCK
    kg = kb // 16
    nkq = nk // kb

    at_ref[...] = _dot_nt(u_ref[...], h_ref[...])

    for h, al in itertools.product(range(PEER_HEADS), range(na)):
        bcn_ref[h, al] = jnp.broadcast_to(n1_ref[h, j, al:al + 1, :], (16, tm)).astype(BF16)
        bce_ref[h, al] = jnp.broadcast_to(e1_ref[h, j, al:al + 1, :], (16, tm)).astype(BF16)

    def gate_tile(it, carry):
        lc = it // (na // PEER_ROW_BLOCK)
        ab = it % (na // PEER_ROW_BLOCK)
        ls = pl.ds(pl.multiple_of(lc * LANES, LANES), LANES)
        a0 = ab * PEER_ROW_BLOCK
        nt = nkq * kg
        w = {(r, t): jnp.zeros((16, LANES), BF16) for r in range(PEER_ROW_BLOCK) for t in range(nt)}
        for h in range(PEER_HEADS):
            r2 = [r2_ref[h, t, :, ls] for t in range(nt)]
            e2 = [e2_ref[h, t, :, ls] for t in range(nt)]
            for r in range(PEER_ROW_BLOCK):
                n1 = bcn_ref[h, a0 + r, :, ls]
                e1 = bce_ref[h, a0 + r, :, ls]
                for t in range(nt):
                    w[r, t] = w[r, t] + jnp.where(r2[t] < n1, e2[t], jnp.zeros_like(e2[t])) * e1
        for r, kq in itertools.product(range(PEER_ROW_BLOCK), range(nkq)):
            rs = pl.ds(pl.multiple_of((a0 + r) * nk + kq * kb, kb), kb)
            wt = jnp.concatenate([w[r, kq * kg + t] for t in range(kg)], axis=0)
            wa_ref[rs, ls] = (wt.astype(F32) * _gelu(at_ref[rs, ls])).astype(BF16)
        return carry

    lax.fori_loop(0, (tm // LANES) * (na // PEER_ROW_BLOCK), gate_tile, 0)

    acc_ref[...] += _dot(vt_ref[0], wa_ref[...])

    @pl.when(j == nsteps - 1)
    def _():
        o_ref[...] = x_ref[...] + g2_ref[0] * acc_ref[...].T


def _peer_dense(hh, u, vt, sel, x, g2, tiles_per_row):
    t, d = hh.shape
    ne = u.shape[0]
    tm, te = PEER_TM, PEER_TE
    one = pl.Buffered(1)
    na = te // PEER_NKEYS
    hd, g8, g16 = PEER_HEADS, PEER_NKEYS // 8, PEER_NKEYS // 16
    sel_map = lambda i, j: (0, 0, 0, i)
    return pl.pallas_call(
        _peer_dense_kernel,
        grid=(t // tm, ne // te),
        in_specs=[pl.BlockSpec((tm, d), lambda i, j: (i, 0), pipeline_mode=one),
                  pl.BlockSpec((te, d), lambda i, j: (j, 0)),
                  pl.BlockSpec((1, d, te), lambda i, j: (j, 0, 0)),
                  pl.BlockSpec((hd, g8, 8, tm), sel_map, pipeline_mode=one),
                  pl.BlockSpec((hd, g8, 8, tm), sel_map, pipeline_mode=one),
                  pl.BlockSpec((hd, g16, 16, tm), sel_map, pipeline_mode=one),
                  pl.BlockSpec((hd, g16, 16, tm), sel_map, pipeline_mode=one),
                  pl.BlockSpec((tm, d), lambda i, j: (i, 0), pipeline_mode=one),
                  pl.BlockSpec((1, 1, d), lambda i, j: (i // tiles_per_row, 0, 0))],
        out_specs=pl.BlockSpec((tm, d), lambda i, j: (i, 0)),
        out_shape=jax.ShapeDtypeStruct((t, d), F32),
        scratch_shapes=[pltpu.VMEM((d, tm), F32),
                        pltpu.VMEM((te, tm), F32),
                        pltpu.VMEM((te, tm), BF16),
                        pltpu.VMEM((hd, na, 16, tm), BF16),
                        pltpu.VMEM((hd, na, 16, tm), BF16)],
        compiler_params=_cparams(("parallel", "arbitrary")),
        name="peer_dense",
    )(hh, u, vt, *sel, x, g2)


def _peer_block(x, g, shift, scale, gate2, wq, keys, u, vt):
    b, l, d = x.shape
    t = b * l
    x_rows = x if shift.shape[0] > 1 else x.reshape(1, t, d)
    q, hh = _norm_mod_matmul(x_rows, g, shift, scale, wq, BF16, 512, emit_h=True)
    sel = _peer_select(q.reshape(t, -1), keys)
    bm = gate2.shape[0]
    tiles_per_row = (l // PEER_TM) if bm > 1 else (t // PEER_TM)
    out = _peer_dense(hh.reshape(t, d), u, vt, sel, x.reshape(t, d), gate2.reshape(bm, 1, d), tiles_per_row)
    return out.reshape(b, l, d)


def _rmsnorm_kernel(x_ref, g_ref, o_ref):
    x = x_ref[...]
    o_ref[...] = x * lax.rsqrt(jnp.mean(x * x, axis=-1, keepdims=True) + EPS) * g_ref[...]


def _rmsnorm(x, g):
    t, d = x.shape
    tm = 512
    return pl.pallas_call(
        _rmsnorm_kernel,
        grid=(t // tm,),
        in_specs=[pl.BlockSpec((tm, d), lambda i: (i, 0)), pl.BlockSpec((1, d), lambda i: (0, 0))],
        out_specs=pl.BlockSpec((tm, d), lambda i: (i, 0)),
        out_shape=jax.ShapeDtypeStruct((t, d), F32),
        compiler_params=_cparams(("parallel",)),
        name="final_rmsnorm",
    )(x, g.reshape(1, d))


def _pack_w_in(w_in):
    q, k, v, z, xbc, dt, glu, gate = jnp.split(
        w_in, np.cumsum([NA_WIDTH, NA_WIDTH, NA_WIDTH, SSM_INNER, SSM_XBC, 2 * SSM_HEADS, 2 * CONV_CH]).tolist(),
        axis=-1)
    packed = jnp.concatenate([gate, glu, q, k, v, z, xbc], axis=-1).astype(BF16)
    dt_w = jnp.pad(dt, ((0, 0), (0, LANES - 2 * SSM_HEADS))).astype(BF16)
    return packed, dt_w


def _trunk_layer(xl, xc, c_rows, need_ctx, w_mod, b_mod, norm1_g, norm2_g, w_in, na_rpb, na_wo,
                 ssm_conv_w, ssm_conv_b, ssm_dt_bias, ssm_a_log, ssm_d, ssm_norm_g, ssm_wo,
                 cv_dw_w, cv_dw_b, cv_ln_g, cv_ln_b, cv_wo, cv_bo, w_out, peer_wq, peer_keys, peer_u, peer_v):
    b, s, d = xl.shape
    mod = _modulation(c_rows, w_mod, b_mod)
    sh1, sc1, g1, sh2, sc2, g2 = [mod[:b, i * d:(i + 1) * d] for i in range(6)]
    csh1, csc1, cg1, csh2, csc2, cg2 = [mod[b:b + 1, i * d:(i + 1) * d] for i in range(6)]

    w_pack, w_dt = _pack_w_in(w_in)
    dt_bias = jnp.pad(ssm_dt_bias.astype(F32).reshape(-1), (0, LANES - 2 * SSM_HEADS))

    cb, cl, _ = xc.shape
    xc_flat = xc.reshape(1, cb * cl, d)
    p_l = _norm_mod_matmul(xl, norm1_g, sh1, sc1, w_pack, BF16, 512)
    p_c = _norm_mod_matmul(xc_flat, norm1_g, csh1, csc1, w_pack, BF16, 512).reshape(cb, cl, -1)
    dt_l = _norm_mod_matmul(xl, norm1_g, sh1, sc1, w_dt, F32, LANES, softplus_bias=dt_bias)
    dt_c = _norm_mod_matmul(xc_flat, norm1_g, csh1, csc1, w_dt, F32, LANES,
                            softplus_bias=dt_bias).reshape(cb, cl, -1)

    bias_tab = _na_bias_table(na_rpb, s // GRID_W)
    y_att, y_att_c = _neighbourhood_attention(p_l, p_c, bias_tab, need_ctx)

    xbc_l = _dwconv(p_l, OFF_XBC, None, SSM_XBC, ssm_conv_w, ssm_conv_b, None, None, F32)
    xbc_c = _dwconv(p_c, OFF_XBC, None, SSM_XBC, ssm_conv_w, ssm_conv_b, None, None, F32)
    y_ssm, y_ssm_c = _ssd_scan(xbc_l, xbc_c, dt_l, dt_c, ssm_a_log, ssm_d)

    wa, ws, wc, wo = (na_wo.astype(BF16), ssm_wo.astype(BF16), cv_wo.astype(BF16), w_out.astype(BF16))
    y_cv = _dwconv(p_l, OFF_GLU_A, OFF_GLU_G, CONV_CH, cv_dw_w, cv_dw_b, cv_ln_g, cv_ln_b, BF16)
    xl = _merge(xl, g1, p_l, y_att, y_ssm, y_cv, ssm_norm_g, wa, ws, wc, cv_bo, wo)
    if need_ctx:
        y_cv_c = _dwconv(p_c, OFF_GLU_A, OFF_GLU_G, CONV_CH, cv_dw_w, cv_dw_b, cv_ln_g, cv_ln_b, BF16)
        xc = _merge(xc, cg1, p_c, y_att_c, y_ssm_c, y_cv_c, ssm_norm_g, wa, ws, wc, cv_bo, wo)

    wq = peer_wq.astype(BF16)
    keys = peer_keys.astype(BF16).reshape(2 * PEER_HEADS, PEER_NKEYS, PEER_KEY_DIM)
    u = peer_u.astype(BF16)
    vt = peer_v.astype(BF16).reshape(PEER_EXPERTS // PEER_TE, PEER_TE, d).transpose(0, 2, 1)
    xl = _peer_block(xl, norm2_g, sh2, sc2, g2, wq, keys, u, vt)
    if need_ctx:
        xc = _peer_block(xc, norm2_g, csh2, csc2, cg2, wq, keys, u, vt)
    return xl, xc


def kernel(x, c, ctx, c_ctx, w_mod, b_mod, norm1_g, norm2_g, w_in, na_rpb, na_wo, ssm_conv_w, ssm_conv_b,
           ssm_dt_bias, ssm_A_log, ssm_D, ssm_norm_g, ssm_wo, cv_dw_w, cv_dw_b, cv_ln_g, cv_ln_b, cv_wo,
           cv_bo, w_out, peer_wq, peer_keys, peer_u, peer_v, final_norm_g):
    b, s, d = x.shape
    depth = w_mod.shape[0]
    rows = -(-(b + 1) // 8) * 8
    c_rows = jnp.zeros((rows, d), F32).at[:b].set(c).at[b].set(c_ctx)
    xl, xc = x, ctx
    for l in range(depth):
        xl, xc = _trunk_layer(xl, xc, c_rows, l < depth - 1, w_mod[l], b_mod[l], norm1_g[l], norm2_g[l],
                              w_in[l], na_rpb[l], na_wo[l], ssm_conv_w[l], ssm_conv_b[l], ssm_dt_bias[l],
                              ssm_A_log[l], ssm_D[l], ssm_norm_g[l], ssm_wo[l], cv_dw_w[l], cv_dw_b[l],
                              cv_ln_g[l], cv_ln_b[l], cv_wo[l], cv_bo[l], w_out[l],
                              peer_wq[l], peer_keys[l], peer_u[l], peer_v[l])
    return _rmsnorm(xl.reshape(b * s, d), final_norm_g).reshape(b, s, d)
```
